```python
import math
import jax, jax.numpy as jnp
from jax import lax
import numpy as np

D_MODEL = 1024
BATCH = 4
SEQ = 8192
DEPTH = 2

A_WIDTH = D_MODEL // 2
A_HEAD_DIM = 64
A_HEADS = A_WIDTH // A_HEAD_DIM
MOBA_BLOCK = 256
MOBA_TOPK = 3
MOBA_Q_CHUNK = 64
B_WIDTH = D_MODEL - A_WIDTH
B_HEADS = 4
B_HEAD_DIM = B_WIDTH // B_HEADS
MLSTM_CHUNK = 64
CONV_WIDTH = 4
POOL_WINDOWS = (2, 4, 8, 16)
POOL_GROUP = D_MODEL // len(POOL_WINDOWS)
D_FF = int(round(8 * D_MODEL / 3 / 256)) * 256
RMS_EPS = 1e-6
N_EVEN = (DEPTH + 1) // 2
N_ODD = DEPTH // 2
IN_COLS = 3 * A_WIDTH + 4 * B_WIDTH + 2 * B_HEADS
F32 = jnp.float32

kernel_name = 'hybrid_moba_mlstm_pool_macaron'


def rmsnorm(x, g):
    xf = x.astype(F32)
    y = xf * lax.rsqrt(jnp.mean(xf * xf, axis=-1, keepdims=True) + RMS_EPS)
    return (y * g.astype(F32)).astype(x.dtype)


def swiglu(x, w_gate, w_up, w_down):
    return (jax.nn.silu(x @ w_gate) * (x @ w_up)) @ w_down


def alibi_slopes(n_heads):
    return jnp.exp2(-8.0 * jnp.arange(1, n_heads + 1, dtype=F32) / n_heads)


def causal_depthwise_conv(x, w, b):
    s = x.shape[1]
    xp = jnp.pad(x, ((0, 0), (CONV_WIDTH - 1, 0), (0, 0)))
    y = b
    for j in range(CONV_WIDTH):
        y = y + xp[:, j:j + s] * w[j]
    return y


def moba_attention(q, k, v):
    bsz, nh, s, dh = q.shape
    nb = max(-(-s // MOBA_BLOCK), MOBA_TOPK)
    sp = nb * MOBA_BLOCK
    pad = ((0, 0), (0, 0), (0, sp - s), (0, 0))
    q, k, v = jnp.pad(q, pad), jnp.pad(k, pad), jnp.pad(v, pad)
    nqc = sp // MOBA_Q_CHUNK
    scale = dh ** -0.5
    slopes = alibi_slopes(nh)
    k_blk = k.reshape(bsz, nh, nb, MOBA_BLOCK, dh)
    v_blk = v.reshape(bsz, nh, nb, MOBA_BLOCK, dh)
    k_mean = jnp.mean(k_blk.astype(F32), axis=3).astype(k.dtype)
    q_ch = jnp.moveaxis(q.reshape(bsz, nh, nqc, MOBA_Q_CHUNK, dh), 2, 0)
    b_ix = jnp.arange(bsz)[:, None, None, None]
    h_ix = jnp.arange(nh)[None, :, None, None]
    blk_pos = jnp.arange(MOBA_BLOCK)
    blk_ids = jnp.arange(nb)

    def one_chunk(args):
        qc, c = args
        t = c * MOBA_Q_CHUNK + jnp.arange(MOBA_Q_CHUNK)
        cur = (c * MOBA_Q_CHUNK) // MOBA_BLOCK
        gate = jnp.einsum('bhqd,bhnd->bhqn', qc, k_mean).astype(F32)
        gate = jnp.where(blk_ids < cur, gate, -jnp.inf)
        _, sel = lax.top_k(gate, MOBA_TOPK)
        sel_ok = sel < cur
        kg = k_blk[b_ix, h_ix, sel]
        vg = v_blk[b_ix, h_ix, sel]
        pos_past = sel[..., None] * MOBA_BLOCK + blk_pos
        s_past = jnp.einsum('bhqd,bhqkjd->bhqkj', qc, kg).astype(F32) * scale
        s_past = s_past - slopes[:, None, None, None] * (t[:, None, None] - pos_past).astype(F32)
        s_past = jnp.where(sel_ok[..., None], s_past, -jnp.inf)
        s_past = s_past.reshape(bsz, nh, MOBA_Q_CHUNK, MOBA_TOPK * MOBA_BLOCK)
        k_own = lax.dynamic_index_in_dim(k_blk, cur, axis=2, keepdims=False)
        v_own = lax.dynamic_index_in_dim(v_blk, cur, axis=2, keepdims=False)
        pos_own = cur * MOBA_BLOCK + blk_pos
        dist_own = t[:, None] - pos_own[None, :]
        s_own = jnp.einsum('bhqd,bhjd->bhqj', qc, k_own).astype(F32) * scale
        s_own = s_own - slopes[:, None, None] * dist_own.astype(F32)
        s_own = jnp.where(dist_own >= 0, s_own, -jnp.inf)
        p = jax.nn.softmax(jnp.concatenate([s_past, s_own], axis=-1), axis=-1)
        p_past = p[..., :MOBA_TOPK * MOBA_BLOCK].reshape(bsz, nh, MOBA_Q_CHUNK, MOBA_TOPK, MOBA_BLOCK).astype(v.dtype)
        p_own = p[..., MOBA_TOPK * MOBA_BLOCK:].astype(v.dtype)
        return (jnp.einsum('bhqkj,bhqkjd->bhqd', p_past, vg)
                + jnp.einsum('bhqj,bhjd->bhqd', p_own, v_own))

    out = lax.map(one_chunk, (q_ch, jnp.arange(nqc)))
    out = jnp.moveaxis(out, 0, 2).reshape(bsz, nh, sp, dh)
    return out[:, :, :s]


def mlstm_chunkwise(q, k, v, i_pre, f_pre):
    bsz, nh, s, d = q.shape
    L = MLSTM_CHUNK
    nc = s // L
    q = q * (d ** -0.5)
    logf = jax.nn.log_sigmoid(f_pre.astype(F32))
    ig = i_pre.astype(F32)

    def to_chunks(a):
        return jnp.moveaxis(a.reshape(bsz, nh, nc, L, *a.shape[3:]), 2, 0)

    causal = jnp.tril(jnp.ones((L, L), dtype=bool))

    def step(carry, xs):
        C, n, m = carry
        qc, kc, vc, ic, fc = xs
        qf, kf, vf = qc.astype(F32), kc.astype(F32), vc.astype(F32)
        b = jnp.cumsum(fc, axis=-1)
        log_inter = b + m[..., None]
        D = b[..., :, None] - b[..., None, :] + ic[..., None, :]
        D = jnp.where(causal, D, -jnp.inf)
        m_t = jnp.maximum(log_inter, jnp.max(D, axis=-1))
        w_inter = jnp.exp(log_inter - m_t)
        sc = jnp.einsum('bhtd,bhsd->bhts', qf, kf) * jnp.exp(D - m_t[..., None])
        num = (w_inter[..., None] * jnp.einsum('bhtd,bhde->bhte', qf, C)
               + jnp.einsum('bhts,bhse->bhte', sc, vf))
        den = w_inter * jnp.einsum('bhtd,bhd->bht', qf, n) + jnp.sum(sc, axis=-1)
        h = num / jnp.maximum(jnp.abs(den), jnp.exp(-m_t))[..., None]
        bL = b[..., -1]
        log_old = bL + m
        log_new = bL[..., None] - b + ic
        m_new = jnp.maximum(log_old, jnp.max(log_new, axis=-1))
        a_old = jnp.exp(log_old - m_new)
        a_new = jnp.exp(log_new - m_new[..., None])
        C = a_old[..., None, None] * C + jnp.einsum('bhs,bhsd,bhse->bhde', a_new, kf, vf)
        n = a_old[..., None] * n + jnp.einsum('bhs,bhsd->bhd', a_new, kf)
        return (C, n, m_new), h.astype(q.dtype)

    init = (jnp.zeros((bsz, nh, d, d), F32), jnp.zeros((bsz, nh, d), F32), jnp.zeros((bsz, nh), F32))
    _, hs = lax.scan(step, init, (to_chunks(q), to_chunks(k), to_chunks(v), to_chunks(ig), to_chunks(logf)))
    return jnp.moveaxis(hs, 0, 2).reshape(bsz, nh, s, d)


def mixer_moba_mlstm(h, w_in, w_out, g_q, g_k, conv_w, conv_b, b_i, b_f):
    bsz, s, _ = h.shape
    proj = h @ w_in
    offs = [A_WIDTH, 2 * A_WIDTH, 3 * A_WIDTH, 3 * A_WIDTH + 2 * B_WIDTH,
            3 * A_WIDTH + 3 * B_WIDTH, 3 * A_WIDTH + 3 * B_WIDTH + B_HEADS,
            3 * A_WIDTH + 3 * B_WIDTH + 2 * B_HEADS]
    qa, ka, va, qkb, vb, ib, fb, ob = jnp.split(proj, offs, axis=-1)

    def heads(t, nh, dh):
        return t.reshape(bsz, s, nh, dh).transpose(0, 2, 1, 3)

    qa = rmsnorm(heads(qa, A_HEADS, A_HEAD_DIM), g_q)
    ka = rmsnorm(heads(ka, A_HEADS, A_HEAD_DIM), g_k)
    ya = moba_attention(qa, ka, heads(va, A_HEADS, A_HEAD_DIM))
    ya = ya.transpose(0, 2, 1, 3).reshape(bsz, s, A_WIDTH)
    qkb = jax.nn.silu(causal_depthwise_conv(qkb, conv_w, conv_b))
    qb, kb = jnp.split(qkb, 2, axis=-1)
    i_pre = (ib.astype(F32) + b_i.astype(F32)).transpose(0, 2, 1)
    f_pre = (fb.astype(F32) + b_f.astype(F32)).transpose(0, 2, 1)
    hb = mlstm_chunkwise(heads(qb, B_HEADS, B_HEAD_DIM), heads(kb, B_HEADS, B_HEAD_DIM),
                         heads(vb, B_HEADS, B_HEAD_DIM), i_pre, f_pre)
    yb = jax.nn.sigmoid(ob) * hb.transpose(0, 2, 1, 3).reshape(bsz, s, B_WIDTH)
    return jnp.concatenate([ya, yb], axis=-1) @ w_out


def pool_mixer(h, w_grp, scale):
    bsz, s, _ = h.shape
    hf = h.astype(F32)
    cs0 = jnp.pad(jnp.cumsum(hf, axis=1), ((0, 0), (1, 0), (0, 0)))
    t1 = jnp.arange(1, s + 1, dtype=F32)[None, :, None]
    outs = []
    for g, w in enumerate(POOL_WINDOWS):
        sl = slice(g * POOL_GROUP, (g + 1) * POOL_GROUP)
        c = cs0[:, :, sl]
        upper = c[:, 1:]
        lower = jnp.pad(c[:, :s + 1 - w], ((0, 0), (w - 1, 0), (0, 0)))
        outs.append((upper - lower) / jnp.minimum(t1, float(w)) - hf[:, :, sl])
    pooled = jnp.stack(outs, axis=2).astype(h.dtype)
    y = jnp.einsum('bsgc,gcd->bsgd', pooled, w_grp).reshape(bsz, s, D_MODEL)
    return y * scale


def setup_inputs(seed: int = 0) -> dict:
    key = jax.random.key(seed)
    ks = jax.random.split(key, 20)
    nrm = jax.random.normal
    x = nrm(ks[0], (BATCH, SEQ, D_MODEL), F32)
    norm_g = 1.0 + 0.02 * nrm(ks[1], (DEPTH, 3, D_MODEL), F32)
    ffn_w_gate = nrm(ks[2], (DEPTH, 2, D_MODEL, D_FF), F32) * D_MODEL ** -0.5
    ffn_w_up = nrm(ks[3], (DEPTH, 2, D_MODEL, D_FF), F32) * D_MODEL ** -0.5
    ffn_w_down = nrm(ks[4], (DEPTH, 2, D_FF, D_MODEL), F32) * D_FF ** -0.5
    ab_w_in = nrm(ks[5], (N_EVEN, D_MODEL, IN_COLS), F32) * D_MODEL ** -0.5
    ab_w_out = nrm(ks[6], (N_EVEN, A_WIDTH + B_WIDTH, D_MODEL), F32) * (A_WIDTH + B_WIDTH) ** -0.5
    ab_g_q = 1.0 + 0.02 * nrm(ks[7], (N_EVEN, A_HEAD_DIM), F32)
    ab_g_k = 1.0 + 0.02 * nrm(ks[8], (N_EVEN, A_HEAD_DIM), F32)
    ab_conv_w = nrm(ks[9], (N_EVEN, CONV_WIDTH, 2 * B_WIDTH), F32) * CONV_WIDTH ** -0.5
    ab_conv_b = 0.01 * nrm(ks[10], (N_EVEN, 2 * B_WIDTH), F32)
    ab_b_i = 0.1 * nrm(ks[11], (N_EVEN, B_HEADS), F32)
    ab_b_f = jnp.linspace(3.0, 6.0, B_HEADS, dtype=F32)[None, :] + 0.01 * nrm(ks[12], (N_EVEN, B_HEADS), F32)
    pool_w = nrm(ks[13], (N_ODD, len(POOL_WINDOWS), POOL_GROUP, POOL_GROUP), F32) * POOL_GROUP ** -0.5
    pool_scale = 1.0 + 0.02 * nrm(ks[14], (N_ODD, D_MODEL), F32)
    return {'x': x, 'norm_g': norm_g, 'ffn_w_gate': ffn_w_gate, 'ffn_w_up': ffn_w_up,
            'ffn_w_down': ffn_w_down, 'ab_w_in': ab_w_in, 'ab_w_out': ab_w_out,
            'ab_g_q': ab_g_q, 'ab_g_k': ab_g_k, 'ab_conv_w': ab_conv_w, 'ab_conv_b': ab_conv_b,
            'ab_b_i': ab_b_i, 'ab_b_f': ab_b_f, 'pool_w': pool_w, 'pool_scale': pool_scale}


def reference(x, norm_g, ffn_w_gate, ffn_w_up, ffn_w_down, ab_w_in, ab_w_out, ab_g_q, ab_g_k,
              ab_conv_w, ab_conv_b, ab_b_i, ab_b_f, pool_w, pool_scale):
    for layer in range(DEPTH):
        x = x + 0.5 * swiglu(rmsnorm(x, norm_g[layer, 0]), ffn_w_gate[layer, 0],
                             ffn_w_up[layer, 0], ffn_w_down[layer, 0])
        h = rmsnorm(x, norm_g[layer, 1])
        if layer % 2 == 0:
            e = layer // 2
            x = x + mixer_moba_mlstm(h, ab_w_in[e], ab_w_out[e], ab_g_q[e], ab_g_k[e],
                                     ab_conv_w[e], ab_conv_b[e], ab_b_i[e], ab_b_f[e])
        else:
            o = layer // 2
            x = x + pool_mixer(h, pool_w[o], pool_scale[o])
        x = x + 0.5 * swiglu(rmsnorm(x, norm_g[layer, 2]), ffn_w_gate[layer, 1],
                             ffn_w_up[layer, 1], ffn_w_down[layer, 1])
    return x
```

```python
import functools

import jax
import jax.numpy as jnp
from jax import lax
from jax.experimental import pallas as pl
from jax.experimental.pallas import tpu as pltpu

F32 = jnp.float32
BF16 = jnp.bfloat16

RMS_EPS = 1e-6
A_HEADS = 8
A_HEAD_DIM = 64
MOBA_BLOCK = 256
MOBA_TOPK = 3
B_HEADS = 4
B_HEAD_DIM = 128
CONV_WIDTH = 4
POOL_WINDOWS = (2, 4, 8, 16)
POOL_HALO = 16
MLSTM_KERNEL_CHUNK = 256
LANES = 128
MASK_VALUE = -1e30

FFN_TOKEN_TILE = 512
FFN_HIDDEN_TILE = 256
PROJ_TOKEN_TILE = 512
POOL_TOKEN_TILE = 512
VMEM_LIMIT_BYTES = 56 * 1024 * 1024


def _compiler_params(semantics):
    return pltpu.CompilerParams(dimension_semantics=semantics,
                                vmem_limit_bytes=VMEM_LIMIT_BYTES)


def _rms_normalize(x, gain):
    ms = jnp.mean(x * x, axis=-1, keepdims=True)
    return x * lax.rsqrt(ms + RMS_EPS) * gain


def _dot(a, b):
    return jnp.dot(a, b, preferred_element_type=F32)


def _dot_nt(a, b):
    return lax.dot_general(a, b, (((1,), (1,)), ((), ())), preferred_element_type=F32)


def _dot_tn(a, b):
    return lax.dot_general(a, b, (((0,), (0,)), ((), ())), preferred_element_type=F32)


def _split_bf16(x, parts):
    out = []
    rem = x
    for _ in range(parts):
        p = rem.astype(BF16)
        out.append(p)
        rem = rem - p.astype(F32)
    return out


def _ffn_kernel(x_ref, g_ref, wg_ref, wu_ref, wd_ref, o_ref, h_ref, acc_ref):
    f = pl.program_id(1)

    @pl.when(f == 0)
    def _():
        h_ref[...] = _rms_normalize(x_ref[...], g_ref[...]).astype(BF16)
        acc_ref[...] = jnp.zeros_like(acc_ref)

    h = h_ref[...]
    gate = _dot(h, wg_ref[...])
    up = _dot(h, wu_ref[...])
    act = (gate * jax.nn.sigmoid(gate) * up).astype(BF16)
    acc_ref[...] += _dot(act, wd_ref[...])

    @pl.when(f == pl.num_programs(1) - 1)
    def _():
        o_ref[...] = x_ref[...] + 0.5 * acc_ref[...]


def _ffn(x, gain, w_gate, w_up, w_down):
    t, d = x.shape
    d_ff = w_gate.shape[1]
    tm, tf = FFN_TOKEN_TILE, FFN_HIDDEN_TILE
    assert t % tm == 0 and d_ff % tf == 0
    return pl.pallas_call(
        _ffn_kernel,
        grid=(t // tm, d_ff // tf),
        in_specs=[
            pl.BlockSpec((tm, d), lambda i, f: (i, 0)),
            pl.BlockSpec((1, d), lambda i, f: (0, 0)),
            pl.BlockSpec((d, tf), lambda i, f: (0, f)),
            pl.BlockSpec((d, tf), lambda i, f: (0, f)),
            pl.BlockSpec((tf, d), lambda i, f: (f, 0)),
        ],
        out_specs=pl.BlockSpec((tm, d), lambda i, f: (i, 0)),
        out_shape=jax.ShapeDtypeStruct((t, d), F32),
        scratch_shapes=[pltpu.VMEM((tm, d), BF16), pltpu.VMEM((tm, d), F32)],
        compiler_params=_compiler_params(("parallel", "arbitrary")),
        name="ffn",
    )(x, gain.reshape(1, d), w_gate.astype(BF16), w_up.astype(BF16), w_down.astype(BF16))


def _inproj_kernel(x_ref, g_ref, w_ref, wgate_ref, gq_ref, gk_ref, grp_ref, gbias_ref,
                   qa_ref, ka_ref, va_ref, qkb_ref, vb_ref, ob_ref, gates_ref, *, a_width, b_width):
    h = _rms_normalize(x_ref[...], g_ref[...]).astype(BF16)
    grp = grp_ref[...]

    def head_norm(cols, gain_ref):
        y = _dot(h, w_ref[:, cols:cols + a_width])
        sq_hi, sq_lo = _split_bf16(y * y, 2)
        ssq = _dot(sq_hi, grp) + _dot(sq_lo, grp)
        return y * lax.rsqrt(ssq * (1.0 / A_HEAD_DIM) + RMS_EPS) * gain_ref[...]

    qa_ref[...] = (head_norm(0, gq_ref) * (A_HEAD_DIM ** -0.5)).astype(BF16)
    ka_ref[...] = head_norm(a_width, gk_ref).astype(BF16)
    va_ref[...] = _dot(h, w_ref[:, 2 * a_width:3 * a_width]).astype(BF16)
    c0 = 3 * a_width
    qkb_ref[...] = _dot(h, w_ref[:, c0:c0 + 2 * b_width]).astype(BF16)
    c0 += 2 * b_width
    vb_ref[...] = _dot(h, w_ref[:, c0:c0 + b_width]).astype(BF16)
    c0 += b_width
    ob_ref[...] = _dot(h, w_ref[:, c0:c0 + b_width]).astype(BF16)
    gates = _dot(h, wgate_ref[...]) + gbias_ref[...]
    gates_ref[...] = gates[:, :2 * B_HEADS]


def _inproj(x, gain, w_in, g_q, g_k, b_i, b_f):
    t, d = x.shape
    a_width = A_HEADS * A_HEAD_DIM
    b_width = B_HEADS * B_HEAD_DIM
    n_main = 3 * a_width + 4 * b_width
    o_ib = 3 * a_width + 3 * b_width
    tm = PROJ_TOKEN_TILE
    assert t % tm == 0
    w_main = jnp.concatenate([w_in[:, :o_ib], w_in[:, o_ib + 2 * B_HEADS:]], axis=1).astype(BF16)
    w_gate = jnp.pad(w_in[:, o_ib:o_ib + 2 * B_HEADS], ((0, 0), (0, LANES - 2 * B_HEADS))).astype(BF16)
    gate_bias = jnp.pad(jnp.concatenate([b_i, b_f]).astype(F32), (0, LANES - 2 * B_HEADS)).reshape(1, LANES)
    head_id = jnp.arange(a_width) // A_HEAD_DIM
    grp = (head_id[:, None] == head_id[None, :]).astype(BF16)
    gq = jnp.tile(g_q.astype(F32), A_HEADS).reshape(1, a_width)
    gk = jnp.tile(g_k.astype(F32), A_HEADS).reshape(1, a_width)
    const = lambda i: (0, 0)
    row = lambda i: (i, 0)
    outs = pl.pallas_call(
        functools.partial(_inproj_kernel, a_width=a_width, b_width=b_width),
        grid=(t // tm,),
        in_specs=[
            pl.BlockSpec((tm, d), row),
            pl.BlockSpec((1, d), const),
            pl.BlockSpec((d, n_main), const),
            pl.BlockSpec((d, LANES), const),
            pl.BlockSpec((1, a_width), const),
            pl.BlockSpec((1, a_width), const),
            pl.BlockSpec((a_width, a_width), const),
            pl.BlockSpec((1, LANES), const),
        ],
        out_specs=[
            pl.BlockSpec((tm, a_width), row),
            pl.BlockSpec((tm, a_width), row),
            pl.BlockSpec((tm, a_width), row),
            pl.BlockSpec((tm, 2 * b_width), row),
            pl.BlockSpec((tm, b_width), row),
            pl.BlockSpec((tm, b_width), row),
            pl.BlockSpec((tm, 2 * B_HEADS), row),
        ],
        out_shape=[
            jax.ShapeDtypeStruct((t, a_width), BF16),
            jax.ShapeDtypeStruct((t, a_width), BF16),
            jax.ShapeDtypeStruct((t, a_width), BF16),
            jax.ShapeDtypeStruct((t, 2 * b_width), BF16),
            jax.ShapeDtypeStruct((t, b_width), BF16),
            jax.ShapeDtypeStruct((t, b_width), BF16),
            jax.ShapeDtypeStruct((t, 2 * B_HEADS), F32),
        ],
        compiler_params=_compiler_params(("parallel",)),
        name="inproj",
    )(x, gain.reshape(1, d), w_main, w_gate, gq, gk, grp, gate_bias)
    return outs


def _moba_kernel(slope_ref, q_ref, k_ref, v_ref, o_ref, kaug_ref, kmean_ref, *, nb):
    hp = pl.program_id(1)
    c = pl.program_id(2)
    blk = MOBA_BLOCK
    lane = lax.broadcasted_iota(jnp.int32, (blk, LANES), 1)
    row = lax.broadcasted_iota(jnp.int32, (blk, LANES), 0)

    @pl.when(c == 0)
    def _build_key_side():
        kmean_ref[...] = jnp.zeros_like(kmean_ref)

        def body(j, carry):
            start = pl.multiple_of(j * blk, blk)
            kb = k_ref[pl.ds(start, blk), :].astype(F32)
            kmean_ref[pl.ds(j, 1), :] = jnp.mean(kb, axis=0, keepdims=True)
            pos = (row + j * blk).astype(F32)
            for hh in range(2):
                slope = slope_ref[pl.ds(2 * hp + hh, 1), :]
                aug0 = A_HEAD_DIM * (1 - hh)
                p1, p2, p3 = _split_bf16(slope * pos, 3)
                rel = lane - aug0
                aug = jnp.where(rel == j, 1.0, 0.0)
                aug = jnp.where(rel == nb, p1.astype(F32), aug)
                aug = jnp.where(rel == nb + 1, p2.astype(F32), aug)
                aug = jnp.where(rel == nb + 2, p3.astype(F32), aug)
                aug = jnp.where((rel >= nb + 3) & (rel < nb + 6), 1.0, aug)
                is_data = (lane >= A_HEAD_DIM * hh) & (lane < A_HEAD_DIM * (hh + 1))
                kaug_ref[hh, pl.ds(start, blk), :] = jnp.where(is_data, kb, aug).astype(BF16)
            return carry

        lax.fori_loop(0, nb, body, 0)

    q2 = q_ref[...].astype(F32)
    kmean = kmean_ref[...].astype(BF16)
    tpos = (row + c * blk).astype(F32)
    own_start = pl.multiple_of(c * blk, blk)
    outs = []
    for hh in range(2):
        aug0 = A_HEAD_DIM * (1 - hh)
        is_data = (lane >= A_HEAD_DIM * hh) & (lane < A_HEAD_DIM * (hh + 1))
        q_data = jnp.where(is_data, q2, 0.0)
        gate = _dot_nt(q_data.astype(BF16), kmean)
        gate = jnp.where(lane < c, gate, -jnp.inf)
        lane_f = lane.astype(F32)
        chosen = lane == c
        for _ in range(MOBA_TOPK):
            best = jnp.max(gate, axis=-1, keepdims=True)
            first = jnp.min(jnp.where(gate == best, lane_f, float(LANES)), axis=-1, keepdims=True)
            pick = lane_f == first
            chosen = chosen | (pick & (lane < c))
            gate = jnp.where(pick, -jnp.inf, gate)
        block_bias = jnp.where(chosen, 0.0, MASK_VALUE)
        if aug0:
            block_bias = pltpu.roll(block_bias, aug0, axis=1)
        slope = slope_ref[pl.ds(2 * hp + hh, 1), :]
        t1, t2, t3 = _split_bf16(-slope * tpos, 3)
        rel = lane - aug0
        aug = jnp.where((rel >= 0) & (rel < nb), block_bias, 0.0)
        aug = jnp.where((rel >= nb) & (rel < nb + 3), 1.0, aug)
        aug = jnp.where(rel == nb + 3, t1.astype(F32), aug)
        aug = jnp.where(rel == nb + 4, t2.astype(F32), aug)
        aug = jnp.where(rel == nb + 5, t3.astype(F32), aug)
        q_aug = jnp.where(is_data, q2, aug).astype(BF16)

        s = _dot_nt(q_aug, kaug_ref[hh, pl.ds(own_start, blk), :])
        rr = lax.broadcasted_iota(jnp.int32, (blk, blk), 0)
        cc = lax.broadcasted_iota(jnp.int32, (blk, blk), 1)
        s = jnp.where(rr >= cc, s, MASK_VALUE)
        m0 = jnp.max(s, axis=-1, keepdims=True)
        p = jnp.exp(s - m0)
        l0 = jnp.sum(p, axis=-1, keepdims=True)
        acc0 = _dot(p.astype(BF16), v_ref[pl.ds(own_start, blk), :])

        def past_block(j, carry, hh=hh, q_aug=q_aug):
            m, l, acc = carry
            start = pl.multiple_of(j * blk, blk)
            s = _dot_nt(q_aug, kaug_ref[hh, pl.ds(start, blk), :])
            m_new = jnp.maximum(m, jnp.max(s, axis=-1, keepdims=True))
            alpha = jnp.exp(m - m_new)
            p = jnp.exp(s - m_new)
            l = alpha * l + jnp.sum(p, axis=-1, keepdims=True)
            acc = alpha * acc + _dot(p.astype(BF16), v_ref[pl.ds(start, blk), :])
            return m_new, l, acc

        _, l, acc = lax.fori_loop(0, c, past_block, (m0, l0, acc0))
        outs.append(acc / l)
    o_ref[...] = jnp.where(lane < A_HEAD_DIM, outs[0], outs[1]).astype(o_ref.dtype)


def _moba(qa, ka, va, bsz, seq):
    t, a_width = qa.shape
    blk = MOBA_BLOCK
    assert seq % blk == 0 and seq // blk >= MOBA_TOPK
    nb = seq // blk
    assert nb + 6 <= A_HEAD_DIM
    n_pairs = a_width // LANES
    slopes = jnp.exp2(-8.0 * jnp.arange(1, A_HEADS + 1, dtype=F32) / A_HEADS)
    slope_tbl = jnp.broadcast_to(slopes[:, None], (A_HEADS, LANES))
    return pl.pallas_call(
        functools.partial(_moba_kernel, nb=nb),
        grid=(bsz, n_pairs, nb),
        in_specs=[
            pl.BlockSpec((A_HEADS, LANES), lambda b, hp, c: (0, 0)),
            pl.BlockSpec((blk, LANES), lambda b, hp, c: (b * nb + c, hp)),
            pl.BlockSpec((seq, LANES), lambda b, hp, c: (b, hp)),
            pl.BlockSpec((seq, LANES), lambda b, hp, c: (b, hp)),
        ],
        out_specs=pl.BlockSpec((blk, LANES), lambda b, hp, c: (b * nb + c, hp)),
        out_shape=jax.ShapeDtypeStruct((t, a_width), BF16),
        scratch_shapes=[pltpu.VMEM((2, seq, LANES), BF16), pltpu.VMEM((LANES, LANES), F32)],
        compiler_params=_compiler_params(("parallel", "parallel", "arbitrary")),
        name="moba",
    )(slope_tbl, qa, ka, va)


def _mlstm_kernel(q_ref, k_ref, v_ref, ob_ref, gcol_ref, grow_ref, cwq_ref, cwk_ref, cbq_ref, cbk_ref,
                  o_ref, qbuf_ref, kbuf_ref, c_ref, m_ref):
    hd = pl.program_id(1)
    ci = pl.program_id(2)
    L, d = q_ref.shape
    halo = 8

    @pl.when(ci == 0)
    def _():
        qbuf_ref[0:halo, :] = jnp.zeros((halo, d), F32)
        kbuf_ref[0:halo, :] = jnp.zeros((halo, d), F32)
        c_ref[...] = jnp.zeros_like(c_ref)
        m_ref[...] = jnp.zeros_like(m_ref)

    def conv_silu(x_ref, buf_ref, w_ref, b_ref):
        buf_ref[halo:halo + L, :] = x_ref[...].astype(F32)
        y = b_ref[...]
        for j in range(CONV_WIDTH):
            off = halo - (CONV_WIDTH - 1) + j
            y = y + buf_ref[off:off + L, :] * w_ref[j:j + 1, :]
        buf_ref[0:halo, :] = buf_ref[L:L + halo, :]
        return y * jax.nn.sigmoid(y)

    q = (conv_silu(q_ref, qbuf_ref, cwq_ref, cbq_ref) * (d ** -0.5)).astype(BF16)
    k = conv_silu(k_ref, kbuf_ref, cwk_ref, cbk_ref)
    lane = lax.broadcasted_iota(jnp.int32, (L, d), 1)
    v_aug = jnp.concatenate([v_ref[...], jnp.where(lane == 0, 1.0, 0.0).astype(BF16)], axis=1)

    gcol = gcol_ref[...]
    hsel = lax.broadcasted_iota(jnp.int32, gcol.shape, 1)
    i_col = jnp.sum(jnp.where(hsel == hd, gcol, 0.0), axis=1, keepdims=True)
    f_col = jnp.sum(jnp.where(hsel == hd + B_HEADS, gcol, 0.0), axis=1, keepdims=True)
    i_row = grow_ref[0, pl.ds(hd, 1), :]
    f_row = grow_ref[0, pl.ds(hd + B_HEADS, 1), :]

    def log_sigmoid(z):
        return jnp.minimum(z, 0.0) - jnp.log1p(jnp.exp(-jnp.abs(z)))

    rr = lax.broadcasted_iota(jnp.int32, (L, L), 0)
    cc = lax.broadcasted_iota(jnp.int32, (L, L), 1)
    causal = rr >= cc
    lf_col = log_sigmoid(f_col)
    lf_row = log_sigmoid(f_row)
    tri_low = causal.astype(BF16)
    tri_up = (rr <= cc).astype(BF16)
    lane_l = lax.broadcasted_iota(jnp.int32, (L, LANES), 1)
    c_hi, c_lo = _split_bf16(jnp.where(lane_l == 0, lf_col, 0.0), 2)
    b_col = (_dot(tri_low, c_hi) + _dot(tri_low, c_lo))[:, 0:1]
    sub8 = lax.broadcasted_iota(jnp.int32, (8, L), 0)
    r_hi, r_lo = _split_bf16(jnp.where(sub8 == 0, lf_row, 0.0), 2)
    b_row = (_dot(r_hi, tri_up) + _dot(r_lo, tri_up))[0:1, :]

    m_prev = m_ref[...]
    log_inter = b_col + m_prev
    dmat = jnp.where(causal, b_col - b_row + i_row, -jnp.inf)
    m_t = jnp.maximum(log_inter, jnp.max(dmat, axis=-1, keepdims=True))
    w_inter = jnp.exp(log_inter - m_t)
    kb16 = k.astype(BF16)
    sc = _dot_nt(q, kb16) * jnp.exp(dmat - m_t)
    c_prev = c_ref[...]
    num_aug = w_inter * _dot(q, c_prev.astype(BF16)) + _dot(sc.astype(BF16), v_aug)
    num = num_aug[:, :d]
    den = num_aug[:, d:d + 1]
    hidden = num / jnp.maximum(jnp.abs(den), jnp.exp(-m_t))
    o_ref[...] = (jax.nn.sigmoid(ob_ref[...].astype(F32)) * hidden).astype(o_ref.dtype)

    b_last = b_col[L - 1:L, :]
    log_old = b_last + m_prev
    log_new = b_last - b_col + i_col
    m_new = jnp.maximum(log_old, jnp.max(log_new, axis=0, keepdims=True))
    a_old = jnp.exp(log_old - m_new)
    a_new = jnp.exp(log_new - m_new)
    c_ref[...] = a_old * c_prev + _dot_tn((a_new * k).astype(BF16), v_aug)
    m_ref[...] = m_new


def _mlstm(qkb, vb, ob, gates, conv_w, conv_b, bsz, seq):
    t, b_width = vb.shape
    d = B_HEAD_DIM
    L = min(MLSTM_KERNEL_CHUNK, seq)
    assert seq % L == 0 and d == LANES
    nc = seq // L
    gates_row = gates.reshape(bsz, seq, 2 * B_HEADS).transpose(0, 2, 1)
    cw = conv_w.astype(F32)
    cb = conv_b.astype(F32).reshape(1, 2 * b_width)
    tok = lambda off: (lambda b, h, c: (b * nc + c, h + off))
    return pl.pallas_call(
        _mlstm_kernel,
        grid=(bsz, B_HEADS, nc),
        in_specs=[
            pl.BlockSpec((L, d), tok(0)),
            pl.BlockSpec((L, d), tok(B_HEADS)),
            pl.BlockSpec((L, d), tok(0)),
            pl.BlockSpec((L, d), tok(0)),
            pl.BlockSpec((L, 2 * B_HEADS), lambda b, h, c: (b * nc + c, 0)),
            pl.BlockSpec((1, 2 * B_HEADS, L), lambda b, h, c: (b, 0, c)),
            pl.BlockSpec((CONV_WIDTH, d), lambda b, h, c: (0, h)),
            pl.BlockSpec((CONV_WIDTH, d), lambda b, h, c: (0, h + B_HEADS)),
            pl.BlockSpec((1, d), lambda b, h, c: (0, h)),
            pl.BlockSpec((1, d), lambda b, h, c: (0, h + B_HEADS)),
        ],
        out_specs=pl.BlockSpec((L, d), tok(0)),
        out_shape=jax.ShapeDtypeStruct((t, b_width), BF16),
        scratch_shapes=[
            pltpu.VMEM((L + 8, d), F32),
            pltpu.VMEM((L + 8, d), F32),
            pltpu.VMEM((d, 2 * d), F32),
            pltpu.VMEM((1, 1), F32),
        ],
        compiler_params=_compiler_params(("parallel", "parallel", "arbitrary")),
        name="mlstm",
    )(qkb, qkb, vb, ob, gates, gates_row, cw, cw, cb, cb)


def _outproj_kernel(x_ref, ya_ref, yb_ref, wa_ref, wb_ref, o_ref):
    o_ref[...] = x_ref[...] + _dot(ya_ref[...], wa_ref[...]) + _dot(yb_ref[...], wb_ref[...])


def _outproj(x, ya, yb, w_out):
    t, d = x.shape
    a_width = ya.shape[1]
    b_width = yb.shape[1]
    tm = PROJ_TOKEN_TILE
    row = lambda i: (i, 0)
    const = lambda i: (0, 0)
    return pl.pallas_call(
        _outproj_kernel,
        grid=(t // tm,),
        in_specs=[
            pl.BlockSpec((tm, d), row),
            pl.BlockSpec((tm, a_width), row),
            pl.BlockSpec((tm, b_width), row),
            pl.BlockSpec((a_width, d), const),
            pl.BlockSpec((b_width, d), const),
        ],
        out_specs=pl.BlockSpec((tm, d), row),
        out_shape=jax.ShapeDtypeStruct((t, d), F32),
        compiler_params=_compiler_params(("parallel",)),
        name="outproj",
    )(x, ya, yb, w_out[:a_width].astype(BF16), w_out[a_width:].astype(BF16))


def _pool_kernel(x_ref, halo_ref, g_ref, w_ref, scale_ref, o_ref, buf_ref, *, tiles_per_seq):
    i = pl.program_id(0)
    tm, d = x_ref.shape
    grp = d // len(POOL_WINDOWS)
    x = x_ref[...]
    seq_tile = i % tiles_per_seq
    halo_h = _rms_normalize(halo_ref[...], g_ref[...])
    buf_ref[0:POOL_HALO, :] = jnp.where(seq_tile == 0, 0.0, halo_h)
    h = _rms_normalize(x, g_ref[...])
    buf_ref[POOL_HALO:POOL_HALO + tm, :] = h
    t1 = (lax.broadcasted_iota(jnp.int32, (tm, 1), 0) + seq_tile * tm + 1).astype(F32)
    for gi, win in enumerate(POOL_WINDOWS):
        c0 = gi * grp
        total = h[:, c0:c0 + grp]
        for back in range(1, win):
            total = total + buf_ref[POOL_HALO - back:POOL_HALO - back + tm, c0:c0 + grp]
        pooled = total / jnp.minimum(t1, float(win)) - h[:, c0:c0 + grp]
        y = _dot(pooled.astype(BF16), w_ref[gi])
        o_ref[:, c0:c0 + grp] = x[:, c0:c0 + grp] + y * scale_ref[:, c0:c0 + grp]


def _pool(x, gain, w_grp, scale, seq):
    t, d = x.shape
    tm = min(POOL_TOKEN_TILE, seq)
    assert seq % tm == 0 and tm % POOL_HALO == 0
    n_grp, grp, _ = w_grp.shape
    halo_blocks = tm // POOL_HALO
    return pl.pallas_call(
        functools.partial(_pool_kernel, tiles_per_seq=seq // tm),
        grid=(t // tm,),
        in_specs=[
            pl.BlockSpec((tm, d), lambda i: (i, 0)),
            pl.BlockSpec((POOL_HALO, d), lambda i: (jnp.maximum(i * halo_blocks - 1, 0), 0)),
            pl.BlockSpec((1, d), lambda i: (0, 0)),
            pl.BlockSpec((n_grp, grp, grp), lambda i: (0, 0, 0)),
            pl.BlockSpec((1, d), lambda i: (0, 0)),
        ],
        out_specs=pl.BlockSpec((tm, d), lambda i: (i, 0)),
        out_shape=jax.ShapeDtypeStruct((t, d), F32),
        scratch_shapes=[pltpu.VMEM((tm + POOL_HALO, d), F32)],
        compiler_params=_compiler_params(("parallel",)),
        name="pool",
    )(x, x, gain.reshape(1, d), w_grp.astype(BF16), scale.astype(F32).reshape(1, d))


def _mixer_moba_mlstm(x, gain, w_in, w_out, g_q, g_k, conv_w, conv_b, b_i, b_f, bsz, seq):
    qa, ka, va, qkb, vb, ob, gates = _inproj(x, gain, w_in, g_q, g_k, b_i, b_f)
    ya = _moba(qa, ka, va, bsz, seq)
    yb = _mlstm(qkb, vb, ob, gates, conv_w, conv_b, bsz, seq)
    return _outproj(x, ya, yb, w_out)


def kernel(x, norm_g, ffn_w_gate, ffn_w_up, ffn_w_down, ab_w_in, ab_w_out, ab_g_q, ab_g_k,
           ab_conv_w, ab_conv_b, ab_b_i, ab_b_f, pool_w, pool_scale):
    bsz, seq, d = x.shape
    depth = norm_g.shape[0]
    y = x.reshape(bsz * seq, d)
    for layer in range(depth):
        y = _ffn(y, norm_g[layer, 0], ffn_w_gate[layer, 0], ffn_w_up[layer, 0], ffn_w_down[layer, 0])
        if layer % 2 == 0:
            e = layer // 2
            y = _mixer_moba_mlstm(y, norm_g[layer, 1], ab_w_in[e], ab_w_out[e], ab_g_q[e], ab_g_k[e],
                                  ab_conv_w[e], ab_conv_b[e], ab_b_i[e], ab_b_f[e], bsz, seq)
        else:
            o = layer // 2
            y = _pool(y, norm_g[layer, 1], pool_w[o], pool_scale[o], seq)
        y = _ffn(y, norm_g[layer, 2], ffn_w_gate[layer, 1], ffn_w_up[layer, 1], ffn_w_down[layer, 1])
    return y.reshape(bsz, seq, d)
```

```python
import functools

import jax
import jax.numpy as jnp
from jax import lax
from jax.experimental import pallas as pl
from jax.experimental.pallas import tpu as pltpu

F32 = jnp.float32
BF16 = jnp.bfloat16

RMS_EPS = 1e-6
A_HEADS = 8
A_HEAD_DIM = 64
MOBA_BLOCK = 256
MOBA_TOPK = 3
MOBA_KEY_GROUP = 4
B_HEADS = 4
B_HEAD_DIM = 128
CONV_WIDTH = 4
POOL_WINDOWS = (2, 4, 8, 16)
POOL_HALO = 16
MLSTM_KERNEL_CHUNK = 256
LANES = 128
MASK_VALUE = -1e30
LOG2E = 1.4426950408889634
V_AUG_ROWS = A_HEAD_DIM + 16

FFN_TOKEN_TILE = 512
FFN_HIDDEN_TILE = 256
PROJ_TOKEN_TILE = 512
POOL_TOKEN_TILE = 512
VMEM_LIMIT_BYTES = 56 * 1024 * 1024


def _compiler_params(semantics):
    return pltpu.CompilerParams(dimension_semantics=semantics,
                                vmem_limit_bytes=VMEM_LIMIT_BYTES)


def _rms_normalize(x, gain):
    ms = jnp.mean(x * x, axis=-1, keepdims=True)
    return x * lax.rsqrt(ms + RMS_EPS) * gain


def _dot(a, b):
    return jnp.dot(a, b, preferred_element_type=F32)


def _dot_nt(a, b):
    return lax.dot_general(a, b, (((1,), (1,)), ((), ())), preferred_element_type=F32)


def _dot_tn(a, b):
    return lax.dot_general(a, b, (((0,), (0,)), ((), ())), preferred_element_type=F32)


def _split_bf16(x, parts):
    out = []
    rem = x
    for _ in range(parts):
        p = rem.astype(BF16)
        out.append(p)
        rem = rem - p.astype(F32)
    return out


def _ffn_kernel(x_ref, g_ref, wg_ref, wu_ref, wd_ref, o_ref, *, hidden_tile):
    x = x_ref[...]
    h = _rms_normalize(x, g_ref[...]).astype(BF16)
    d_ff = wg_ref.shape[1]
    acc = None
    for c0 in range(0, d_ff, hidden_tile):
        gate = _dot(h, wg_ref[:, c0:c0 + hidden_tile])
        up = _dot(h, wu_ref[:, c0:c0 + hidden_tile])
        act = (gate * jax.nn.sigmoid(gate) * up).astype(BF16)
        part = _dot(act, wd_ref[c0:c0 + hidden_tile, :])
        acc = part if acc is None else acc + part
    o_ref[...] = x + 0.5 * acc


def _resident(shape):
    return pl.BlockSpec(shape, lambda *_: (0,) * len(shape), pipeline_mode=pl.Buffered(1))


def _ffn(x, gain, w_gate, w_up, w_down):
    t, d = x.shape
    d_ff = w_gate.shape[1]
    tm, tf = FFN_TOKEN_TILE, FFN_HIDDEN_TILE
    assert t % tm == 0 and d_ff % tf == 0
    return pl.pallas_call(
        functools.partial(_ffn_kernel, hidden_tile=tf),
        grid=(t // tm,),
        in_specs=[
            pl.BlockSpec((tm, d), lambda i: (i, 0)),
            _resident((1, d)),
            _resident((d, d_ff)),
            _resident((d, d_ff)),
            _resident((d_ff, d)),
        ],
        out_specs=pl.BlockSpec((tm, d), lambda i: (i, 0)),
        out_shape=jax.ShapeDtypeStruct((t, d), F32),
        compiler_params=_compiler_params(("parallel",)),
        name="ffn",
    )(x, gain.reshape(1, d), w_gate.astype(BF16), w_up.astype(BF16), w_down.astype(BF16))


def _inproj_kernel(x_ref, g_ref, w_ref, wvt_ref, wgate_ref, gq_ref, gk_ref, grp_ref, gbias_ref,
                   qa_ref, ka_ref, vat_ref, qkb_ref, vb_ref, ob_ref, gates_ref, *, a_width, b_width):
    h = _rms_normalize(x_ref[...], g_ref[...]).astype(BF16)
    grp = grp_ref[...]

    def head_norm(cols, gain_ref):
        y = _dot(h, w_ref[:, cols:cols + a_width])
        sq_hi, sq_lo = _split_bf16(y * y, 2)
        ssq = _dot(sq_hi, grp) + _dot(sq_lo, grp)
        return y * lax.rsqrt(ssq * (1.0 / A_HEAD_DIM) + RMS_EPS) * gain_ref[...]

    qa_ref[...] = (head_norm(0, gq_ref) * (A_HEAD_DIM ** -0.5 * LOG2E)).astype(BF16)
    ka_ref[...] = head_norm(a_width, gk_ref).astype(BF16)
    vat_ref[...] = _dot_nt(wvt_ref[...], h).astype(BF16)
    c0 = 2 * a_width
    qkb_ref[...] = _dot(h, w_ref[:, c0:c0 + 2 * b_width]).astype(BF16)
    c0 += 2 * b_width
    vb_ref[...] = _dot(h, w_ref[:, c0:c0 + b_width]).astype(BF16)
    c0 += b_width
    ob_ref[...] = _dot(h, w_ref[:, c0:c0 + b_width]).astype(BF16)
    gates = _dot(h, wgate_ref[...]) + gbias_ref[...]
    gates_ref[...] = gates[:, :2 * B_HEADS]


def _inproj(x, gain, w_in, g_q, g_k, b_i, b_f):
    t, d = x.shape
    a_width = A_HEADS * A_HEAD_DIM
    b_width = B_HEADS * B_HEAD_DIM
    n_main = 2 * a_width + 4 * b_width
    o_ib = 3 * a_width + 3 * b_width
    tm = PROJ_TOKEN_TILE
    assert t % tm == 0
    w_main = jnp.concatenate([w_in[:, :2 * a_width], w_in[:, 3 * a_width:o_ib],
                              w_in[:, o_ib + 2 * B_HEADS:]], axis=1).astype(BF16)
    w_vt = w_in[:, 2 * a_width:3 * a_width].T.astype(BF16)
    w_gate = jnp.pad(w_in[:, o_ib:o_ib + 2 * B_HEADS], ((0, 0), (0, LANES - 2 * B_HEADS))).astype(BF16)
    gate_bias = jnp.pad(jnp.concatenate([b_i, b_f]).astype(F32), (0, LANES - 2 * B_HEADS)).reshape(1, LANES)
    head_id = jnp.arange(a_width) // A_HEAD_DIM
    grp = (head_id[:, None] == head_id[None, :]).astype(BF16)
    gq = jnp.tile(g_q.astype(F32), A_HEADS).reshape(1, a_width)
    gk = jnp.tile(g_k.astype(F32), A_HEADS).reshape(1, a_width)
    const = lambda i: (0, 0)
    row = lambda i: (i, 0)
    outs = pl.pallas_call(
        functools.partial(_inproj_kernel, a_width=a_width, b_width=b_width),
        grid=(t // tm,),
        in_specs=[
            pl.BlockSpec((tm, d), row),
            pl.BlockSpec((1, d), const),
            pl.BlockSpec((d, n_main), const),
            pl.BlockSpec((a_width, d), const),
            pl.BlockSpec((d, LANES), const),
            pl.BlockSpec((1, a_width), const),
            pl.BlockSpec((1, a_width), const),
            pl.BlockSpec((a_width, a_width), const),
            pl.BlockSpec((1, LANES), const),
        ],
        out_specs=[
            pl.BlockSpec((tm, a_width), row),
            pl.BlockSpec((tm, a_width), row),
            pl.BlockSpec((a_width, tm), lambda i: (0, i)),
            pl.BlockSpec((tm, 2 * b_width), row),
            pl.BlockSpec((tm, b_width), row),
            pl.BlockSpec((tm, b_width), row),
            pl.BlockSpec((tm, 2 * B_HEADS), row),
        ],
        out_shape=[
            jax.ShapeDtypeStruct((t, a_width), BF16),
            jax.ShapeDtypeStruct((t, a_width), BF16),
            jax.ShapeDtypeStruct((a_width, t), BF16),
            jax.ShapeDtypeStruct((t, 2 * b_width), BF16),
            jax.ShapeDtypeStruct((t, b_width), BF16),
            jax.ShapeDtypeStruct((t, b_width), BF16),
            jax.ShapeDtypeStruct((t, 2 * B_HEADS), F32),
        ],
        compiler_params=_compiler_params(("parallel",)),
        name="inproj",
    )(x, gain.reshape(1, d), w_main, w_vt, w_gate, gq, gk, grp, gate_bias)
    return outs


def _moba_kernel(slope_ref, q_ref, k_ref, vt_ref, o_ref, kaug_ref, vaug_ref, kmean_ref, qaug_ref, acc_ref,
                 *, nb, grp):
    hp = pl.program_id(1)
    c = pl.program_id(2)
    blk = MOBA_BLOCK
    lane = lax.broadcasted_iota(jnp.int32, (blk, LANES), 1)
    row = lax.broadcasted_iota(jnp.int32, (blk, LANES), 0)

    @pl.when(c == 0)
    def _build_key_value_side():
        kmean_ref[...] = jnp.zeros_like(kmean_ref)
        ones_rows = jnp.where(lax.broadcasted_iota(jnp.int32, (V_AUG_ROWS - A_HEAD_DIM, blk), 0) == 0,
                              1.0, 0.0).astype(BF16)
        for hh in range(2):
            for j in range(nb):
                g, off = divmod(j, grp)
                vaug_ref[hh, g, 0:A_HEAD_DIM, off * blk:(off + 1) * blk] = vt_ref[
                    A_HEAD_DIM * hh:A_HEAD_DIM * (hh + 1), j * blk:(j + 1) * blk]
                vaug_ref[hh, g, A_HEAD_DIM:V_AUG_ROWS, off * blk:(off + 1) * blk] = ones_rows

        def body(j, carry):
            start = pl.multiple_of(j * blk, blk)
            kb = k_ref[pl.ds(start, blk), :].astype(F32)
            kmean_ref[pl.ds(j, 1), :] = jnp.mean(kb, axis=0, keepdims=True)
            pos = (row + j * blk).astype(F32)
            for hh in range(2):
                slope = slope_ref[pl.ds(2 * hp + hh, 1), :] * LOG2E
                aug0 = A_HEAD_DIM * (1 - hh)
                p1, p2, p3 = _split_bf16(slope * pos, 3)
                rel = lane - aug0
                aug = jnp.where(rel == j, 1.0, 0.0)
                aug = jnp.where(rel == nb, p1.astype(F32), aug)
                aug = jnp.where(rel == nb + 1, p2.astype(F32), aug)
                aug = jnp.where(rel == nb + 2, p3.astype(F32), aug)
                aug = jnp.where((rel >= nb + 3) & (rel < nb + 6), 1.0, aug)
                is_data = (lane >= A_HEAD_DIM * hh) & (lane < A_HEAD_DIM * (hh + 1))
                kaug_ref[hh, pl.ds(start, blk), :] = jnp.where(is_data, kb, aug).astype(BF16)
            return carry

        lax.fori_loop(0, nb, body, 0)

    q2 = q_ref[...].astype(F32)
    kmean_hi, kmean_lo = _split_bf16(kmean_ref[...], 2)
    tpos = (row + c * blk).astype(F32)
    own_grp = c // grp
    own_start = pl.multiple_of(own_grp * (grp * blk), grp * blk)
    key_ix = lax.broadcasted_iota(jnp.int32, (grp * blk, blk), 0) + own_grp * (grp * blk)
    qry_ix = lax.broadcasted_iota(jnp.int32, (grp * blk, blk), 1) + c * blk
    m_init = []
    for hh in range(2):
        aug0 = A_HEAD_DIM * (1 - hh)
        is_data = (lane >= A_HEAD_DIM * hh) & (lane < A_HEAD_DIM * (hh + 1))
        q_data = jnp.where(is_data, q2, 0.0)
        q_data = q_data.astype(BF16)
        gate = _dot_nt(q_data, kmean_hi) + _dot_nt(q_data, kmean_lo)
        gate = jnp.where(lane < c, gate, -jnp.inf)
        lane_f = lane.astype(F32)
        chosen = lane == c
        for _ in range(MOBA_TOPK):
            best = jnp.max(gate, axis=-1, keepdims=True)
            first = jnp.min(jnp.where(gate == best, lane_f, float(LANES)), axis=-1, keepdims=True)
            pick = lane_f == first
            chosen = chosen | (pick & (lane < c))
            gate = jnp.where(pick, -jnp.inf, gate)
        block_bias = jnp.where(chosen, 0.0, MASK_VALUE)
        if aug0:
            block_bias = pltpu.roll(block_bias, aug0, axis=1)
        slope = slope_ref[pl.ds(2 * hp + hh, 1), :] * LOG2E
        t1, t2, t3 = _split_bf16(-slope * tpos, 3)
        rel = lane - aug0
        aug = jnp.where((rel >= 0) & (rel < nb), block_bias, 0.0)
        aug = jnp.where((rel >= nb) & (rel < nb + 3), 1.0, aug)
        aug = jnp.where(rel == nb + 3, t1.astype(F32), aug)
        aug = jnp.where(rel == nb + 4, t2.astype(F32), aug)
        aug = jnp.where(rel == nb + 5, t3.astype(F32), aug)
        q_aug = jnp.where(is_data, q2, aug).astype(BF16)
        qaug_ref[hh] = q_aug

        s = _dot_nt(kaug_ref[hh, pl.ds(own_start, grp * blk), :], q_aug)
        s = jnp.where(key_ix <= qry_ix, s, MASK_VALUE)
        m0 = jnp.max(s, axis=0, keepdims=True)
        p = jnp.exp2(s - m0)
        acc_ref[hh] = _dot(vaug_ref[hh, own_grp], p.astype(BF16))
        m_init.append(m0)

    def past_group(g, ms):
        start = pl.multiple_of(g * (grp * blk), grp * blk)
        scores = [_dot_nt(kaug_ref[hh, pl.ds(start, grp * blk), :], qaug_ref[hh]) for hh in range(2)]
        new_ms = []
        for hh in range(2):
            s = scores[hh]
            m_new = jnp.maximum(ms[hh], jnp.max(s, axis=0, keepdims=True))
            alpha = jnp.exp2(ms[hh] - m_new)
            p = jnp.exp2(s - m_new)
            acc_ref[hh] = alpha * acc_ref[hh] + _dot(vaug_ref[hh, g], p.astype(BF16))
            new_ms.append(m_new)
        return tuple(new_ms)

    lax.fori_loop(0, own_grp, past_group, tuple(m_init))
    halves = []
    for hh in range(2):
        acc = acc_ref[hh]
        halves.append(acc[0:A_HEAD_DIM, :] / acc[A_HEAD_DIM:A_HEAD_DIM + 1, :])
    o_ref[...] = jnp.concatenate(halves, axis=0).T.astype(o_ref.dtype)


def _moba(qa, ka, vat, bsz, seq):
    t, a_width = qa.shape
    blk = MOBA_BLOCK
    assert seq % blk == 0 and seq // blk >= MOBA_TOPK
    nb = seq // blk
    assert nb + 6 <= A_HEAD_DIM
    grp = MOBA_KEY_GROUP if nb % MOBA_KEY_GROUP == 0 else 1
    n_pairs = a_width // LANES
    slopes = jnp.exp2(-8.0 * jnp.arange(1, A_HEADS + 1, dtype=F32) / A_HEADS)
    slope_tbl = jnp.broadcast_to(slopes[:, None], (A_HEADS, LANES))
    return pl.pallas_call(
        functools.partial(_moba_kernel, nb=nb, grp=grp),
        grid=(bsz, n_pairs, nb),
        in_specs=[
            pl.BlockSpec((A_HEADS, LANES), lambda b, hp, c: (0, 0)),
            pl.BlockSpec((blk, LANES), lambda b, hp, c: (b * nb + c, hp)),
            pl.BlockSpec((seq, LANES), lambda b, hp, c: (b, hp)),
            pl.BlockSpec((LANES, seq), lambda b, hp, c: (hp, b)),
        ],
        out_specs=pl.BlockSpec((blk, LANES), lambda b, hp, c: (b * nb + c, hp)),
        out_shape=jax.ShapeDtypeStruct((t, a_width), BF16),
        scratch_shapes=[
            pltpu.VMEM((2, seq, LANES), BF16),
            pltpu.VMEM((2, nb // grp, V_AUG_ROWS, grp * blk), BF16),
            pltpu.VMEM((LANES, LANES), F32),
            pltpu.VMEM((2, blk, LANES), BF16),
            pltpu.VMEM((2, V_AUG_ROWS, blk), F32),
        ],
        compiler_params=_compiler_params(("parallel", "parallel", "arbitrary")),
        name="moba",
    )(slope_tbl, qa, ka, vat)


def _mlstm_kernel(q_ref, k_ref, v_ref, ob_ref, gcol_ref, grow_ref, cwq_ref, cwk_ref, cbq_ref, cbk_ref,
                  o_ref, qbuf_ref, kbuf_ref, c_ref, m_ref):
    hd = pl.program_id(1)
    ci = pl.program_id(2)
    L, d = q_ref.shape
    halo = 8

    @pl.when(ci == 0)
    def _():
        qbuf_ref[0:halo, :] = jnp.zeros((halo, d), F32)
        kbuf_ref[0:halo, :] = jnp.zeros((halo, d), F32)
        c_ref[...] = jnp.zeros_like(c_ref)
        m_ref[...] = jnp.zeros_like(m_ref)

    def conv_silu(x_ref, buf_ref, w_ref, b_ref):
        buf_ref[halo:halo + L, :] = x_ref[...].astype(F32)
        y = b_ref[...]
        for j in range(CONV_WIDTH):
            off = halo - (CONV_WIDTH - 1) + j
            y = y + buf_ref[off:off + L, :] * w_ref[j:j + 1, :]
        buf_ref[0:halo, :] = buf_ref[L:L + halo, :]
        return y * jax.nn.sigmoid(y)

    q = (conv_silu(q_ref, qbuf_ref, cwq_ref, cbq_ref) * (d ** -0.5)).astype(BF16)
    k = conv_silu(k_ref, kbuf_ref, cwk_ref, cbk_ref)
    lane = lax.broadcasted_iota(jnp.int32, (L, d), 1)
    v_aug = jnp.concatenate([v_ref[...], jnp.where(lane == 0, 1.0, 0.0).astype(BF16)], axis=1)

    gcol = gcol_ref[...]
    hsel = lax.broadcasted_iota(jnp.int32, gcol.shape, 1)
    i_col = jnp.sum(jnp.where(hsel == hd, gcol, 0.0), axis=1, keepdims=True)
    f_col = jnp.sum(jnp.where(hsel == hd + B_HEADS, gcol, 0.0), axis=1, keepdims=True)
    i_row = grow_ref[0, pl.ds(hd, 1), :]
    f_row = grow_ref[0, pl.ds(hd + B_HEADS, 1), :]

    def log_sigmoid(z):
        return jnp.minimum(z, 0.0) - jnp.log1p(jnp.exp(-jnp.abs(z)))

    rr = lax.broadcasted_iota(jnp.int32, (L, L), 0)
    cc = lax.broadcasted_iota(jnp.int32, (L, L), 1)
    causal = rr >= cc
    lf_col = log_sigmoid(f_col)
    lf_row = log_sigmoid(f_row)
    tri_low = causal.astype(BF16)
    tri_up = (rr <= cc).astype(BF16)
    lane_l = lax.broadcasted_iota(jnp.int32, (L, LANES), 1)
    c_hi, c_lo = _split_bf16(jnp.where(lane_l == 0, lf_col, 0.0), 2)
    b_col = (_dot(tri_low, c_hi) + _dot(tri_low, c_lo))[:, 0:1]
    sub8 = lax.broadcasted_iota(jnp.int32, (8, L), 0)
    r_hi, r_lo = _split_bf16(jnp.where(sub8 == 0, lf_row, 0.0), 2)
    b_row = (_dot(r_hi, tri_up) + _dot(r_lo, tri_up))[0:1, :]

    m_prev = m_ref[...]
    log_inter = b_col + m_prev
    dmat = jnp.where(causal, b_col - b_row + i_row, -jnp.inf)
    m_t = jnp.maximum(log_inter, jnp.max(dmat, axis=-1, keepdims=True))
    w_inter = jnp.exp(log_inter - m_t)
    kb16 = k.astype(BF16)
    sc = _dot_nt(q, kb16) * jnp.exp(dmat - m_t)
    c_prev = c_ref[...]
    num_aug = w_inter * _dot(q, c_prev.astype(BF16)) + _dot(sc.astype(BF16), v_aug)
    num = num_aug[:, :d]
    den = num_aug[:, d:d + 1]
    hidden = num / jnp.maximum(jnp.abs(den), jnp.exp(-m_t))
    o_ref[...] = (jax.nn.sigmoid(ob_ref[...].astype(F32)) * hidden).astype(o_ref.dtype)

    b_last = b_col[L - 1:L, :]
    log_old = b_last + m_prev
    log_new = b_last - b_col + i_col
    m_new = jnp.maximum(log_old, jnp.max(log_new, axis=0, keepdims=True))
    a_old = jnp.exp(log_old - m_new)
    a_new = jnp.exp(log_new - m_new)
    c_ref[...] = a_old * c_prev + _dot_tn((a_new * k).astype(BF16), v_aug)
    m_ref[...] = m_new


def _mlstm(qkb, vb, ob, gates, conv_w, conv_b, bsz, seq):
    t, b_width = vb.shape
    d = B_HEAD_DIM
    L = min(MLSTM_KERNEL_CHUNK, seq)
    assert seq % L == 0 and d == LANES
    nc = seq // L
    gates_row = gates.reshape(bsz, seq, 2 * B_HEADS).transpose(0, 2, 1)
    cw = conv_w.astype(F32)
    cb = conv_b.astype(F32).reshape(1, 2 * b_width)
    tok = lambda off: (lambda b, h, c: (b * nc + c, h + off))
    return pl.pallas_call(
        _mlstm_kernel,
        grid=(bsz, B_HEADS, nc),
        in_specs=[
            pl.BlockSpec((L, d), tok(0)),
            pl.BlockSpec((L, d), tok(B_HEADS)),
            pl.BlockSpec((L, d), tok(0)),
            pl.BlockSpec((L, d), tok(0)),
            pl.BlockSpec((L, 2 * B_HEADS), lambda b, h, c: (b * nc + c, 0)),
            pl.BlockSpec((1, 2 * B_HEADS, L), lambda b, h, c: (b, 0, c)),
            pl.BlockSpec((CONV_WIDTH, d), lambda b, h, c: (0, h)),
            pl.BlockSpec((CONV_WIDTH, d), lambda b, h, c: (0, h + B_HEADS)),
            pl.BlockSpec((1, d), lambda b, h, c: (0, h)),
            pl.BlockSpec((1, d), lambda b, h, c: (0, h + B_HEADS)),
        ],
        out_specs=pl.BlockSpec((L, d), tok(0)),
        out_shape=jax.ShapeDtypeStruct((t, b_width), BF16),
        scratch_shapes=[
            pltpu.VMEM((L + 8, d), F32),
            pltpu.VMEM((L + 8, d), F32),
            pltpu.VMEM((d, 2 * d), F32),
            pltpu.VMEM((1, 1), F32),
        ],
        compiler_params=_compiler_params(("parallel", "parallel", "arbitrary")),
        name="mlstm",
    )(qkb, qkb, vb, ob, gates, gates_row, cw, cw, cb, cb)


def _outproj_kernel(x_ref, ya_ref, yb_ref, wa_ref, wb_ref, o_ref):
    o_ref[...] = x_ref[...] + _dot(ya_ref[...], wa_ref[...]) + _dot(yb_ref[...], wb_ref[...])


def _outproj(x, ya, yb, w_out):
    t, d = x.shape
    a_width = ya.shape[1]
    b_width = yb.shape[1]
    tm = PROJ_TOKEN_TILE
    row = lambda i: (i, 0)
    const = lambda i: (0, 0)
    return pl.pallas_call(
        _outproj_kernel,
        grid=(t // tm,),
        in_specs=[
            pl.BlockSpec((tm, d), row),
            pl.BlockSpec((tm, a_width), row),
            pl.BlockSpec((tm, b_width), row),
            pl.BlockSpec((a_width, d), const),
            pl.BlockSpec((b_width, d), const),
        ],
        out_specs=pl.BlockSpec((tm, d), row),
        out_shape=jax.ShapeDtypeStruct((t, d), F32),
        compiler_params=_compiler_params(("parallel",)),
        name="outproj",
    )(x, ya, yb, w_out[:a_width].astype(BF16), w_out[a_width:].astype(BF16))


def _pool_kernel(x_ref, halo_ref, g_ref, w_ref, scale_ref, o_ref, buf_ref, *, tiles_per_seq):
    i = pl.program_id(0)
    tm, d = x_ref.shape
    grp = d // len(POOL_WINDOWS)
    x = x_ref[...]
    seq_tile = i % tiles_per_seq
    halo_h = _rms_normalize(halo_ref[...], g_ref[...])
    buf_ref[0:POOL_HALO, :] = jnp.where(seq_tile == 0, 0.0, halo_h)
    h = _rms_normalize(x, g_ref[...])
    buf_ref[POOL_HALO:POOL_HALO + tm, :] = h
    t1 = (lax.broadcasted_iota(jnp.int32, (tm, 1), 0) + seq_tile * tm + 1).astype(F32)
    for gi, win in enumerate(POOL_WINDOWS):
        c0 = gi * grp
        total = h[:, c0:c0 + grp]
        for back in range(1, win):
            total = total + buf_ref[POOL_HALO - back:POOL_HALO - back + tm, c0:c0 + grp]
        pooled = total / jnp.minimum(t1, float(win)) - h[:, c0:c0 + grp]
        y = _dot(pooled.astype(BF16), w_ref[gi])
        o_ref[:, c0:c0 + grp] = x[:, c0:c0 + grp] + y * scale_ref[:, c0:c0 + grp]


def _pool(x, gain, w_grp, scale, seq):
    t, d = x.shape
    tm = min(POOL_TOKEN_TILE, seq)
    assert seq % tm == 0 and tm % POOL_HALO == 0
    n_grp, grp, _ = w_grp.shape
    halo_blocks = tm // POOL_HALO
    return pl.pallas_call(
        functools.partial(_pool_kernel, tiles_per_seq=seq // tm),
        grid=(t // tm,),
        in_specs=[
            pl.BlockSpec((tm, d), lambda i: (i, 0)),
            pl.BlockSpec((POOL_HALO, d), lambda i: (jnp.maximum(i * halo_blocks - 1, 0), 0)),
            pl.BlockSpec((1, d), lambda i: (0, 0)),
            pl.BlockSpec((n_grp, grp, grp), lambda i: (0, 0, 0)),
            pl.BlockSpec((1, d), lambda i: (0, 0)),
        ],
        out_specs=pl.BlockSpec((tm, d), lambda i: (i, 0)),
        out_shape=jax.ShapeDtypeStruct((t, d), F32),
        scratch_shapes=[pltpu.VMEM((tm + POOL_HALO, d), F32)],
        compiler_params=_compiler_params(("parallel",)),
        name="pool",
    )(x, x, gain.reshape(1, d), w_grp.astype(BF16), scale.astype(F32).reshape(1, d))


def _mixer_moba_mlstm(x, gain, w_in, w_out, g_q, g_k, conv_w, conv_b, b_i, b_f, bsz, seq):
    qa, ka, vat, qkb, vb, ob, gates = _inproj(x, gain, w_in, g_q, g_k, b_i, b_f)
    ya = _moba(qa, ka, vat, bsz, seq)
    yb = _mlstm(qkb, vb, ob, gates, conv_w, conv_b, bsz, seq)
    return _outproj(x, ya, yb, w_out)


def kernel(x, norm_g, ffn_w_gate, ffn_w_up, ffn_w_down, ab_w_in, ab_w_out, ab_g_q, ab_g_k,
           ab_conv_w, ab_conv_b, ab_b_i, ab_b_f, pool_w, pool_scale):
    bsz, seq, d = x.shape
    depth = norm_g.shape[0]
    y = x.reshape(bsz * seq, d)
    for layer in range(depth):
        y = _ffn(y, norm_g[layer, 0], ffn_w_gate[layer, 0], ffn_w_up[layer, 0], ffn_w_down[layer, 0])
        if layer % 2 == 0:
            e = layer // 2
            y = _mixer_moba_mlstm(y, norm_g[layer, 1], ab_w_in[e], ab_w_out[e], ab_g_q[e], ab_g_k[e],
                                  ab_conv_w[e], ab_conv_b[e], ab_b_i[e], ab_b_f[e], bsz, seq)
        else:
            o = layer // 2
            y = _pool(y, norm_g[layer, 1], pool_w[o], pool_scale[o], seq)
        y = _ffn(y, norm_g[layer, 2], ffn_w_gate[layer, 1], ffn_w_up[layer, 1], ffn_w_down[layer, 1])
    return y.reshape(bsz, seq, d)
```

```python
import functools

import jax
import jax.numpy as jnp
from jax import lax
from jax.experimental import pallas as pl
from jax.experimental.pallas import tpu as pltpu

F32 = jnp.float32
BF16 = jnp.bfloat16

RMS_EPS = 1e-6
A_HEADS = 8
A_HEAD_DIM = 64
MOBA_BLOCK = 256
MOBA_TOPK = 3
MOBA_KEY_GROUP = 4
MOBA_HEADS_PER_STEP = 4
B_HEADS = 4
B_HEAD_DIM = 128
CONV_WIDTH = 4
POOL_WINDOWS = (2, 4, 8, 16)
POOL_HALO = 16
MLSTM_KERNEL_CHUNK = 256
LANES = 128
MASK_VALUE = -1e30
LOG2E = 1.4426950408889634
V_AUG_ROWS = A_HEAD_DIM + 16

FFN_TOKEN_TILE = 512
FFN_HIDDEN_TILE = 256
PROJ_TOKEN_TILE = 512
POOL_TOKEN_TILE = 512
VMEM_LIMIT_BYTES = 56 * 1024 * 1024


def _compiler_params(semantics):
    return pltpu.CompilerParams(dimension_semantics=semantics,
                                vmem_limit_bytes=VMEM_LIMIT_BYTES)


def _rms_normalize(x, gain):
    ms = jnp.mean(x * x, axis=-1, keepdims=True)
    return x * lax.rsqrt(ms + RMS_EPS) * gain


def _dot(a, b):
    return jnp.dot(a, b, preferred_element_type=F32)


def _dot_nt(a, b):
    return lax.dot_general(a, b, (((1,), (1,)), ((), ())), preferred_element_type=F32)


def _dot_tn(a, b):
    return lax.dot_general(a, b, (((0,), (0,)), ((), ())), preferred_element_type=F32)


def _split_bf16(x, parts):
    out = []
    rem = x
    for _ in range(parts):
        p = rem.astype(BF16)
        out.append(p)
        rem = rem - p.astype(F32)
    return out


def _ffn_kernel(x_ref, g_ref, wg_ref, wu_ref, wd_ref, o_ref, *, hidden_tile):
    x = x_ref[...]
    h = _rms_normalize(x, g_ref[...]).astype(BF16)
    d_ff = wg_ref.shape[1]
    acc = None
    for c0 in range(0, d_ff, hidden_tile):
        gate = _dot(h, wg_ref[:, c0:c0 + hidden_tile])
        up = _dot(h, wu_ref[:, c0:c0 + hidden_tile])
        act = (gate * jax.nn.sigmoid(gate) * up).astype(BF16)
        part = _dot(act, wd_ref[c0:c0 + hidden_tile, :])
        acc = part if acc is None else acc + part
    o_ref[...] = x + 0.5 * acc


def _resident(shape):
    return pl.BlockSpec(shape, lambda *_: (0,) * len(shape), pipeline_mode=pl.Buffered(1))


def _ffn(x, gain, w_gate, w_up, w_down):
    t, d = x.shape
    d_ff = w_gate.shape[1]
    tm, tf = FFN_TOKEN_TILE, FFN_HIDDEN_TILE
    assert t % tm == 0 and d_ff % tf == 0
    return pl.pallas_call(
        functools.partial(_ffn_kernel, hidden_tile=tf),
        grid=(t // tm,),
        in_specs=[
            pl.BlockSpec((tm, d), lambda i: (i, 0)),
            _resident((1, d)),
            _resident((d, d_ff)),
            _resident((d, d_ff)),
            _resident((d_ff, d)),
        ],
        out_specs=pl.BlockSpec((tm, d), lambda i: (i, 0)),
        out_shape=jax.ShapeDtypeStruct((t, d), F32),
        compiler_params=_compiler_params(("parallel",)),
        name="ffn",
    )(x, gain.reshape(1, d), w_gate.astype(BF16), w_up.astype(BF16), w_down.astype(BF16))


def _inproj_kernel(x_ref, g_ref, w_ref, wqt_ref, wvt_ref, wgate_ref, gqcol_ref, gk_ref, grp_ref, gbias_ref,
                   qat_ref, ka_ref, vat_ref, qkb_ref, vb_ref, ob_ref, gates_ref, *, a_width, b_width):
    h = _rms_normalize(x_ref[...], g_ref[...]).astype(BF16)

    qt = _dot_nt(wqt_ref[...], h)
    for hd in range(a_width // A_HEAD_DIM):
        rows = slice(hd * A_HEAD_DIM, (hd + 1) * A_HEAD_DIM)
        y = qt[rows, :]
        ms = jnp.mean(y * y, axis=0, keepdims=True)
        gain = gqcol_ref[rows, :] * (A_HEAD_DIM ** -0.5 * LOG2E)
        qat_ref[rows, :] = (y * lax.rsqrt(ms + RMS_EPS) * gain).astype(BF16)

    y = _dot(h, w_ref[:, 0:a_width])
    sq_hi, sq_lo = _split_bf16(y * y, 2)
    ssq = _dot(sq_hi, grp_ref[...]) + _dot(sq_lo, grp_ref[...])
    ka_ref[...] = (y * lax.rsqrt(ssq * (1.0 / A_HEAD_DIM) + RMS_EPS) * gk_ref[...]).astype(BF16)

    vat_ref[...] = _dot_nt(wvt_ref[...], h).astype(BF16)
    c0 = a_width
    qkb_ref[...] = _dot(h, w_ref[:, c0:c0 + 2 * b_width]).astype(BF16)
    c0 += 2 * b_width
    vb_ref[...] = _dot(h, w_ref[:, c0:c0 + b_width]).astype(BF16)
    c0 += b_width
    ob_ref[...] = _dot(h, w_ref[:, c0:c0 + b_width]).astype(BF16)
    gates = _dot(h, wgate_ref[...]) + gbias_ref[...]
    gates_ref[...] = gates[:, :2 * B_HEADS]


def _inproj(x, gain, w_in, g_q, g_k, b_i, b_f):
    t, d = x.shape
    a_width = A_HEADS * A_HEAD_DIM
    b_width = B_HEADS * B_HEAD_DIM
    n_main = a_width + 4 * b_width
    o_ib = 3 * a_width + 3 * b_width
    tm = PROJ_TOKEN_TILE
    assert t % tm == 0
    w_main = jnp.concatenate([w_in[:, a_width:2 * a_width], w_in[:, 3 * a_width:o_ib],
                              w_in[:, o_ib + 2 * B_HEADS:]], axis=1).astype(BF16)
    w_qt = w_in[:, :a_width].T.astype(BF16)
    w_vt = w_in[:, 2 * a_width:3 * a_width].T.astype(BF16)
    w_gate = jnp.pad(w_in[:, o_ib:o_ib + 2 * B_HEADS], ((0, 0), (0, LANES - 2 * B_HEADS))).astype(BF16)
    gate_bias = jnp.pad(jnp.concatenate([b_i, b_f]).astype(F32), (0, LANES - 2 * B_HEADS)).reshape(1, LANES)
    head_id = jnp.arange(a_width) // A_HEAD_DIM
    grp = (head_id[:, None] == head_id[None, :]).astype(BF16)
    gq_col = jnp.tile(g_q.astype(F32), A_HEADS).reshape(a_width, 1)
    gk = jnp.tile(g_k.astype(F32), A_HEADS).reshape(1, a_width)
    const = lambda i: (0, 0)
    row = lambda i: (i, 0)
    col = lambda i: (0, i)
    outs = pl.pallas_call(
        functools.partial(_inproj_kernel, a_width=a_width, b_width=b_width),
        grid=(t // tm,),
        in_specs=[
            pl.BlockSpec((tm, d), row),
            pl.BlockSpec((1, d), const),
            pl.BlockSpec((d, n_main), const),
            pl.BlockSpec((a_width, d), const),
            pl.BlockSpec((a_width, d), const),
            pl.BlockSpec((d, LANES), const),
            pl.BlockSpec((a_width, 1), const),
            pl.BlockSpec((1, a_width), const),
            pl.BlockSpec((a_width, a_width), const),
            pl.BlockSpec((1, LANES), const),
        ],
        out_specs=[
            pl.BlockSpec((a_width, tm), col),
            pl.BlockSpec((tm, a_width), row),
            pl.BlockSpec((a_width, tm), col),
            pl.BlockSpec((tm, 2 * b_width), row),
            pl.BlockSpec((tm, b_width), row),
            pl.BlockSpec((tm, b_width), row),
            pl.BlockSpec((tm, 2 * B_HEADS), row),
        ],
        out_shape=[
            jax.ShapeDtypeStruct((a_width, t), BF16),
            jax.ShapeDtypeStruct((t, a_width), BF16),
            jax.ShapeDtypeStruct((a_width, t), BF16),
            jax.ShapeDtypeStruct((t, 2 * b_width), BF16),
            jax.ShapeDtypeStruct((t, b_width), BF16),
            jax.ShapeDtypeStruct((t, b_width), BF16),
            jax.ShapeDtypeStruct((t, 2 * B_HEADS), F32),
        ],
        compiler_params=_compiler_params(("parallel",)),
        name="inproj",
    )(x, gain.reshape(1, d), w_main, w_qt, w_vt, w_gate, gq_col, gk, grp, gate_bias)
    return outs


def _moba_kernel(slope_ref, qt_ref, k_ref, vt_ref, o_ref, kaug_ref, vaug_ref, kmean_ref, qaug_ref, acc_ref,
                 *, nb, grp, heads):
    hg = pl.program_id(1)
    c = pl.program_id(2)
    blk = MOBA_BLOCK
    dh = A_HEAD_DIM
    nbp = -(-nb // 16) * 16

    @pl.when(c == 0)
    def _build_key_value_side():
        ones_rows = jnp.where(lax.broadcasted_iota(jnp.int32, (V_AUG_ROWS - dh, blk), 0) == 0,
                              1.0, 0.0).astype(BF16)
        for h in range(heads):
            for j in range(nb):
                g, off = divmod(j, grp)
                vaug_ref[h, g, 0:dh, off * blk:(off + 1) * blk] = vt_ref[dh * h:dh * (h + 1),
                                                                         j * blk:(j + 1) * blk]
                vaug_ref[h, g, dh:V_AUG_ROWS, off * blk:(off + 1) * blk] = ones_rows
        kmean_ref[...] = jnp.zeros_like(kmean_ref)
        lane = lax.broadcasted_iota(jnp.int32, (blk, LANES), 1)
        row = lax.broadcasted_iota(jnp.int32, (blk, LANES), 0)

        def body(j, carry):
            start = pl.multiple_of(j * blk, blk)
            pos = (row + j * blk).astype(F32)
            for pr in range(heads // 2):
                kb = k_ref[pl.ds(start, blk), pr * LANES:(pr + 1) * LANES].astype(F32)
                kmean_ref[pr, pl.ds(j, 1), :] = jnp.mean(kb, axis=0, keepdims=True)
                for hh in range(2):
                    h = 2 * pr + hh
                    slope = slope_ref[pl.ds(heads * hg + h, 1), :][:, 0:LANES] * LOG2E
                    p1, p2, p3 = _split_bf16(slope * pos, 3)
                    rel = lane - dh * (1 - hh)
                    aug = jnp.where(rel == j, 1.0, 0.0)
                    aug = jnp.where(rel == nb, p1.astype(F32), aug)
                    aug = jnp.where(rel == nb + 1, p2.astype(F32), aug)
                    aug = jnp.where(rel == nb + 2, p3.astype(F32), aug)
                    aug = jnp.where((rel >= nb + 3) & (rel < nb + 6), 1.0, aug)
                    is_data = (lane >= dh * hh) & (lane < dh * (hh + 1))
                    kaug_ref[h, pl.ds(start, blk), :] = jnp.where(is_data, kb, aug).astype(BF16)
            return carry

        lax.fori_loop(0, nb, body, 0)

    blk_ix = lax.broadcasted_iota(jnp.int32, (nbp, blk), 0)
    blk_f = blk_ix.astype(F32)
    valid = blk_ix < c
    aug_row = lax.broadcasted_iota(jnp.int32, (dh, blk), 0)
    tq = (lax.broadcasted_iota(jnp.int32, (dh, blk), 1) + c * blk).astype(F32)
    lane_k = lax.broadcasted_iota(jnp.int32, (nbp, LANES), 1)
    own_grp = c // grp
    own_start = pl.multiple_of(own_grp * (grp * blk), grp * blk)
    key_ix = lax.broadcasted_iota(jnp.int32, (grp * blk, blk), 0) + own_grp * (grp * blk)
    qry_ix = lax.broadcasted_iota(jnp.int32, (grp * blk, blk), 1) + c * blk
    gates = []
    for h in range(heads):
        pr, hh = divmod(h, 2)
        qt_pair = qt_ref[pr * LANES:(pr + 1) * LANES, :]
        is_data_k = (lane_k >= dh * hh) & (lane_k < dh * (hh + 1))
        km_hi, km_lo = _split_bf16(jnp.where(is_data_k, kmean_ref[pr, 0:nbp, :], 0.0), 2)
        gates.append(_dot(km_hi, qt_pair) + _dot(km_lo, qt_pair))
    for h in range(heads):
        pr, hh = divmod(h, 2)
        gate = jnp.where(valid, gates[h], -jnp.inf)
        chosen = blk_ix == c
        for _ in range(MOBA_TOPK):
            best = jnp.max(gate, axis=0, keepdims=True)
            first = jnp.min(jnp.where(gate == best, blk_f, float(nbp)), axis=0, keepdims=True)
            pick = blk_f == first
            chosen = chosen | (pick & valid)
            gate = jnp.where(pick, -jnp.inf, gate)
        bias = jnp.where(chosen, 0.0, MASK_VALUE)
        if nbp < dh:
            bias = jnp.concatenate([bias, jnp.zeros((dh - nbp, blk), F32)], axis=0)
        slope = slope_ref[pl.ds(heads * hg + h, 1), :] * LOG2E
        t1, t2, t3 = _split_bf16(-slope * tq, 3)
        aug = jnp.where(aug_row < nb, bias, 0.0)
        aug = jnp.where((aug_row >= nb) & (aug_row < nb + 3), 1.0, aug)
        aug = jnp.where(aug_row == nb + 3, t1.astype(F32), aug)
        aug = jnp.where(aug_row == nb + 4, t2.astype(F32), aug)
        aug = jnp.where(aug_row == nb + 5, t3.astype(F32), aug).astype(BF16)
        data = qt_ref[dh * h:dh * (h + 1), :]
        qaug_ref[h] = jnp.concatenate([data, aug] if hh == 0 else [aug, data], axis=0)

    scores = [_dot(kaug_ref[h, pl.ds(own_start, grp * blk), :], qaug_ref[h]) for h in range(heads)]
    m_init = []
    for h in range(heads):
        s = jnp.where(key_ix <= qry_ix, scores[h], MASK_VALUE)
        m0 = jnp.max(s, axis=0, keepdims=True)
        p = jnp.exp2(s - m0)
        acc_ref[h] = _dot(vaug_ref[h, own_grp], p.astype(BF16))
        m_init.append(m0)

    def past_group(g, ms):
        start = pl.multiple_of(g * (grp * blk), grp * blk)
        scores = [_dot(kaug_ref[h, pl.ds(start, grp * blk), :], qaug_ref[h]) for h in range(heads)]
        new_ms = []
        for h in range(heads):
            s = scores[h]
            m_new = jnp.maximum(ms[h], jnp.max(s, axis=0, keepdims=True))
            alpha = jnp.exp2(ms[h] - m_new)
            p = jnp.exp2(s - m_new)
            acc_ref[h] = alpha * acc_ref[h] + _dot(vaug_ref[h, g], p.astype(BF16))
            new_ms.append(m_new)
        return tuple(new_ms)

    lax.fori_loop(0, own_grp, past_group, tuple(m_init))
    outs = []
    for h in range(heads):
        acc = acc_ref[h]
        outs.append(acc[0:dh, :] / acc[dh:dh + 1, :])
    o_ref[...] = jnp.concatenate(outs, axis=0).T.astype(o_ref.dtype)


def _moba(qat, ka, vat, bsz, seq):
    t, a_width = ka.shape
    blk = MOBA_BLOCK
    assert seq % blk == 0 and seq // blk >= MOBA_TOPK
    nb = seq // blk
    assert nb + 6 <= A_HEAD_DIM
    grp = MOBA_KEY_GROUP if nb % MOBA_KEY_GROUP == 0 else 1
    heads = MOBA_HEADS_PER_STEP
    width = heads * A_HEAD_DIM
    assert heads % 2 == 0 and a_width % width == 0
    slopes = jnp.exp2(-8.0 * jnp.arange(1, A_HEADS + 1, dtype=F32) / A_HEADS)
    slope_tbl = jnp.broadcast_to(slopes[:, None], (A_HEADS, blk))
    return pl.pallas_call(
        functools.partial(_moba_kernel, nb=nb, grp=grp, heads=heads),
        grid=(bsz, a_width // width, nb),
        in_specs=[
            pl.BlockSpec((A_HEADS, blk), lambda b, hg, c: (0, 0)),
            pl.BlockSpec((width, blk), lambda b, hg, c: (hg, b * nb + c)),
            pl.BlockSpec((seq, width), lambda b, hg, c: (b, hg)),
            pl.BlockSpec((width, seq), lambda b, hg, c: (hg, b)),
        ],
        out_specs=pl.BlockSpec((blk, width), lambda b, hg, c: (b * nb + c, hg)),
        out_shape=jax.ShapeDtypeStruct((t, a_width), BF16),
        scratch_shapes=[
            pltpu.VMEM((heads, seq, LANES), BF16),
            pltpu.VMEM((heads, nb // grp, V_AUG_ROWS, grp * blk), BF16),
            pltpu.VMEM((heads // 2, LANES, LANES), F32),
            pltpu.VMEM((heads, LANES, blk), BF16),
            pltpu.VMEM((heads, V_AUG_ROWS, blk), F32),
        ],
        compiler_params=_compiler_params(("parallel", "parallel", "arbitrary")),
        name="moba",
    )(slope_tbl, qat, ka, vat)


def _mlstm_kernel(q_ref, k_ref, v_ref, ob_ref, gcol_ref, grow_ref, cwq_ref, cwk_ref, cbq_ref, cbk_ref,
                  o_ref, qbuf_ref, kbuf_ref, c_ref, m_ref):
    hd = pl.program_id(1)
    ci = pl.program_id(2)
    L, d = q_ref.shape
    halo = 8

    @pl.when(ci == 0)
    def _():
        qbuf_ref[0:halo, :] = jnp.zeros((halo, d), F32)
        kbuf_ref[0:halo, :] = jnp.zeros((halo, d), F32)
        c_ref[...] = jnp.zeros_like(c_ref)
        m_ref[...] = jnp.zeros_like(m_ref)

    def conv_silu(x_ref, buf_ref, w_ref, b_ref):
        buf_ref[halo:halo + L, :] = x_ref[...].astype(F32)
        y = b_ref[...]
        for j in range(CONV_WIDTH):
            off = halo - (CONV_WIDTH - 1) + j
            y = y + buf_ref[off:off + L, :] * w_ref[j:j + 1, :]
        buf_ref[0:halo, :] = buf_ref[L:L + halo, :]
        return y * jax.nn.sigmoid(y)

    q = (conv_silu(q_ref, qbuf_ref, cwq_ref, cbq_ref) * (d ** -0.5)).astype(BF16)
    k = conv_silu(k_ref, kbuf_ref, cwk_ref, cbk_ref)
    lane = lax.broadcasted_iota(jnp.int32, (L, d), 1)
    v_aug = jnp.concatenate([v_ref[...], jnp.where(lane == 0, 1.0, 0.0).astype(BF16)], axis=1)

    gcol = gcol_ref[...]
    hsel = lax.broadcasted_iota(jnp.int32, gcol.shape, 1)
    i_col = jnp.sum(jnp.where(hsel == hd, gcol, 0.0), axis=1, keepdims=True)
    f_col = jnp.sum(jnp.where(hsel == hd + B_HEADS, gcol, 0.0), axis=1, keepdims=True)
    i_row = grow_ref[0, pl.ds(hd, 1), :]
    f_row = grow_ref[0, pl.ds(hd + B_HEADS, 1), :]

    def log_sigmoid(z):
        return jnp.minimum(z, 0.0) - jnp.log1p(jnp.exp(-jnp.abs(z)))

    rr = lax.broadcasted_iota(jnp.int32, (L, L), 0)
    cc = lax.broadcasted_iota(jnp.int32, (L, L), 1)
    causal = rr >= cc
    lf_col = log_sigmoid(f_col)
    lf_row = log_sigmoid(f_row)
    tri_low = causal.astype(BF16)
    tri_up = (rr <= cc).astype(BF16)
    lane_l = lax.broadcasted_iota(jnp.int32, (L, LANES), 1)
    c_hi, c_lo = _split_bf16(jnp.where(lane_l == 0, lf_col, 0.0), 2)
    b_col = (_dot(tri_low, c_hi) + _dot(tri_low, c_lo))[:, 0:1]
    sub8 = lax.broadcasted_iota(jnp.int32, (8, L), 0)
    r_hi, r_lo = _split_bf16(jnp.where(sub8 == 0, lf_row, 0.0), 2)
    b_row = (_dot(r_hi, tri_up) + _dot(r_lo, tri_up))[0:1, :]

    m_prev = m_ref[...]
    log_inter = b_col + m_prev
    dmat = jnp.where(causal, b_col - b_row + i_row, -jnp.inf)
    m_t = jnp.maximum(log_inter, jnp.max(dmat, axis=-1, keepdims=True))
    w_inter = jnp.exp(log_inter - m_t)
    kb16 = k.astype(BF16)
    sc = _dot_nt(q, kb16) * jnp.exp(dmat - m_t)
    c_prev = c_ref[...]
    num_aug = w_inter * _dot(q, c_prev.astype(BF16)) + _dot(sc.astype(BF16), v_aug)
    num = num_aug[:, :d]
    den = num_aug[:, d:d + 1]
    hidden = num / jnp.maximum(jnp.abs(den), jnp.exp(-m_t))
    o_ref[...] = (jax.nn.sigmoid(ob_ref[...].astype(F32)) * hidden).astype(o_ref.dtype)

    b_last = b_col[L - 1:L, :]
    log_old = b_last + m_prev
    log_new = b_last - b_col + i_col
    m_new = jnp.maximum(log_old, jnp.max(log_new, axis=0, keepdims=True))
    a_old = jnp.exp(log_old - m_new)
    a_new = jnp.exp(log_new - m_new)
    c_ref[...] = a_old * c_prev + _dot_tn((a_new * k).astype(BF16), v_aug)
    m_ref[...] = m_new


def _mlstm(qkb, vb, ob, gates, conv_w, conv_b, bsz, seq):
    t, b_width = vb.shape
    d = B_HEAD_DIM
    L = min(MLSTM_KERNEL_CHUNK, seq)
    assert seq % L == 0 and d == LANES
    nc = seq // L
    gates_row = gates.reshape(bsz, seq, 2 * B_HEADS).transpose(0, 2, 1)
    cw = conv_w.astype(F32)
    cb = conv_b.astype(F32).reshape(1, 2 * b_width)
    tok = lambda off: (lambda b, h, c: (b * nc + c, h + off))
    return pl.pallas_call(
        _mlstm_kernel,
        grid=(bsz, B_HEADS, nc),
        in_specs=[
            pl.BlockSpec((L, d), tok(0)),
            pl.BlockSpec((L, d), tok(B_HEADS)),
            pl.BlockSpec((L, d), tok(0)),
            pl.BlockSpec((L, d), tok(0)),
            pl.BlockSpec((L, 2 * B_HEADS), lambda b, h, c: (b * nc + c, 0)),
            pl.BlockSpec((1, 2 * B_HEADS, L), lambda b, h, c: (b, 0, c)),
            pl.BlockSpec((CONV_WIDTH, d), lambda b, h, c: (0, h)),
            pl.BlockSpec((CONV_WIDTH, d), lambda b, h, c: (0, h + B_HEADS)),
            pl.BlockSpec((1, d), lambda b, h, c: (0, h)),
            pl.BlockSpec((1, d), lambda b, h, c: (0, h + B_HEADS)),
        ],
        out_specs=pl.BlockSpec((L, d), tok(0)),
        out_shape=jax.ShapeDtypeStruct((t, b_width), BF16),
        scratch_shapes=[
            pltpu.VMEM((L + 8, d), F32),
            pltpu.VMEM((L + 8, d), F32),
            pltpu.VMEM((d, 2 * d), F32),
            pltpu.VMEM((1, 1), F32),
        ],
        compiler_params=_compiler_params(("parallel", "parallel", "arbitrary")),
        name="mlstm",
    )(qkb, qkb, vb, ob, gates, gates_row, cw, cw, cb, cb)


def _outproj_kernel(x_ref, ya_ref, yb_ref, wa_ref, wb_ref, o_ref):
    o_ref[...] = x_ref[...] + _dot(ya_ref[...], wa_ref[...]) + _dot(yb_ref[...], wb_ref[...])


def _outproj(x, ya, yb, w_out):
    t, d = x.shape
    a_width = ya.shape[1]
    b_width = yb.shape[1]
    tm = PROJ_TOKEN_TILE
    row = lambda i: (i, 0)
    const = lambda i: (0, 0)
    return pl.pallas_call(
        _outproj_kernel,
        grid=(t // tm,),
        in_specs=[
            pl.BlockSpec((tm, d), row),
            pl.BlockSpec((tm, a_width), row),
            pl.BlockSpec((tm, b_width), row),
            pl.BlockSpec((a_width, d), const),
            pl.BlockSpec((b_width, d), const),
        ],
        out_specs=pl.BlockSpec((tm, d), row),
        out_shape=jax.ShapeDtypeStruct((t, d), F32),
        compiler_params=_compiler_params(("parallel",)),
        name="outproj",
    )(x, ya, yb, w_out[:a_width].astype(BF16), w_out[a_width:].astype(BF16))


def _pool_kernel(x_ref, halo_ref, g_ref, w_ref, scale_ref, o_ref, buf_ref, *, tiles_per_seq):
    i = pl.program_id(0)
    tm, d = x_ref.shape
    grp = d // len(POOL_WINDOWS)
    x = x_ref[...]
    seq_tile = i % tiles_per_seq
    halo_h = _rms_normalize(halo_ref[...], g_ref[...])
    buf_ref[0:POOL_HALO, :] = jnp.where(seq_tile == 0, 0.0, halo_h)
    h = _rms_normalize(x, g_ref[...])
    buf_ref[POOL_HALO:POOL_HALO + tm, :] = h
    t1 = (lax.broadcasted_iota(jnp.int32, (tm, 1), 0) + seq_tile * tm + 1).astype(F32)
    for gi, win in enumerate(POOL_WINDOWS):
        c0 = gi * grp
        total = h[:, c0:c0 + grp]
        for back in range(1, win):
            total = total + buf_ref[POOL_HALO - back:POOL_HALO - back + tm, c0:c0 + grp]
        pooled = total / jnp.minimum(t1, float(win)) - h[:, c0:c0 + grp]
        y = _dot(pooled.astype(BF16), w_ref[gi])
        o_ref[:, c0:c0 + grp] = x[:, c0:c0 + grp] + y * scale_ref[:, c0:c0 + grp]


def _pool(x, gain, w_grp, scale, seq):
    t, d = x.shape
    tm = min(POOL_TOKEN_TILE, seq)
    assert seq % tm == 0 and tm % POOL_HALO == 0
    n_grp, grp, _ = w_grp.shape
    halo_blocks = tm // POOL_HALO
    return pl.pallas_call(
        functools.partial(_pool_kernel, tiles_per_seq=seq // tm),
        grid=(t // tm,),
        in_specs=[
            pl.BlockSpec((tm, d), lambda i: (i, 0)),
            pl.BlockSpec((POOL_HALO, d), lambda i: (jnp.maximum(i * halo_blocks - 1, 0), 0)),
            pl.BlockSpec((1, d), lambda i: (0, 0)),
            pl.BlockSpec((n_grp, grp, grp), lambda i: (0, 0, 0)),
            pl.BlockSpec((1, d), lambda i: (0, 0)),
        ],
        out_specs=pl.BlockSpec((tm, d), lambda i: (i, 0)),
        out_shape=jax.ShapeDtypeStruct((t, d), F32),
        scratch_shapes=[pltpu.VMEM((tm + POOL_HALO, d), F32)],
        compiler_params=_compiler_params(("parallel",)),
        name="pool",
    )(x, x, gain.reshape(1, d), w_grp.astype(BF16), scale.astype(F32).reshape(1, d))


def _mixer_moba_mlstm(x, gain, w_in, w_out, g_q, g_k, conv_w, conv_b, b_i, b_f, bsz, seq):
    qat, ka, vat, qkb, vb, ob, gates = _inproj(x, gain, w_in, g_q, g_k, b_i, b_f)
    ya = _moba(qat, ka, vat, bsz, seq)
    yb = _mlstm(qkb, vb, ob, gates, conv_w, conv_b, bsz, seq)
    return _outproj(x, ya, yb, w_out)


def kernel(x, norm_g, ffn_w_gate, ffn_w_up, ffn_w_down, ab_w_in, ab_w_out, ab_g_q, ab_g_k,
           ab_conv_w, ab_conv_b, ab_b_i, ab_b_f, pool_w, pool_scale):
    bsz, seq, d = x.shape
    depth = norm_g.shape[0]
    y = x.reshape(bsz * seq, d)
    for layer in range(depth):
        y = _ffn(y, norm_g[layer, 0], ffn_w_gate[layer, 0], ffn_w_up[layer, 0], ffn_w_down[layer, 0])
        if layer % 2 == 0:
            e = layer // 2
            y = _mixer_moba_mlstm(y, norm_g[layer, 1], ab_w_in[e], ab_w_out[e], ab_g_q[e], ab_g_k[e],
                                  ab_conv_w[e], ab_conv_b[e], ab_b_i[e], ab_b_f[e], bsz, seq)
        else:
            o = layer // 2
            y = _pool(y, norm_g[layer, 1], pool_w[o], pool_scale[o], seq)
        y = _ffn(y, norm_g[layer, 2], ffn_w_gate[layer, 1], ffn_w_up[layer, 1], ffn_w_down[layer, 1])
    return y.reshape(bsz, seq, d)
```

```python
import functools

import jax
import jax.numpy as jnp
from jax import lax
from jax.experimental import pallas as pl
from jax.experimental.pallas import tpu as pltpu

F32 = jnp.float32
BF16 = jnp.bfloat16

RMS_EPS = 1e-6
A_HEADS = 8
A_HEAD_DIM = 64
MOBA_BLOCK = 256
MOBA_TOPK = 3
MOBA_KEY_GROUP = 2
MOBA_HEADS_PER_STEP = 4
B_HEADS = 4
B_HEAD_DIM = 128
CONV_WIDTH = 4
POOL_WINDOWS = (2, 4, 8, 16)
POOL_HALO = 16
MLSTM_KERNEL_CHUNK = 256
LANES = 128
MASK_VALUE = -1e30
LOG2E = 1.4426950408889634
V_AUG_ROWS = A_HEAD_DIM + 16

FFN_TOKEN_TILE = 512
FFN_HIDDEN_TILE = 256
PROJ_TOKEN_TILE = 512
POOL_TOKEN_TILE = 512
VMEM_LIMIT_BYTES = 56 * 1024 * 1024


def _compiler_params(semantics):
    return pltpu.CompilerParams(dimension_semantics=semantics,
                                vmem_limit_bytes=VMEM_LIMIT_BYTES)


def _rms_normalize(x, gain):
    ms = jnp.mean(x * x, axis=-1, keepdims=True)
    return x * lax.rsqrt(ms + RMS_EPS) * gain


def _dot(a, b):
    return jnp.dot(a, b, preferred_element_type=F32)


def _dot_nt(a, b):
    return lax.dot_general(a, b, (((1,), (1,)), ((), ())), preferred_element_type=F32)


def _dot_tn(a, b):
    return lax.dot_general(a, b, (((0,), (0,)), ((), ())), preferred_element_type=F32)


def _split_bf16(x, parts):
    out = []
    rem = x
    for _ in range(parts):
        p = rem.astype(BF16)
        out.append(p)
        rem = rem - p.astype(F32)
    return out


def _ffn_kernel(x_ref, g_ref, wg_ref, wu_ref, wd_ref, o_ref, *, hidden_tile):
    x = x_ref[...]
    h = _rms_normalize(x, g_ref[...]).astype(BF16)
    d_ff = wg_ref.shape[1]
    acc = None
    for c0 in range(0, d_ff, hidden_tile):
        gate = _dot(h, wg_ref[:, c0:c0 + hidden_tile])
        up = _dot(h, wu_ref[:, c0:c0 + hidden_tile])
        act = (gate * jax.nn.sigmoid(gate) * up).astype(BF16)
        part = _dot(act, wd_ref[c0:c0 + hidden_tile, :])
        acc = part if acc is None else acc + part
    o_ref[...] = x + 0.5 * acc


def _resident(shape):
    return pl.BlockSpec(shape, lambda *_: (0,) * len(shape), pipeline_mode=pl.Buffered(1))


def _ffn(x, gain, w_gate, w_up, w_down):
    t, d = x.shape
    d_ff = w_gate.shape[1]
    tm, tf = FFN_TOKEN_TILE, FFN_HIDDEN_TILE
    assert t % tm == 0 and d_ff % tf == 0
    return pl.pallas_call(
        functools.partial(_ffn_kernel, hidden_tile=tf),
        grid=(t // tm,),
        in_specs=[
            pl.BlockSpec((tm, d), lambda i: (i, 0)),
            _resident((1, d)),
            _resident((d, d_ff)),
            _resident((d, d_ff)),
            _resident((d_ff, d)),
        ],
        out_specs=pl.BlockSpec((tm, d), lambda i: (i, 0)),
        out_shape=jax.ShapeDtypeStruct((t, d), F32),
        compiler_params=_compiler_params(("parallel",)),
        name="ffn",
    )(x, gain.reshape(1, d), w_gate.astype(BF16), w_up.astype(BF16), w_down.astype(BF16))


def _inproj_kernel(x_ref, g_ref, w_ref, wqt_ref, wvt_ref, wgate_ref, gqcol_ref, gk_ref, grp_ref, gbias_ref,
                   qat_ref, ka_ref, vat_ref, qkb_ref, vb_ref, ob_ref, gates_ref, *, a_width, b_width):
    h = _rms_normalize(x_ref[...], g_ref[...]).astype(BF16)

    qt = _dot_nt(wqt_ref[...], h)
    for hd in range(a_width // A_HEAD_DIM):
        rows = slice(hd * A_HEAD_DIM, (hd + 1) * A_HEAD_DIM)
        y = qt[rows, :]
        ms = jnp.mean(y * y, axis=0, keepdims=True)
        gain = gqcol_ref[rows, :] * (A_HEAD_DIM ** -0.5 * LOG2E)
        qat_ref[rows, :] = (y * lax.rsqrt(ms + RMS_EPS) * gain).astype(BF16)

    y = _dot(h, w_ref[:, 0:a_width])
    sq_hi, sq_lo = _split_bf16(y * y, 2)
    ssq = _dot(sq_hi, grp_ref[...]) + _dot(sq_lo, grp_ref[...])
    ka_ref[...] = (y * lax.rsqrt(ssq * (1.0 / A_HEAD_DIM) + RMS_EPS) * gk_ref[...]).astype(BF16)

    vat_ref[...] = _dot_nt(wvt_ref[...], h).astype(BF16)
    c0 = a_width
    qkb_ref[...] = _dot(h, w_ref[:, c0:c0 + 2 * b_width]).astype(BF16)
    c0 += 2 * b_width
    vb_ref[...] = _dot(h, w_ref[:, c0:c0 + b_width]).astype(BF16)
    c0 += b_width
    ob_ref[...] = _dot(h, w_ref[:, c0:c0 + b_width]).astype(BF16)
    gates = _dot(h, wgate_ref[...]) + gbias_ref[...]
    gates_ref[...] = gates[:, :2 * B_HEADS]


def _inproj(x, gain, w_in, g_q, g_k, b_i, b_f):
    t, d = x.shape
    a_width = A_HEADS * A_HEAD_DIM
    b_width = B_HEADS * B_HEAD_DIM
    n_main = a_width + 4 * b_width
    o_ib = 3 * a_width + 3 * b_width
    tm = PROJ_TOKEN_TILE
    assert t % tm == 0
    w_main = jnp.concatenate([w_in[:, a_width:2 * a_width], w_in[:, 3 * a_width:o_ib],
                              w_in[:, o_ib + 2 * B_HEADS:]], axis=1).astype(BF16)
    w_qt = w_in[:, :a_width].T.astype(BF16)
    w_vt = w_in[:, 2 * a_width:3 * a_width].T.astype(BF16)
    w_gate = jnp.pad(w_in[:, o_ib:o_ib + 2 * B_HEADS], ((0, 0), (0, LANES - 2 * B_HEADS))).astype(BF16)
    gate_bias = jnp.pad(jnp.concatenate([b_i, b_f]).astype(F32), (0, LANES - 2 * B_HEADS)).reshape(1, LANES)
    head_id = jnp.arange(a_width) // A_HEAD_DIM
    grp = (head_id[:, None] == head_id[None, :]).astype(BF16)
    gq_col = jnp.tile(g_q.astype(F32), A_HEADS).reshape(a_width, 1)
    gk = jnp.tile(g_k.astype(F32), A_HEADS).reshape(1, a_width)
    const = lambda i: (0, 0)
    row = lambda i: (i, 0)
    col = lambda i: (0, i)
    outs = pl.pallas_call(
        functools.partial(_inproj_kernel, a_width=a_width, b_width=b_width),
        grid=(t // tm,),
        in_specs=[
            pl.BlockSpec((tm, d), row),
            pl.BlockSpec((1, d), const),
            pl.BlockSpec((d, n_main), const),
            pl.BlockSpec((a_width, d), const),
            pl.BlockSpec((a_width, d), const),
            pl.BlockSpec((d, LANES), const),
            pl.BlockSpec((a_width, 1), const),
            pl.BlockSpec((1, a_width), const),
            pl.BlockSpec((a_width, a_width), const),
            pl.BlockSpec((1, LANES), const),
        ],
        out_specs=[
            pl.BlockSpec((a_width, tm), col),
            pl.BlockSpec((tm, a_width), row),
            pl.BlockSpec((a_width, tm), col),
            pl.BlockSpec((tm, 2 * b_width), row),
            pl.BlockSpec((tm, b_width), row),
            pl.BlockSpec((tm, b_width), row),
            pl.BlockSpec((tm, 2 * B_HEADS), row),
        ],
        out_shape=[
            jax.ShapeDtypeStruct((a_width, t), BF16),
            jax.ShapeDtypeStruct((t, a_width), BF16),
            jax.ShapeDtypeStruct((a_width, t), BF16),
            jax.ShapeDtypeStruct((t, 2 * b_width), BF16),
            jax.ShapeDtypeStruct((t, b_width), BF16),
            jax.ShapeDtypeStruct((t, b_width), BF16),
            jax.ShapeDtypeStruct((t, 2 * B_HEADS), F32),
        ],
        compiler_params=_compiler_params(("parallel",)),
        name="inproj",
    )(x, gain.reshape(1, d), w_main, w_qt, w_vt, w_gate, gq_col, gk, grp, gate_bias)
    return outs


def _moba_kernel(slope_ref, qt_ref, k_ref, vt_ref, o_ref, kaug_ref, vaug_ref, kmean_ref, qaug_ref, acc_ref,
                 sa_ref, sb_ref, *, nb, grp, heads):
    hg = pl.program_id(1)
    c = pl.program_id(2)
    blk = MOBA_BLOCK
    dh = A_HEAD_DIM
    nbp = -(-nb // 16) * 16

    @pl.when(c == 0)
    def _build_key_value_side():
        ones_rows = jnp.where(lax.broadcasted_iota(jnp.int32, (V_AUG_ROWS - dh, blk), 0) == 0,
                              1.0, 0.0).astype(BF16)
        for h in range(heads):
            for j in range(nb):
                g, off = divmod(j, grp)
                vaug_ref[h, g, 0:dh, off * blk:(off + 1) * blk] = vt_ref[dh * h:dh * (h + 1),
                                                                         j * blk:(j + 1) * blk]
                vaug_ref[h, g, dh:V_AUG_ROWS, off * blk:(off + 1) * blk] = ones_rows
        kmean_ref[...] = jnp.zeros_like(kmean_ref)
        lane = lax.broadcasted_iota(jnp.int32, (blk, LANES), 1)
        row = lax.broadcasted_iota(jnp.int32, (blk, LANES), 0)

        def body(j, carry):
            start = pl.multiple_of(j * blk, blk)
            pos = (row + j * blk).astype(F32)
            for pr in range(heads // 2):
                kb = k_ref[pl.ds(start, blk), pr * LANES:(pr + 1) * LANES].astype(F32)
                kmean_ref[pr, pl.ds(j, 1), :] = jnp.mean(kb, axis=0, keepdims=True)
                for hh in range(2):
                    h = 2 * pr + hh
                    slope = slope_ref[pl.ds(heads * hg + h, 1), :][:, 0:LANES] * LOG2E
                    p1, p2, p3 = _split_bf16(slope * pos, 3)
                    rel = lane - dh * (1 - hh)
                    aug = jnp.where(rel == j, 1.0, 0.0)
                    aug = jnp.where(rel == nb, p1.astype(F32), aug)
                    aug = jnp.where(rel == nb + 1, p2.astype(F32), aug)
                    aug = jnp.where(rel == nb + 2, p3.astype(F32), aug)
                    aug = jnp.where((rel >= nb + 3) & (rel < nb + 6), 1.0, aug)
                    is_data = (lane >= dh * hh) & (lane < dh * (hh + 1))
                    kaug_ref[h, pl.ds(start, blk), :] = jnp.where(is_data, kb, aug).astype(BF16)
            return carry

        lax.fori_loop(0, nb, body, 0)

    blk_ix = lax.broadcasted_iota(jnp.int32, (nbp, blk), 0)
    blk_f = blk_ix.astype(F32)
    valid = blk_ix < c
    aug_row = lax.broadcasted_iota(jnp.int32, (dh, blk), 0)
    tq = (lax.broadcasted_iota(jnp.int32, (dh, blk), 1) + c * blk).astype(F32)
    lane_k = lax.broadcasted_iota(jnp.int32, (nbp, LANES), 1)
    own_grp = c // grp
    key_ix = lax.broadcasted_iota(jnp.int32, (grp * blk, blk), 0) + own_grp * (grp * blk)
    qry_ix = lax.broadcasted_iota(jnp.int32, (grp * blk, blk), 1) + c * blk
    gates = []
    for h in range(heads):
        pr, hh = divmod(h, 2)
        qt_pair = qt_ref[pr * LANES:(pr + 1) * LANES, :]
        is_data_k = (lane_k >= dh * hh) & (lane_k < dh * (hh + 1))
        km_hi, km_lo = _split_bf16(jnp.where(is_data_k, kmean_ref[pr, 0:nbp, :], 0.0), 2)
        gates.append(_dot(km_hi, qt_pair) + _dot(km_lo, qt_pair))
    for h in range(heads):
        pr, hh = divmod(h, 2)
        gate = jnp.where(valid, gates[h], -jnp.inf)
        chosen = blk_ix == c
        for _ in range(MOBA_TOPK):
            best = jnp.max(gate, axis=0, keepdims=True)
            first = jnp.min(jnp.where(gate == best, blk_f, float(nbp)), axis=0, keepdims=True)
            pick = blk_f == first
            chosen = chosen | (pick & valid)
            gate = jnp.where(pick, -jnp.inf, gate)
        bias = jnp.where(chosen, 0.0, MASK_VALUE)
        if nbp < dh:
            bias = jnp.concatenate([bias, jnp.zeros((dh - nbp, blk), F32)], axis=0)
        slope = slope_ref[pl.ds(heads * hg + h, 1), :] * LOG2E
        t1, t2, t3 = _split_bf16(-slope * tq, 3)
        aug = jnp.where(aug_row < nb, bias, 0.0)
        aug = jnp.where((aug_row >= nb) & (aug_row < nb + 3), 1.0, aug)
        aug = jnp.where(aug_row == nb + 3, t1.astype(F32), aug)
        aug = jnp.where(aug_row == nb + 4, t2.astype(F32), aug)
        aug = jnp.where(aug_row == nb + 5, t3.astype(F32), aug).astype(BF16)
        data = qt_ref[dh * h:dh * (h + 1), :]
        qaug_ref[h] = jnp.concatenate([data, aug] if hh == 0 else [aug, data], axis=0)

    def scores_into(buf_ref, g):
        start = pl.multiple_of(g * (grp * blk), grp * blk)
        for h in range(heads):
            buf_ref[h] = _dot(kaug_ref[h, pl.ds(start, grp * blk), :], qaug_ref[h])

    def consume(buf_ref, g, ms, own):
        new_ms = []
        for h in range(heads):
            s = buf_ref[h]
            if own:
                s = jnp.where(key_ix <= qry_ix, s, MASK_VALUE)
            m_new = jnp.maximum(ms[h], jnp.max(s, axis=0, keepdims=True))
            alpha = jnp.exp2(ms[h] - m_new)
            p = jnp.exp2(s - m_new)
            acc_ref[h] = alpha * acc_ref[h] + _dot(vaug_ref[h, g], p.astype(BF16))
            new_ms.append(m_new)
        return tuple(new_ms)

    def write_output():
        outs = []
        for h in range(heads):
            acc = acc_ref[h]
            outs.append(acc[0:dh, :] / acc[dh:dh + 1, :])
        o_ref[...] = jnp.concatenate(outs, axis=0).T.astype(o_ref.dtype)

    acc_ref[...] = jnp.zeros_like(acc_ref)
    masked_score = jnp.full((1, blk), MASK_VALUE, F32).astype(BF16).astype(F32)
    scores_into(sa_ref, 0)

    def two_groups(i, ms):
        scores_into(sb_ref, 2 * i + 1)
        ms = consume(sa_ref, 2 * i, ms, False)
        scores_into(sa_ref, 2 * i + 2)
        return consume(sb_ref, 2 * i + 1, ms, False)

    ms = lax.fori_loop(0, own_grp // 2, two_groups, (masked_score,) * heads)

    @pl.when(own_grp % 2 == 0)
    def _own_group_in_a():
        consume(sa_ref, own_grp, ms, True)
        write_output()

    @pl.when(own_grp % 2 == 1)
    def _own_group_in_b():
        scores_into(sb_ref, own_grp)
        consume(sb_ref, own_grp, consume(sa_ref, own_grp - 1, ms, False), True)
        write_output()


def _moba(qat, ka, vat, bsz, seq):
    t, a_width = ka.shape
    blk = MOBA_BLOCK
    assert seq % blk == 0 and seq // blk >= MOBA_TOPK
    nb = seq // blk
    assert nb + 6 <= A_HEAD_DIM
    grp = MOBA_KEY_GROUP if nb % MOBA_KEY_GROUP == 0 else 1
    heads = MOBA_HEADS_PER_STEP
    width = heads * A_HEAD_DIM
    assert heads % 2 == 0 and a_width % width == 0
    slopes = jnp.exp2(-8.0 * jnp.arange(1, A_HEADS + 1, dtype=F32) / A_HEADS)
    slope_tbl = jnp.broadcast_to(slopes[:, None], (A_HEADS, blk))
    return pl.pallas_call(
        functools.partial(_moba_kernel, nb=nb, grp=grp, heads=heads),
        grid=(bsz, a_width // width, nb),
        in_specs=[
            pl.BlockSpec((A_HEADS, blk), lambda b, hg, c: (0, 0)),
            pl.BlockSpec((width, blk), lambda b, hg, c: (hg, b * nb + c)),
            pl.BlockSpec((seq, width), lambda b, hg, c: (b, hg), pipeline_mode=pl.Buffered(1)),
            pl.BlockSpec((width, seq), lambda b, hg, c: (hg, b), pipeline_mode=pl.Buffered(1)),
        ],
        out_specs=pl.BlockSpec((blk, width), lambda b, hg, c: (b * nb + c, hg)),
        out_shape=jax.ShapeDtypeStruct((t, a_width), BF16),
        scratch_shapes=[
            pltpu.VMEM((heads, seq, LANES), BF16),
            pltpu.VMEM((heads, nb // grp, V_AUG_ROWS, grp * blk), BF16),
            pltpu.VMEM((heads // 2, LANES, LANES), F32),
            pltpu.VMEM((heads, LANES, blk), BF16),
            pltpu.VMEM((heads, V_AUG_ROWS, blk), F32),
            pltpu.VMEM((heads, grp * blk, blk), F32),
            pltpu.VMEM((heads, grp * blk, blk), F32),
        ],
        compiler_params=_compiler_params(("parallel", "parallel", "arbitrary")),
        name="moba",
    )(slope_tbl, qat, ka, vat)


def _mlstm_kernel(q_ref, k_ref, v_ref, ob_ref, gcol_ref, grow_ref, cwq_ref, cwk_ref, cbq_ref, cbk_ref,
                  o_ref, qbuf_ref, kbuf_ref, c_ref, m_ref):
    ci = pl.program_id(1)
    L, width = q_ref.shape
    d = B_HEAD_DIM
    halo = 8

    @pl.when(ci == 0)
    def _():
        qbuf_ref[0:halo, :] = jnp.zeros((halo, width), F32)
        kbuf_ref[0:halo, :] = jnp.zeros((halo, width), F32)
        c_ref[...] = jnp.zeros_like(c_ref)
        m_ref[...] = jnp.zeros_like(m_ref)

    def conv_silu(x_ref, buf_ref, w_ref, b_ref):
        buf_ref[halo:halo + L, :] = x_ref[...].astype(F32)
        y = b_ref[...]
        for j in range(CONV_WIDTH):
            off = halo - (CONV_WIDTH - 1) + j
            y = y + buf_ref[off:off + L, :] * w_ref[j:j + 1, :]
        buf_ref[0:halo, :] = buf_ref[L:L + halo, :]
        return y * jax.nn.sigmoid(y)

    q_all = (conv_silu(q_ref, qbuf_ref, cwq_ref, cbq_ref) * (d ** -0.5)).astype(BF16)
    k_all = conv_silu(k_ref, kbuf_ref, cwk_ref, cbk_ref)

    def log_sigmoid(z):
        return jnp.minimum(z, 0.0) - jnp.log1p(jnp.exp(-jnp.abs(z)))

    rr = lax.broadcasted_iota(jnp.int32, (L, L), 0)
    cc = lax.broadcasted_iota(jnp.int32, (L, L), 1)
    causal = rr >= cc
    tri_low = causal.astype(BF16)
    tri_up = (rr <= cc).astype(BF16)
    lane_l = lax.broadcasted_iota(jnp.int32, (L, LANES), 1)
    sub8 = lax.broadcasted_iota(jnp.int32, (8, L), 0)
    ones_col = jnp.where(lax.broadcasted_iota(jnp.int32, (L, d), 1) == 0, 1.0, 0.0).astype(BF16)
    gcol = gcol_ref[...]
    for hd in range(B_HEADS):
        cols = slice(hd * d, (hd + 1) * d)
        q = q_all[:, cols]
        k = k_all[:, cols]
        v_aug = jnp.concatenate([v_ref[:, cols], ones_col], axis=1)
        i_col = gcol[:, hd:hd + 1]
        f_col = gcol[:, hd + B_HEADS:hd + B_HEADS + 1]
        i_row = grow_ref[0, hd:hd + 1, :]
        f_row = grow_ref[0, hd + B_HEADS:hd + B_HEADS + 1, :]

        c_hi, c_lo = _split_bf16(jnp.where(lane_l == 0, log_sigmoid(f_col), 0.0), 2)
        b_col = (_dot(tri_low, c_hi) + _dot(tri_low, c_lo))[:, 0:1]
        r_hi, r_lo = _split_bf16(jnp.where(sub8 == 0, log_sigmoid(f_row), 0.0), 2)
        b_row = (_dot(r_hi, tri_up) + _dot(r_lo, tri_up))[0:1, :]

        m_prev = m_ref[hd]
        log_inter = b_col + m_prev
        dmat = jnp.where(causal, b_col - b_row + i_row, -jnp.inf)
        m_t = jnp.maximum(log_inter, jnp.max(dmat, axis=-1, keepdims=True))
        w_inter = jnp.exp(log_inter - m_t)
        sc = _dot_nt(q, k.astype(BF16)) * jnp.exp(dmat - m_t)
        c_prev = c_ref[hd]
        num_aug = w_inter * _dot(q, c_prev.astype(BF16)) + _dot(sc.astype(BF16), v_aug)
        num = num_aug[:, :d]
        den = num_aug[:, d:d + 1]
        hidden = num / jnp.maximum(jnp.abs(den), jnp.exp(-m_t))
        o_ref[:, cols] = (jax.nn.sigmoid(ob_ref[:, cols].astype(F32)) * hidden).astype(o_ref.dtype)

        b_last = b_col[L - 1:L, :]
        log_old = b_last + m_prev
        log_new = b_last - b_col + i_col
        m_new = jnp.maximum(log_old, jnp.max(log_new, axis=0, keepdims=True))
        a_old = jnp.exp(log_old - m_new)
        a_new = jnp.exp(log_new - m_new)
        c_ref[hd] = a_old * c_prev + _dot_tn((a_new * k).astype(BF16), v_aug)
        m_ref[hd] = m_new


def _mlstm(qkb, vb, ob, gates, conv_w, conv_b, bsz, seq):
    t, b_width = vb.shape
    d = B_HEAD_DIM
    L = min(MLSTM_KERNEL_CHUNK, seq)
    assert seq % L == 0 and d == LANES and b_width == B_HEADS * d
    nc = seq // L
    gates_row = gates.reshape(bsz, seq, 2 * B_HEADS).transpose(0, 2, 1)
    cw = conv_w.astype(F32)
    cb = conv_b.astype(F32).reshape(1, 2 * b_width)
    tok = lambda off: (lambda b, c: (b * nc + c, off))
    return pl.pallas_call(
        _mlstm_kernel,
        grid=(bsz, nc),
        in_specs=[
            pl.BlockSpec((L, b_width), tok(0)),
            pl.BlockSpec((L, b_width), tok(1)),
            pl.BlockSpec((L, b_width), tok(0)),
            pl.BlockSpec((L, b_width), tok(0)),
            pl.BlockSpec((L, 2 * B_HEADS), tok(0)),
            pl.BlockSpec((1, 2 * B_HEADS, L), lambda b, c: (b, 0, c)),
            pl.BlockSpec((CONV_WIDTH, b_width), lambda b, c: (0, 0)),
            pl.BlockSpec((CONV_WIDTH, b_width), lambda b, c: (0, 1)),
            pl.BlockSpec((1, b_width), lambda b, c: (0, 0)),
            pl.BlockSpec((1, b_width), lambda b, c: (0, 1)),
        ],
        out_specs=pl.BlockSpec((L, b_width), tok(0)),
        out_shape=jax.ShapeDtypeStruct((t, b_width), BF16),
        scratch_shapes=[
            pltpu.VMEM((L + 8, b_width), F32),
            pltpu.VMEM((L + 8, b_width), F32),
            pltpu.VMEM((B_HEADS, d, 2 * d), F32),
            pltpu.VMEM((B_HEADS, 1, 1), F32),
        ],
        compiler_params=_compiler_params(("parallel", "arbitrary")),
        name="mlstm",
    )(qkb, qkb, vb, ob, gates, gates_row, cw, cw, cb, cb)


def _outproj_kernel(x_ref, ya_ref, yb_ref, wa_ref, wb_ref, o_ref):
    o_ref[...] = x_ref[...] + _dot(ya_ref[...], wa_ref[...]) + _dot(yb_ref[...], wb_ref[...])


def _outproj(x, ya, yb, w_out):
    t, d = x.shape
    a_width = ya.shape[1]
    b_width = yb.shape[1]
    tm = PROJ_TOKEN_TILE
    row = lambda i: (i, 0)
    const = lambda i: (0, 0)
    return pl.pallas_call(
        _outproj_kernel,
        grid=(t // tm,),
        in_specs=[
            pl.BlockSpec((tm, d), row),
            pl.BlockSpec((tm, a_width), row),
            pl.BlockSpec((tm, b_width), row),
            pl.BlockSpec((a_width, d), const),
            pl.BlockSpec((b_width, d), const),
        ],
        out_specs=pl.BlockSpec((tm, d), row),
        out_shape=jax.ShapeDtypeStruct((t, d), F32),
        compiler_params=_compiler_params(("parallel",)),
        name="outproj",
    )(x, ya, yb, w_out[:a_width].astype(BF16), w_out[a_width:].astype(BF16))


def _pool_kernel(x_ref, halo_ref, g_ref, w_ref, scale_ref, o_ref, buf_ref, *, tiles_per_seq):
    i = pl.program_id(0)
    tm, d = x_ref.shape
    grp = d // len(POOL_WINDOWS)
    x = x_ref[...]
    seq_tile = i % tiles_per_seq
    halo_h = _rms_normalize(halo_ref[...], g_ref[...])
    buf_ref[0:POOL_HALO, :] = jnp.where(seq_tile == 0, 0.0, halo_h)
    h = _rms_normalize(x, g_ref[...])
    buf_ref[POOL_HALO:POOL_HALO + tm, :] = h
    t1 = (lax.broadcasted_iota(jnp.int32, (tm, 1), 0) + seq_tile * tm + 1).astype(F32)
    for gi, win in enumerate(POOL_WINDOWS):
        c0 = gi * grp
        total = h[:, c0:c0 + grp]
        for back in range(1, win):
            total = total + buf_ref[POOL_HALO - back:POOL_HALO - back + tm, c0:c0 + grp]
        pooled = total / jnp.minimum(t1, float(win)) - h[:, c0:c0 + grp]
        y = _dot(pooled.astype(BF16), w_ref[gi])
        o_ref[:, c0:c0 + grp] = x[:, c0:c0 + grp] + y * scale_ref[:, c0:c0 + grp]


def _pool(x, gain, w_grp, scale, seq):
    t, d = x.shape
    tm = min(POOL_TOKEN_TILE, seq)
    assert seq % tm == 0 and tm % POOL_HALO == 0
    n_grp, grp, _ = w_grp.shape
    halo_blocks = tm // POOL_HALO
    return pl.pallas_call(
        functools.partial(_pool_kernel, tiles_per_seq=seq // tm),
        grid=(t // tm,),
        in_specs=[
            pl.BlockSpec((tm, d), lambda i: (i, 0)),
            pl.BlockSpec((POOL_HALO, d), lambda i: (jnp.maximum(i * halo_blocks - 1, 0), 0)),
            pl.BlockSpec((1, d), lambda i: (0, 0)),
            pl.BlockSpec((n_grp, grp, grp), lambda i: (0, 0, 0)),
            pl.BlockSpec((1, d), lambda i: (0, 0)),
        ],
        out_specs=pl.BlockSpec((tm, d), lambda i: (i, 0)),
        out_shape=jax.ShapeDtypeStruct((t, d), F32),
        scratch_shapes=[pltpu.VMEM((tm + POOL_HALO, d), F32)],
        compiler_params=_compiler_params(("parallel",)),
        name="pool",
    )(x, x, gain.reshape(1, d), w_grp.astype(BF16), scale.astype(F32).reshape(1, d))


def _mixer_moba_mlstm(x, gain, w_in, w_out, g_q, g_k, conv_w, conv_b, b_i, b_f, bsz, seq):
    qat, ka, vat, qkb, vb, ob, gates = _inproj(x, gain, w_in, g_q, g_k, b_i, b_f)
    ya = _moba(qat, ka, vat, bsz, seq)
    yb = _mlstm(qkb, vb, ob, gates, conv_w, conv_b, bsz, seq)
    return _outproj(x, ya, yb, w_out)


def kernel(x, norm_g, ffn_w_gate, ffn_w_up, ffn_w_down, ab_w_in, ab_w_out, ab_g_q, ab_g_k,
           ab_conv_w, ab_conv_b, ab_b_i, ab_b_f, pool_w, pool_scale):
    bsz, seq, d = x.shape
    depth = norm_g.shape[0]
    y = x.reshape(bsz * seq, d)
    for layer in range(depth):
        y = _ffn(y, norm_g[layer, 0], ffn_w_gate[layer, 0], ffn_w_up[layer, 0], ffn_w_down[layer, 0])
        if layer % 2 == 0:
            e = layer // 2
            y = _mixer_moba_mlstm(y, norm_g[layer, 1], ab_w_in[e], ab_w_out[e], ab_g_q[e], ab_g_k[e],
                                  ab_conv_w[e], ab_conv_b[e], ab_b_i[e], ab_b_f[e], bsz, seq)
        else:
            o = layer // 2
            y = _pool(y, norm_g[layer, 1], pool_w[o], pool_scale[o], seq)
        y = _ffn(y, norm_g[layer, 2], ffn_w_gate[layer, 1], ffn_w_up[layer, 1], ffn_w_down[layer, 1])
    return y.reshape(bsz, seq, d)
```

```python
import functools

import jax
import jax.numpy as jnp
from jax import lax
from jax.experimental import pallas as pl
from jax.experimental.pallas import tpu as pltpu

F32 = jnp.float32
BF16 = jnp.bfloat16

RMS_EPS = 1e-6
A_HEADS = 8
A_HEAD_DIM = 64
MOBA_BLOCK = 256
MOBA_TOPK = 3
MOBA_KEY_GROUP = 2
MOBA_HEADS_PER_STEP = 4
B_HEADS = 4
B_HEAD_DIM = 128
CONV_WIDTH = 4
POOL_WINDOWS = (2, 4, 8, 16)
POOL_HALO = 16
POOL_PAD = 8
MLSTM_KERNEL_CHUNK = 256
LANES = 128
MASK_VALUE = -1e30
LOG2E = 1.4426950408889634
V_AUG_ROWS = A_HEAD_DIM + 16

FFN_TOKEN_TILE = 1024
FFN_SUB_TILE = 512
FFN_HIDDEN_TILE = 256
PROJ_TOKEN_TILE = 512
POOL_TOKEN_TILE = 512
VMEM_LIMIT_BYTES = 56 * 1024 * 1024


def _compiler_params(semantics):
    return pltpu.CompilerParams(dimension_semantics=semantics,
                                vmem_limit_bytes=VMEM_LIMIT_BYTES)


def _rms_normalize(x, gain):
    ms = jnp.mean(x * x, axis=-1, keepdims=True)
    return x * lax.rsqrt(ms + RMS_EPS) * gain


def _dot(a, b):
    return jnp.dot(a, b, preferred_element_type=F32)


def _dot_nt(a, b):
    return lax.dot_general(a, b, (((1,), (1,)), ((), ())), preferred_element_type=F32)


def _dot_tn(a, b):
    return lax.dot_general(a, b, (((0,), (0,)), ((), ())), preferred_element_type=F32)


def _split_bf16(x, parts):
    out = []
    rem = x
    for _ in range(parts):
        p = rem.astype(BF16)
        out.append(p)
        rem = rem - p.astype(F32)
    return out


def _ffn_kernel(x_ref, g_ref, wg_ref, wu_ref, wd_ref, *rest, hidden_tile, sub_tile):
    o_ref = rest[-1]
    d_ff = wg_ref.shape[1]
    for r0 in range(0, x_ref.shape[0], sub_tile):
        x = x_ref[r0:r0 + sub_tile, :]
        if len(rest) > 1:
            ya_ref, yb_ref, wa_ref, wb_ref = rest[:-1]
            x = (x + _dot(ya_ref[r0:r0 + sub_tile, :], wa_ref[...])
                 + _dot(yb_ref[r0:r0 + sub_tile, :], wb_ref[...]))
        h = _rms_normalize(x, g_ref[...]).astype(BF16)
        acc = None
        for c0 in range(0, d_ff, hidden_tile):
            gate = _dot(h, wg_ref[:, c0:c0 + hidden_tile])
            up = _dot(h, wu_ref[:, c0:c0 + hidden_tile])
            act = (gate * jax.nn.sigmoid(gate) * up).astype(BF16)
            part = _dot(act, wd_ref[c0:c0 + hidden_tile, :])
            acc = part if acc is None else acc + part
        o_ref[r0:r0 + sub_tile, :] = x + 0.5 * acc


def _resident(shape):
    return pl.BlockSpec(shape, lambda *_: (0,) * len(shape), pipeline_mode=pl.Buffered(1))


def _ffn(x, gain, w_gate, w_up, w_down, mixer_out=None):
    t, d = x.shape
    d_ff = w_gate.shape[1]
    tm, tf = FFN_TOKEN_TILE, FFN_HIDDEN_TILE
    assert t % tm == 0 and d_ff % tf == 0 and tm % FFN_SUB_TILE == 0
    row = lambda i: (i, 0)
    in_specs = [pl.BlockSpec((tm, d), row), _resident((1, d)), _resident((d, d_ff)),
                _resident((d, d_ff)), _resident((d_ff, d))]
    args = [x, gain.reshape(1, d), w_gate.astype(BF16), w_up.astype(BF16), w_down.astype(BF16)]
    if mixer_out is not None:
        ya, yb, w_out = mixer_out
        wa, wb = ya.shape[1], yb.shape[1]
        in_specs += [pl.BlockSpec((tm, wa), row), pl.BlockSpec((tm, wb), row),
                     _resident((wa, d)), _resident((wb, d))]
        args += [ya, yb, w_out[:wa].astype(BF16), w_out[wa:].astype(BF16)]
    return pl.pallas_call(
        functools.partial(_ffn_kernel, hidden_tile=tf, sub_tile=FFN_SUB_TILE),
        grid=(t // tm,),
        in_specs=in_specs,
        out_specs=pl.BlockSpec((tm, d), row),
        out_shape=jax.ShapeDtypeStruct((t, d), F32),
        compiler_params=_compiler_params(("parallel",)),
        name="ffn",
    )(*args)


def _inproj_kernel(x_ref, g_ref, w_ref, wqt_ref, wvt_ref, wgate_ref, gqcol_ref, gk_ref, grp_ref, gbias_ref,
                   qat_ref, ka_ref, vat_ref, qkb_ref, vb_ref, ob_ref, gates_ref, *, a_width, b_width):
    h = _rms_normalize(x_ref[...], g_ref[...]).astype(BF16)

    qt = _dot_nt(wqt_ref[...], h)
    for hd in range(a_width // A_HEAD_DIM):
        rows = slice(hd * A_HEAD_DIM, (hd + 1) * A_HEAD_DIM)
        y = qt[rows, :]
        ms = jnp.mean(y * y, axis=0, keepdims=True)
        gain = gqcol_ref[rows, :] * (A_HEAD_DIM ** -0.5 * LOG2E)
        qat_ref[rows, :] = (y * lax.rsqrt(ms + RMS_EPS) * gain).astype(BF16)

    y = _dot(h, w_ref[:, 0:a_width])
    sq_hi, sq_lo = _split_bf16(y * y, 2)
    ssq = _dot(sq_hi, grp_ref[...]) + _dot(sq_lo, grp_ref[...])
    ka_ref[...] = (y * lax.rsqrt(ssq * (1.0 / A_HEAD_DIM) + RMS_EPS) * gk_ref[...]).astype(BF16)

    vat_ref[...] = _dot_nt(wvt_ref[...], h).astype(BF16)
    c0 = a_width
    qkb_ref[...] = _dot(h, w_ref[:, c0:c0 + 2 * b_width]).astype(BF16)
    c0 += 2 * b_width
    vb_ref[...] = _dot(h, w_ref[:, c0:c0 + b_width]).astype(BF16)
    c0 += b_width
    ob_ref[...] = _dot(h, w_ref[:, c0:c0 + b_width]).astype(BF16)
    gates = _dot(h, wgate_ref[...]) + gbias_ref[...]
    gates_ref[...] = gates[:, :2 * B_HEADS]


def _inproj(x, gain, w_in, g_q, g_k, b_i, b_f):
    t, d = x.shape
    a_width = A_HEADS * A_HEAD_DIM
    b_width = B_HEADS * B_HEAD_DIM
    n_main = a_width + 4 * b_width
    o_ib = 3 * a_width + 3 * b_width
    tm = PROJ_TOKEN_TILE
    assert t % tm == 0
    w_main = jnp.concatenate([w_in[:, a_width:2 * a_width], w_in[:, 3 * a_width:o_ib],
                              w_in[:, o_ib + 2 * B_HEADS:]], axis=1).astype(BF16)
    w_qt = w_in[:, :a_width].T.astype(BF16)
    w_vt = w_in[:, 2 * a_width:3 * a_width].T.astype(BF16)
    w_gate = jnp.pad(w_in[:, o_ib:o_ib + 2 * B_HEADS], ((0, 0), (0, LANES - 2 * B_HEADS))).astype(BF16)
    gate_bias = jnp.pad(jnp.concatenate([b_i, b_f]).astype(F32), (0, LANES - 2 * B_HEADS)).reshape(1, LANES)
    head_id = jnp.arange(a_width) // A_HEAD_DIM
    grp = (head_id[:, None] == head_id[None, :]).astype(BF16)
    gq_col = jnp.tile(g_q.astype(F32), A_HEADS).reshape(a_width, 1)
    gk = jnp.tile(g_k.astype(F32), A_HEADS).reshape(1, a_width)
    const = lambda i: (0, 0)
    row = lambda i: (i, 0)
    col = lambda i: (0, i)
    outs = pl.pallas_call(
        functools.partial(_inproj_kernel, a_width=a_width, b_width=b_width),
        grid=(t // tm,),
        in_specs=[
            pl.BlockSpec((tm, d), row),
            pl.BlockSpec((1, d), const),
            pl.BlockSpec((d, n_main), const),
            pl.BlockSpec((a_width, d), const),
            pl.BlockSpec((a_width, d), const),
            pl.BlockSpec((d, LANES), const),
            pl.BlockSpec((a_width, 1), const),
            pl.BlockSpec((1, a_width), const),
            pl.BlockSpec((a_width, a_width), const),
            pl.BlockSpec((1, LANES), const),
        ],
        out_specs=[
            pl.BlockSpec((a_width, tm), col),
            pl.BlockSpec((tm, a_width), row),
            pl.BlockSpec((a_width, tm), col),
            pl.BlockSpec((tm, 2 * b_width), row),
            pl.BlockSpec((tm, b_width), row),
            pl.BlockSpec((tm, b_width), row),
            pl.BlockSpec((tm, 2 * B_HEADS), row),
        ],
        out_shape=[
            jax.ShapeDtypeStruct((a_width, t), BF16),
            jax.ShapeDtypeStruct((t, a_width), BF16),
            jax.ShapeDtypeStruct((a_width, t), BF16),
            jax.ShapeDtypeStruct((t, 2 * b_width), BF16),
            jax.ShapeDtypeStruct((t, b_width), BF16),
            jax.ShapeDtypeStruct((t, b_width), BF16),
            jax.ShapeDtypeStruct((t, 2 * B_HEADS), F32),
        ],
        compiler_params=_compiler_params(("parallel",)),
        name="inproj",
    )(x, gain.reshape(1, d), w_main, w_qt, w_vt, w_gate, gq_col, gk, grp, gate_bias)
    return outs


def _moba_kernel(slope_ref, qt_ref, k_ref, vt_ref, o_ref, kaug_ref, vaug_ref, kmean_ref, qaug_ref, acc_ref,
                 sa_ref, sb_ref, *, nb, grp, heads):
    hg = pl.program_id(1)
    c = pl.program_id(2)
    blk = MOBA_BLOCK
    dh = A_HEAD_DIM
    nbp = -(-nb // 16) * 16

    @pl.when(c == 0)
    def _build_key_value_side():
        ones_rows = jnp.where(lax.broadcasted_iota(jnp.int32, (V_AUG_ROWS - dh, blk), 0) == 0,
                              1.0, 0.0).astype(BF16)
        for h in range(heads):
            for j in range(nb):
                g, off = divmod(j, grp)
                vaug_ref[h, g, 0:dh, off * blk:(off + 1) * blk] = vt_ref[dh * h:dh * (h + 1),
                                                                         j * blk:(j + 1) * blk]
                vaug_ref[h, g, dh:V_AUG_ROWS, off * blk:(off + 1) * blk] = ones_rows
        kmean_ref[...] = jnp.zeros_like(kmean_ref)
        lane = lax.broadcasted_iota(jnp.int32, (blk, LANES), 1)
        row = lax.broadcasted_iota(jnp.int32, (blk, LANES), 0)

        def body(j, carry):
            start = pl.multiple_of(j * blk, blk)
            pos = (row + j * blk).astype(F32)
            for pr in range(heads // 2):
                kb = k_ref[pl.ds(start, blk), pr * LANES:(pr + 1) * LANES].astype(F32)
                kmean_ref[pr, pl.ds(j, 1), :] = jnp.mean(kb, axis=0, keepdims=True)
                for hh in range(2):
                    h = 2 * pr + hh
                    slope = slope_ref[pl.ds(heads * hg + h, 1), :][:, 0:LANES] * LOG2E
                    p1, p2, p3 = _split_bf16(slope * pos, 3)
                    rel = lane - dh * (1 - hh)
                    aug = jnp.where(rel == j, 1.0, 0.0)
                    aug = jnp.where(rel == nb, p1.astype(F32), aug)
                    aug = jnp.where(rel == nb + 1, p2.astype(F32), aug)
                    aug = jnp.where(rel == nb + 2, p3.astype(F32), aug)
                    aug = jnp.where((rel >= nb + 3) & (rel < nb + 6), 1.0, aug)
                    is_data = (lane >= dh * hh) & (lane < dh * (hh + 1))
                    kaug_ref[h, pl.ds(start, blk), :] = jnp.where(is_data, kb, aug).astype(BF16)
            return carry

        lax.fori_loop(0, nb, body, 0)

    blk_ix = lax.broadcasted_iota(jnp.int32, (nbp, blk), 0)
    blk_f = blk_ix.astype(F32)
    valid = blk_ix < c
    aug_row = lax.broadcasted_iota(jnp.int32, (dh, blk), 0)
    tq = (lax.broadcasted_iota(jnp.int32, (dh, blk), 1) + c * blk).astype(F32)
    lane_k = lax.broadcasted_iota(jnp.int32, (nbp, LANES), 1)
    own_grp = c // grp
    key_ix = lax.broadcasted_iota(jnp.int32, (grp * blk, blk), 0) + own_grp * (grp * blk)
    qry_ix = lax.broadcasted_iota(jnp.int32, (grp * blk, blk), 1) + c * blk
    gates = []
    for h in range(heads):
        pr, hh = divmod(h, 2)
        qt_pair = qt_ref[pr * LANES:(pr + 1) * LANES, :]
        is_data_k = (lane_k >= dh * hh) & (lane_k < dh * (hh + 1))
        km_hi, km_lo = _split_bf16(jnp.where(is_data_k, kmean_ref[pr, 0:nbp, :], 0.0), 2)
        gates.append(_dot(km_hi, qt_pair) + _dot(km_lo, qt_pair))
    for h in range(heads):
        pr, hh = divmod(h, 2)
        gate = jnp.where(valid, gates[h], -jnp.inf)
        chosen = blk_ix == c
        for _ in range(MOBA_TOPK):
            best = jnp.max(gate, axis=0, keepdims=True)
            first = jnp.min(jnp.where(gate == best, blk_f, float(nbp)), axis=0, keepdims=True)
            pick = blk_f == first
            chosen = chosen | (pick & valid)
            gate = jnp.where(pick, -jnp.inf, gate)
        bias = jnp.where(chosen, 0.0, MASK_VALUE)
        if nbp < dh:
            bias = jnp.concatenate([bias, jnp.zeros((dh - nbp, blk), F32)], axis=0)
        slope = slope_ref[pl.ds(heads * hg + h, 1), :] * LOG2E
        t1, t2, t3 = _split_bf16(-slope * tq, 3)
        aug = jnp.where(aug_row < nb, bias, 0.0)
        aug = jnp.where((aug_row >= nb) & (aug_row < nb + 3), 1.0, aug)
        aug = jnp.where(aug_row == nb + 3, t1.astype(F32), aug)
        aug = jnp.where(aug_row == nb + 4, t2.astype(F32), aug)
        aug = jnp.where(aug_row == nb + 5, t3.astype(F32), aug).astype(BF16)
        data = qt_ref[dh * h:dh * (h + 1), :]
        qaug_ref[h] = jnp.concatenate([data, aug] if hh == 0 else [aug, data], axis=0)

    def scores_into(buf_ref, g):
        start = pl.multiple_of(g * (grp * blk), grp * blk)
        for h in range(heads):
            buf_ref[h] = _dot(kaug_ref[h, pl.ds(start, grp * blk), :], qaug_ref[h])

    def consume(buf_ref, g, ms, own):
        new_ms = []
        for h in range(heads):
            s = buf_ref[h]
            if own:
                s = jnp.where(key_ix <= qry_ix, s, MASK_VALUE)
            m_new = jnp.maximum(ms[h], jnp.max(s, axis=0, keepdims=True))
            alpha = jnp.exp2(ms[h] - m_new)
            p = jnp.exp2(s - m_new)
            acc_ref[h] = alpha * acc_ref[h] + _dot(vaug_ref[h, g], p.astype(BF16))
            new_ms.append(m_new)
        return tuple(new_ms)

    def write_output():
        outs = []
        for h in range(heads):
            acc = acc_ref[h]
            outs.append(acc[0:dh, :] / acc[dh:dh + 1, :])
        o_ref[...] = jnp.concatenate(outs, axis=0).T.astype(o_ref.dtype)

    acc_ref[...] = jnp.zeros_like(acc_ref)
    masked_score = jnp.full((1, blk), MASK_VALUE, F32).astype(BF16).astype(F32)
    scores_into(sa_ref, 0)

    def two_groups(i, ms):
        scores_into(sb_ref, 2 * i + 1)
        ms = consume(sa_ref, 2 * i, ms, False)
        scores_into(sa_ref, 2 * i + 2)
        return consume(sb_ref, 2 * i + 1, ms, False)

    ms = lax.fori_loop(0, own_grp // 2, two_groups, (masked_score,) * heads)

    @pl.when(own_grp % 2 == 0)
    def _own_group_in_a():
        consume(sa_ref, own_grp, ms, True)
        write_output()

    @pl.when(own_grp % 2 == 1)
    def _own_group_in_b():
        scores_into(sb_ref, own_grp)
        consume(sb_ref, own_grp, consume(sa_ref, own_grp - 1, ms, False), True)
        write_output()


def _moba(qat, ka, vat, bsz, seq):
    t, a_width = ka.shape
    blk = MOBA_BLOCK
    assert seq % blk == 0 and seq // blk >= MOBA_TOPK
    nb = seq // blk
    assert nb + 6 <= A_HEAD_DIM
    grp = MOBA_KEY_GROUP if nb % MOBA_KEY_GROUP == 0 else 1
    heads = MOBA_HEADS_PER_STEP
    width = heads * A_HEAD_DIM
    assert heads % 2 == 0 and a_width % width == 0
    slopes = jnp.exp2(-8.0 * jnp.arange(1, A_HEADS + 1, dtype=F32) / A_HEADS)
    slope_tbl = jnp.broadcast_to(slopes[:, None], (A_HEADS, blk))
    return pl.pallas_call(
        functools.partial(_moba_kernel, nb=nb, grp=grp, heads=heads),
        grid=(bsz, a_width // width, nb),
        in_specs=[
            pl.BlockSpec((A_HEADS, blk), lambda b, hg, c: (0, 0)),
            pl.BlockSpec((width, blk), lambda b, hg, c: (hg, b * nb + c)),
            pl.BlockSpec((seq, width), lambda b, hg, c: (b, hg), pipeline_mode=pl.Buffered(1)),
            pl.BlockSpec((width, seq), lambda b, hg, c: (hg, b), pipeline_mode=pl.Buffered(1)),
        ],
        out_specs=pl.BlockSpec((blk, width), lambda b, hg, c: (b * nb + c, hg)),
        out_shape=jax.ShapeDtypeStruct((t, a_width), BF16),
        scratch_shapes=[
            pltpu.VMEM((heads, seq, LANES), BF16),
            pltpu.VMEM((heads, nb // grp, V_AUG_ROWS, grp * blk), BF16),
            pltpu.VMEM((heads // 2, LANES, LANES), F32),
            pltpu.VMEM((heads, LANES, blk), BF16),
            pltpu.VMEM((heads, V_AUG_ROWS, blk), F32),
            pltpu.VMEM((heads, grp * blk, blk), F32),
            pltpu.VMEM((heads, grp * blk, blk), F32),
        ],
        compiler_params=_compiler_params(("parallel", "parallel", "arbitrary")),
        name="moba",
    )(slope_tbl, qat, ka, vat)


def _mlstm_kernel(q_ref, k_ref, v_ref, ob_ref, gcol_ref, grow_ref, cwq_ref, cwk_ref, cbq_ref, cbk_ref,
                  o_ref, qbuf_ref, kbuf_ref, c_ref, m_ref):
    ci = pl.program_id(1)
    L, width = q_ref.shape
    d = B_HEAD_DIM
    halo = 8

    @pl.when(ci == 0)
    def _():
        qbuf_ref[0:halo, :] = jnp.zeros((halo, width), F32)
        kbuf_ref[0:halo, :] = jnp.zeros((halo, width), F32)
        c_ref[...] = jnp.zeros_like(c_ref)
        m_ref[...] = jnp.zeros_like(m_ref)

    def conv_silu(x_ref, buf_ref, w_ref, b_ref):
        buf_ref[halo:halo + L, :] = x_ref[...].astype(F32)
        y = b_ref[...]
        for j in range(CONV_WIDTH):
            off = halo - (CONV_WIDTH - 1) + j
            y = y + buf_ref[off:off + L, :] * w_ref[j:j + 1, :]
        buf_ref[0:halo, :] = buf_ref[L:L + halo, :]
        return y * jax.nn.sigmoid(y)

    q_all = (conv_silu(q_ref, qbuf_ref, cwq_ref, cbq_ref) * (d ** -0.5)).astype(BF16)
    k_all = conv_silu(k_ref, kbuf_ref, cwk_ref, cbk_ref)

    def log_sigmoid(z):
        return jnp.minimum(z, 0.0) - jnp.log1p(jnp.exp(-jnp.abs(z)))

    rr = lax.broadcasted_iota(jnp.int32, (L, L), 0)
    cc = lax.broadcasted_iota(jnp.int32, (L, L), 1)
    causal = rr >= cc
    tri_low = causal.astype(BF16)
    tri_up = (rr <= cc).astype(BF16)
    lane_l = lax.broadcasted_iota(jnp.int32, (L, LANES), 1)
    sub8 = lax.broadcasted_iota(jnp.int32, (8, L), 0)
    ones_col = jnp.where(lax.broadcasted_iota(jnp.int32, (L, d), 1) == 0, 1.0, 0.0).astype(BF16)
    gcol = gcol_ref[...]
    for hd in range(B_HEADS):
        cols = slice(hd * d, (hd + 1) * d)
        q = q_all[:, cols]
        k = k_all[:, cols]
        v_aug = jnp.concatenate([v_ref[:, cols], ones_col], axis=1)
        i_col = gcol[:, hd:hd + 1]
        f_col = gcol[:, hd + B_HEADS:hd + B_HEADS + 1]
        i_row = grow_ref[0, hd:hd + 1, :]
        f_row = grow_ref[0, hd + B_HEADS:hd + B_HEADS + 1, :]

        c_hi, c_lo = _split_bf16(jnp.where(lane_l == 0, log_sigmoid(f_col), 0.0), 2)
        b_col = (_dot(tri_low, c_hi) + _dot(tri_low, c_lo))[:, 0:1]
        r_hi, r_lo = _split_bf16(jnp.where(sub8 == 0, log_sigmoid(f_row), 0.0), 2)
        b_row = (_dot(r_hi, tri_up) + _dot(r_lo, tri_up))[0:1, :]

        m_prev = m_ref[hd]
        log_inter = b_col + m_prev
        dmat = jnp.where(causal, b_col - b_row + i_row, -jnp.inf)
        m_t = jnp.maximum(log_inter, jnp.max(dmat, axis=-1, keepdims=True))
        w_inter = jnp.exp(log_inter - m_t)
        sc = _dot_nt(q, k.astype(BF16)) * jnp.exp(dmat - m_t)
        c_prev = c_ref[hd]
        num_aug = w_inter * _dot(q, c_prev.astype(BF16)) + _dot(sc.astype(BF16), v_aug)
        num = num_aug[:, :d]
        den = num_aug[:, d:d + 1]
        hidden = num / jnp.maximum(jnp.abs(den), jnp.exp(-m_t))
        o_ref[:, cols] = (jax.nn.sigmoid(ob_ref[:, cols].astype(F32)) * hidden).astype(o_ref.dtype)

        b_last = b_col[L - 1:L, :]
        log_old = b_last + m_prev
        log_new = b_last - b_col + i_col
        m_new = jnp.maximum(log_old, jnp.max(log_new, axis=0, keepdims=True))
        a_old = jnp.exp(log_old - m_new)
        a_new = jnp.exp(log_new - m_new)
        c_ref[hd] = a_old * c_prev + _dot_tn((a_new * k).astype(BF16), v_aug)
        m_ref[hd] = m_new


def _mlstm(qkb, vb, ob, gates, conv_w, conv_b, bsz, seq):
    t, b_width = vb.shape
    d = B_HEAD_DIM
    L = min(MLSTM_KERNEL_CHUNK, seq)
    assert seq % L == 0 and d == LANES and b_width == B_HEADS * d
    nc = seq // L
    gates_row = gates.reshape(bsz, seq, 2 * B_HEADS).transpose(0, 2, 1)
    cw = conv_w.astype(F32)
    cb = conv_b.astype(F32).reshape(1, 2 * b_width)
    tok = lambda off: (lambda b, c: (b * nc + c, off))
    return pl.pallas_call(
        _mlstm_kernel,
        grid=(bsz, nc),
        in_specs=[
            pl.BlockSpec((L, b_width), tok(0)),
            pl.BlockSpec((L, b_width), tok(1)),
            pl.BlockSpec((L, b_width), tok(0)),
            pl.BlockSpec((L, b_width), tok(0)),
            pl.BlockSpec((L, 2 * B_HEADS), tok(0)),
            pl.BlockSpec((1, 2 * B_HEADS, L), lambda b, c: (b, 0, c)),
            pl.BlockSpec((CONV_WIDTH, b_width), lambda b, c: (0, 0)),
            pl.BlockSpec((CONV_WIDTH, b_width), lambda b, c: (0, 1)),
            pl.BlockSpec((1, b_width), lambda b, c: (0, 0)),
            pl.BlockSpec((1, b_width), lambda b, c: (0, 1)),
        ],
        out_specs=pl.BlockSpec((L, b_width), tok(0)),
        out_shape=jax.ShapeDtypeStruct((t, b_width), BF16),
        scratch_shapes=[
            pltpu.VMEM((L + 8, b_width), F32),
            pltpu.VMEM((L + 8, b_width), F32),
            pltpu.VMEM((B_HEADS, d, 2 * d), F32),
            pltpu.VMEM((B_HEADS, 1, 1), F32),
        ],
        compiler_params=_compiler_params(("parallel", "arbitrary")),
        name="mlstm",
    )(qkb, qkb, vb, ob, gates, gates_row, cw, cw, cb, cb)


def _pool_kernel(x_ref, halo_ref, g_ref, w_ref, scale_ref, o_ref, sums_ref, *, tiles_per_seq):
    i = pl.program_id(0)
    tm, d = x_ref.shape
    n_win = len(POOL_WINDOWS)
    grp = d // n_win
    base = POOL_PAD + POOL_HALO
    rows = POOL_HALO + tm
    x = x_ref[...]
    seq_tile = i % tiles_per_seq
    sums_ref[:, 0:POOL_PAD, :] = jnp.zeros((n_win, POOL_PAD, d), F32)
    halo_h = _rms_normalize(halo_ref[...], g_ref[...])
    sums_ref[0, POOL_PAD:base, :] = jnp.where(seq_tile == 0, 0.0, halo_h)
    h = _rms_normalize(x, g_ref[...])
    sums_ref[0, base:base + tm, :] = h
    t1 = (lax.broadcasted_iota(jnp.int32, (tm, 1), 0) + seq_tile * tm + 1).astype(F32)
    for k, win in enumerate(POOL_WINDOWS):
        half = win // 2
        c0 = k * grp
        if k + 1 < n_win:
            both = (sums_ref[k, POOL_PAD:POOL_PAD + rows, c0:]
                    + sums_ref[k, POOL_PAD - half:POOL_PAD - half + rows, c0:])
            sums_ref[k + 1, POOL_PAD:POOL_PAD + rows, c0:] = both
            total = both[POOL_HALO:, 0:grp]
        else:
            total = (sums_ref[k, base:base + tm, c0:c0 + grp]
                     + sums_ref[k, base - half:base - half + tm, c0:c0 + grp])
        pooled = total / jnp.minimum(t1, float(win)) - h[:, c0:c0 + grp]
        y = _dot(pooled.astype(BF16), w_ref[k])
        o_ref[:, c0:c0 + grp] = x[:, c0:c0 + grp] + y * scale_ref[:, c0:c0 + grp]


def _pool(x, gain, w_grp, scale, seq):
    t, d = x.shape
    tm = min(POOL_TOKEN_TILE, seq)
    assert seq % tm == 0 and tm % POOL_HALO == 0
    assert POOL_WINDOWS == tuple(2 ** (k + 1) for k in range(len(POOL_WINDOWS)))
    assert POOL_WINDOWS[-1] <= POOL_HALO and POOL_WINDOWS[-1] // 2 <= POOL_PAD
    n_grp, grp, _ = w_grp.shape
    halo_blocks = tm // POOL_HALO
    return pl.pallas_call(
        functools.partial(_pool_kernel, tiles_per_seq=seq // tm),
        grid=(t // tm,),
        in_specs=[
            pl.BlockSpec((tm, d), lambda i: (i, 0)),
            pl.BlockSpec((POOL_HALO, d), lambda i: (jnp.maximum(i * halo_blocks - 1, 0), 0)),
            pl.BlockSpec((1, d), lambda i: (0, 0)),
            pl.BlockSpec((n_grp, grp, grp), lambda i: (0, 0, 0)),
            pl.BlockSpec((1, d), lambda i: (0, 0)),
        ],
        out_specs=pl.BlockSpec((tm, d), lambda i: (i, 0)),
        out_shape=jax.ShapeDtypeStruct((t, d), F32),
        scratch_shapes=[pltpu.VMEM((len(POOL_WINDOWS), POOL_PAD + POOL_HALO + tm, d), F32)],
        compiler_params=_compiler_params(("parallel",)),
        name="pool",
    )(x, x, gain.reshape(1, d), w_grp.astype(BF16), scale.astype(F32).reshape(1, d))


def _mixer_heads(x, gain, w_in, g_q, g_k, conv_w, conv_b, b_i, b_f, bsz, seq):
    qat, ka, vat, qkb, vb, ob, gates = _inproj(x, gain, w_in, g_q, g_k, b_i, b_f)
    ya = _moba(qat, ka, vat, bsz, seq)
    yb = _mlstm(qkb, vb, ob, gates, conv_w, conv_b, bsz, seq)
    return ya, yb


def kernel(x, norm_g, ffn_w_gate, ffn_w_up, ffn_w_down, ab_w_in, ab_w_out, ab_g_q, ab_g_k,
           ab_conv_w, ab_conv_b, ab_b_i, ab_b_f, pool_w, pool_scale):
    bsz, seq, d = x.shape
    depth = norm_g.shape[0]
    y = x.reshape(bsz * seq, d)
    for layer in range(depth):
        y = _ffn(y, norm_g[layer, 0], ffn_w_gate[layer, 0], ffn_w_up[layer, 0], ffn_w_down[layer, 0])
        mixer_out = None
        if layer % 2 == 0:
            e = layer // 2
            ya, yb = _mixer_heads(y, norm_g[layer, 1], ab_w_in[e], ab_g_q[e], ab_g_k[e],
                                  ab_conv_w[e], ab_conv_b[e], ab_b_i[e], ab_b_f[e], bsz, seq)
            mixer_out = (ya, yb, ab_w_out[e])
        else:
            o = layer // 2
            y = _pool(y, norm_g[layer, 1], pool_w[o], pool_scale[o], seq)
        y = _ffn(y, norm_g[layer, 2], ffn_w_gate[layer, 1], ffn_w_up[layer, 1], ffn_w_down[layer, 1],
                 mixer_out=mixer_out)
    return y.reshape(bsz, seq, d)
```

```python
import functools

import jax
import jax.numpy as jnp
from jax import lax
from jax.experimental import pallas as pl
from jax.experimental.pallas import tpu as pltpu

F32 = jnp.float32
BF16 = jnp.bfloat16

RMS_EPS = 1e-6
A_HEADS = 8
A_HEAD_DIM = 64
MOBA_BLOCK = 256
MOBA_TOPK = 3
MOBA_KEY_GROUP = 2
MOBA_HEADS_PER_STEP = 4
B_HEADS = 4
B_HEAD_DIM = 128
CONV_WIDTH = 4
POOL_WINDOWS = (2, 4, 8, 16)
POOL_HALO = 16
POOL_PAD = 8
MLSTM_KERNEL_CHUNK = 256
LANES = 128
MASK_VALUE = -1e30
LOG2E = 1.4426950408889634
V_AUG_ROWS = A_HEAD_DIM + 16

FFN_TOKEN_TILE = 1024
FFN_SUB_TILE = 512
FFN_HIDDEN_TILE = 256
PROJ_TOKEN_TILE = 512
POOL_TOKEN_TILE = 512
VMEM_LIMIT_BYTES = 56 * 1024 * 1024


def _compiler_params(semantics):
    return pltpu.CompilerParams(dimension_semantics=semantics,
                                vmem_limit_bytes=VMEM_LIMIT_BYTES)


def _rms_normalize(x, gain):
    ms = jnp.mean(x * x, axis=-1, keepdims=True)
    return x * lax.rsqrt(ms + RMS_EPS) * gain


def _dot(a, b):
    return jnp.dot(a, b, preferred_element_type=F32)


def _dot_nt(a, b):
    return lax.dot_general(a, b, (((1,), (1,)), ((), ())), preferred_element_type=F32)


def _dot_tn(a, b):
    return lax.dot_general(a, b, (((0,), (0,)), ((), ())), preferred_element_type=F32)


def _split_bf16(x, parts):
    out = []
    rem = x
    for _ in range(parts):
        p = rem.astype(BF16)
        out.append(p)
        rem = rem - p.astype(F32)
    return out


def _ffn_kernel(x_ref, g_ref, wg_ref, wu_ref, wd_ref, *rest, hidden_tile, sub_tile):
    o_ref = rest[-1]
    d_ff = wg_ref.shape[1]
    for r0 in range(0, x_ref.shape[0], sub_tile):
        x = x_ref[r0:r0 + sub_tile, :]
        if len(rest) > 1:
            ya_ref, yb_ref, wa_ref, wb_ref = rest[:-1]
            x = (x + _dot(ya_ref[r0:r0 + sub_tile, :], wa_ref[...])
                 + _dot(yb_ref[r0:r0 + sub_tile, :], wb_ref[...]))
        h = _rms_normalize(x, g_ref[...]).astype(BF16)
        acc = None
        for c0 in range(0, d_ff, hidden_tile):
            gate = _dot(h, wg_ref[:, c0:c0 + hidden_tile])
            up = _dot(h, wu_ref[:, c0:c0 + hidden_tile])
            act = (gate * jax.nn.sigmoid(gate) * up).astype(BF16)
            part = _dot(act, wd_ref[c0:c0 + hidden_tile, :])
            acc = part if acc is None else acc + part
        o_ref[r0:r0 + sub_tile, :] = x + 0.5 * acc


def _resident(shape):
    return pl.BlockSpec(shape, lambda *_: (0,) * len(shape), pipeline_mode=pl.Buffered(1))


def _ffn(x, gain, w_gate, w_up, w_down, mixer_out=None):
    t, d = x.shape
    d_ff = w_gate.shape[1]
    tm, tf = FFN_TOKEN_TILE, FFN_HIDDEN_TILE
    assert t % tm == 0 and d_ff % tf == 0 and tm % FFN_SUB_TILE == 0
    row = lambda i: (i, 0)
    in_specs = [pl.BlockSpec((tm, d), row), _resident((1, d)), _resident((d, d_ff)),
                _resident((d, d_ff)), _resident((d_ff, d))]
    args = [x, gain.reshape(1, d), w_gate.astype(BF16), w_up.astype(BF16), w_down.astype(BF16)]
    if mixer_out is not None:
        ya, yb, w_out = mixer_out
        wa, wb = ya.shape[1], yb.shape[1]
        in_specs += [pl.BlockSpec((tm, wa), row), pl.BlockSpec((tm, wb), row),
                     _resident((wa, d)), _resident((wb, d))]
        args += [ya, yb, w_out[:wa].astype(BF16), w_out[wa:].astype(BF16)]
    return pl.pallas_call(
        functools.partial(_ffn_kernel, hidden_tile=tf, sub_tile=FFN_SUB_TILE),
        grid=(t // tm,),
        in_specs=in_specs,
        out_specs=pl.BlockSpec((tm, d), row),
        out_shape=jax.ShapeDtypeStruct((t, d), F32),
        compiler_params=_compiler_params(("parallel",)),
        name="ffn",
    )(*args)


def _inproj_kernel(x_ref, g_ref, w_ref, wqt_ref, wvt_ref, wgate_ref, gqcol_ref, gk_ref, grp_ref, gbias_ref,
                   qat_ref, ka_ref, vat_ref, qkb_ref, vb_ref, ob_ref, gates_ref, *, a_width, b_width):
    h = _rms_normalize(x_ref[...], g_ref[...]).astype(BF16)

    qt = _dot_nt(wqt_ref[...], h)
    for hd in range(a_width // A_HEAD_DIM):
        rows = slice(hd * A_HEAD_DIM, (hd + 1) * A_HEAD_DIM)
        y = qt[rows, :]
        ms = jnp.mean(y * y, axis=0, keepdims=True)
        gain = gqcol_ref[rows, :] * (A_HEAD_DIM ** -0.5 * LOG2E)
        qat_ref[rows, :] = (y * lax.rsqrt(ms + RMS_EPS) * gain).astype(BF16)

    y = _dot(h, w_ref[:, 0:a_width])
    sq_hi, sq_lo = _split_bf16(y * y, 2)
    ssq = _dot(sq_hi, grp_ref[...]) + _dot(sq_lo, grp_ref[...])
    ka_ref[...] = (y * lax.rsqrt(ssq * (1.0 / A_HEAD_DIM) + RMS_EPS) * gk_ref[...]).astype(BF16)

    vat_ref[...] = _dot_nt(wvt_ref[...], h).astype(BF16)
    c0 = a_width
    qkb_ref[...] = _dot(h, w_ref[:, c0:c0 + 2 * b_width]).astype(BF16)
    c0 += 2 * b_width
    vb_ref[...] = _dot(h, w_ref[:, c0:c0 + b_width]).astype(BF16)
    c0 += b_width
    ob_ref[...] = _dot(h, w_ref[:, c0:c0 + b_width]).astype(BF16)
    gates = _dot(h, wgate_ref[...]) + gbias_ref[...]
    gates_ref[...] = gates[:, :2 * B_HEADS]


def _inproj(x, gain, w_in, g_q, g_k, b_i, b_f):
    t, d = x.shape
    a_width = A_HEADS * A_HEAD_DIM
    b_width = B_HEADS * B_HEAD_DIM
    n_main = a_width + 4 * b_width
    o_ib = 3 * a_width + 3 * b_width
    tm = PROJ_TOKEN_TILE
    assert t % tm == 0
    w_main = jnp.concatenate([w_in[:, a_width:2 * a_width], w_in[:, 3 * a_width:o_ib],
                              w_in[:, o_ib + 2 * B_HEADS:]], axis=1).astype(BF16)
    w_qt = w_in[:, :a_width].T.astype(BF16)
    w_vt = w_in[:, 2 * a_width:3 * a_width].T.astype(BF16)
    w_gate = jnp.pad(w_in[:, o_ib:o_ib + 2 * B_HEADS], ((0, 0), (0, LANES - 2 * B_HEADS))).astype(BF16)
    gate_bias = jnp.pad(jnp.concatenate([b_i, b_f]).astype(F32), (0, LANES - 2 * B_HEADS)).reshape(1, LANES)
    head_id = jnp.arange(a_width) // A_HEAD_DIM
    grp = (head_id[:, None] == head_id[None, :]).astype(BF16)
    gq_col = jnp.tile(g_q.astype(F32), A_HEADS).reshape(a_width, 1)
    gk = jnp.tile(g_k.astype(F32), A_HEADS).reshape(1, a_width)
    const = lambda i: (0, 0)
    row = lambda i: (i, 0)
    col = lambda i: (0, i)
    outs = pl.pallas_call(
        functools.partial(_inproj_kernel, a_width=a_width, b_width=b_width),
        grid=(t // tm,),
        in_specs=[
            pl.BlockSpec((tm, d), row),
            pl.BlockSpec((1, d), const),
            pl.BlockSpec((d, n_main), const),
            pl.BlockSpec((a_width, d), const),
            pl.BlockSpec((a_width, d), const),
            pl.BlockSpec((d, LANES), const),
            pl.BlockSpec((a_width, 1), const),
            pl.BlockSpec((1, a_width), const),
            pl.BlockSpec((a_width, a_width), const),
            pl.BlockSpec((1, LANES), const),
        ],
        out_specs=[
            pl.BlockSpec((a_width, tm), col),
            pl.BlockSpec((tm, a_width), row),
            pl.BlockSpec((a_width, tm), col),
            pl.BlockSpec((tm, 2 * b_width), row),
            pl.BlockSpec((tm, b_width), row),
            pl.BlockSpec((tm, b_width), row),
            pl.BlockSpec((tm, 2 * B_HEADS), row),
        ],
        out_shape=[
            jax.ShapeDtypeStruct((a_width, t), BF16),
            jax.ShapeDtypeStruct((t, a_width), BF16),
            jax.ShapeDtypeStruct((a_width, t), BF16),
            jax.ShapeDtypeStruct((t, 2 * b_width), BF16),
            jax.ShapeDtypeStruct((t, b_width), BF16),
            jax.ShapeDtypeStruct((t, b_width), BF16),
            jax.ShapeDtypeStruct((t, 2 * B_HEADS), F32),
        ],
        compiler_params=_compiler_params(("parallel",)),
        name="inproj",
    )(x, gain.reshape(1, d), w_main, w_qt, w_vt, w_gate, gq_col, gk, grp, gate_bias)
    return outs


def _moba_kernel(slope_ref, qt_ref, k_ref, vt_ref, o_ref, kaug_ref, vaug_ref, kmean_ref, qaug_ref, acc_ref,
                 sa_ref, sb_ref, *, nb, grp, heads, tiles):
    hg = pl.program_id(1)
    c0 = pl.program_id(2) * tiles
    blk = MOBA_BLOCK
    dh = A_HEAD_DIM
    nbp = -(-nb // 16) * 16
    streams = [(ti, h) for ti in range(tiles) for h in range(heads)]

    @pl.when(c0 == 0)
    def _build_key_value_side():
        ones_rows = jnp.where(lax.broadcasted_iota(jnp.int32, (V_AUG_ROWS - dh, blk), 0) == 0,
                              1.0, 0.0).astype(BF16)
        for h in range(heads):
            for j in range(nb):
                g, off = divmod(j, grp)
                vaug_ref[h, g, 0:dh, off * blk:(off + 1) * blk] = vt_ref[dh * h:dh * (h + 1),
                                                                         j * blk:(j + 1) * blk]
                vaug_ref[h, g, dh:V_AUG_ROWS, off * blk:(off + 1) * blk] = ones_rows
        kmean_ref[...] = jnp.zeros_like(kmean_ref)
        lane = lax.broadcasted_iota(jnp.int32, (blk, LANES), 1)
        row = lax.broadcasted_iota(jnp.int32, (blk, LANES), 0)

        def body(j, carry):
            start = pl.multiple_of(j * blk, blk)
            pos = (row + j * blk).astype(F32)
            for pr in range(heads // 2):
                kb = k_ref[pl.ds(start, blk), pr * LANES:(pr + 1) * LANES].astype(F32)
                kmean_ref[pr, pl.ds(j, 1), :] = jnp.mean(kb, axis=0, keepdims=True)
                for hh in range(2):
                    h = 2 * pr + hh
                    slope = slope_ref[pl.ds(heads * hg + h, 1), :][:, 0:LANES] * LOG2E
                    p1, p2, p3 = _split_bf16(slope * pos, 3)
                    rel = lane - dh * (1 - hh)
                    aug = jnp.where(rel == j, 1.0, 0.0)
                    aug = jnp.where(rel == nb, p1.astype(F32), aug)
                    aug = jnp.where(rel == nb + 1, p2.astype(F32), aug)
                    aug = jnp.where(rel == nb + 2, p3.astype(F32), aug)
                    aug = jnp.where((rel >= nb + 3) & (rel < nb + 6), 1.0, aug)
                    is_data = (lane >= dh * hh) & (lane < dh * (hh + 1))
                    kaug_ref[h, pl.ds(start, blk), :] = jnp.where(is_data, kb, aug).astype(BF16)
            return carry

        lax.fori_loop(0, nb, body, 0)

    blk_ix = lax.broadcasted_iota(jnp.int32, (nbp, blk), 0)
    blk_f = blk_ix.astype(F32)
    aug_row = lax.broadcasted_iota(jnp.int32, (dh, blk), 0)
    qry_lane = lax.broadcasted_iota(jnp.int32, (dh, blk), 1)
    lane_k = lax.broadcasted_iota(jnp.int32, (nbp, LANES), 1)
    own_grp = c0 // grp
    key_ix = lax.broadcasted_iota(jnp.int32, (grp * blk, blk), 0) + own_grp * (grp * blk)
    qry_ix = lax.broadcasted_iota(jnp.int32, (grp * blk, blk), 1)
    gates = []
    for ti, h in streams:
        pr, hh = divmod(h, 2)
        qt_pair = qt_ref[pr * LANES:(pr + 1) * LANES, ti * blk:(ti + 1) * blk]
        is_data_k = (lane_k >= dh * hh) & (lane_k < dh * (hh + 1))
        km_hi, km_lo = _split_bf16(jnp.where(is_data_k, kmean_ref[pr, 0:nbp, :], 0.0), 2)
        gates.append(_dot(km_hi, qt_pair) + _dot(km_lo, qt_pair))
    for si, (ti, h) in enumerate(streams):
        pr, hh = divmod(h, 2)
        c = c0 + ti
        valid = blk_ix < c
        tq = (qry_lane + c * blk).astype(F32)
        gate = jnp.where(valid, gates[si], -jnp.inf)
        chosen = blk_ix == c
        for _ in range(MOBA_TOPK):
            best = jnp.max(gate, axis=0, keepdims=True)
            first = jnp.min(jnp.where(gate == best, blk_f, float(nbp)), axis=0, keepdims=True)
            pick = blk_f == first
            chosen = chosen | (pick & valid)
            gate = jnp.where(pick, -jnp.inf, gate)
        bias = jnp.where(chosen, 0.0, MASK_VALUE)
        if nbp < dh:
            bias = jnp.concatenate([bias, jnp.zeros((dh - nbp, blk), F32)], axis=0)
        slope = slope_ref[pl.ds(heads * hg + h, 1), :] * LOG2E
        t1, t2, t3 = _split_bf16(-slope * tq, 3)
        aug = jnp.where(aug_row < nb, bias, 0.0)
        aug = jnp.where((aug_row >= nb) & (aug_row < nb + 3), 1.0, aug)
        aug = jnp.where(aug_row == nb + 3, t1.astype(F32), aug)
        aug = jnp.where(aug_row == nb + 4, t2.astype(F32), aug)
        aug = jnp.where(aug_row == nb + 5, t3.astype(F32), aug).astype(BF16)
        data = qt_ref[dh * h:dh * (h + 1), ti * blk:(ti + 1) * blk]
        qaug_ref[si] = jnp.concatenate([data, aug] if hh == 0 else [aug, data], axis=0)

    def scores_into(buf_ref, g):
        start = pl.multiple_of(g * (grp * blk), grp * blk)
        col_max = []
        for si, (ti, h) in enumerate(streams):
            s = _dot(kaug_ref[h, pl.ds(start, grp * blk), :], qaug_ref[si])
            buf_ref[si] = s
            col_max.append(jnp.max(s, axis=0, keepdims=True))
        return tuple(col_max)

    def consume(buf_ref, g, ms, col_max, own=False):
        new_ms = []
        for si, (ti, h) in enumerate(streams):
            s = buf_ref[si]
            if own:
                s = jnp.where(key_ix <= qry_ix + (c0 + ti) * blk, s, MASK_VALUE)
                group_max = jnp.max(s, axis=0, keepdims=True)
            else:
                group_max = col_max[si]
            m_new = jnp.maximum(ms[si], group_max)
            alpha = jnp.exp2(ms[si] - m_new)
            p = jnp.exp2(s - m_new)
            acc_ref[si] = alpha * acc_ref[si] + _dot(vaug_ref[h, g], p.astype(BF16))
            new_ms.append(m_new)
        return tuple(new_ms)

    def write_output():
        for ti in range(tiles):
            outs = []
            for h in range(heads):
                acc = acc_ref[ti * heads + h]
                outs.append(acc[0:dh, :] / acc[dh:dh + 1, :])
            o_ref[ti * blk:(ti + 1) * blk, :] = jnp.concatenate(outs, axis=0).T.astype(o_ref.dtype)

    acc_ref[...] = jnp.zeros_like(acc_ref)
    masked_score = jnp.full((1, blk), MASK_VALUE, F32).astype(BF16).astype(F32)
    max_a0 = scores_into(sa_ref, 0)

    def two_groups(i, carry):
        ms, max_a = carry
        max_b = scores_into(sb_ref, 2 * i + 1)
        ms = consume(sa_ref, 2 * i, ms, max_a)
        max_a = scores_into(sa_ref, 2 * i + 2)
        return consume(sb_ref, 2 * i + 1, ms, max_b), max_a

    ms, max_a = lax.fori_loop(0, own_grp // 2, two_groups, ((masked_score,) * len(streams), max_a0))

    @pl.when(own_grp % 2 == 0)
    def _own_group_in_a():
        consume(sa_ref, own_grp, ms, None, own=True)
        write_output()

    @pl.when(own_grp % 2 == 1)
    def _own_group_in_b():
        scores_into(sb_ref, own_grp)
        consume(sb_ref, own_grp, consume(sa_ref, own_grp - 1, ms, max_a), None, own=True)
        write_output()


def _moba(qat, ka, vat, bsz, seq):
    t, a_width = ka.shape
    blk = MOBA_BLOCK
    assert seq % blk == 0 and seq // blk >= MOBA_TOPK
    nb = seq // blk
    assert nb + 6 <= A_HEAD_DIM
    grp = MOBA_KEY_GROUP if nb % MOBA_KEY_GROUP == 0 else 1
    heads = MOBA_HEADS_PER_STEP
    width = heads * A_HEAD_DIM
    assert heads % 2 == 0 and a_width % width == 0
    tiles = grp
    n_streams = tiles * heads
    steps = nb // tiles
    slopes = jnp.exp2(-8.0 * jnp.arange(1, A_HEADS + 1, dtype=F32) / A_HEADS)
    slope_tbl = jnp.broadcast_to(slopes[:, None], (A_HEADS, blk))
    return pl.pallas_call(
        functools.partial(_moba_kernel, nb=nb, grp=grp, heads=heads, tiles=tiles),
        grid=(bsz, a_width // width, steps),
        in_specs=[
            pl.BlockSpec((A_HEADS, blk), lambda b, hg, c: (0, 0)),
            pl.BlockSpec((width, tiles * blk), lambda b, hg, c: (hg, b * steps + c)),
            pl.BlockSpec((seq, width), lambda b, hg, c: (b, hg), pipeline_mode=pl.Buffered(1)),
            pl.BlockSpec((width, seq), lambda b, hg, c: (hg, b), pipeline_mode=pl.Buffered(1)),
        ],
        out_specs=pl.BlockSpec((tiles * blk, width), lambda b, hg, c: (b * steps + c, hg)),
        out_shape=jax.ShapeDtypeStruct((t, a_width), BF16),
        scratch_shapes=[
            pltpu.VMEM((heads, seq, LANES), BF16),
            pltpu.VMEM((heads, nb // grp, V_AUG_ROWS, grp * blk), BF16),
            pltpu.VMEM((heads // 2, LANES, LANES), F32),
            pltpu.VMEM((n_streams, LANES, blk), BF16),
            pltpu.VMEM((n_streams, V_AUG_ROWS, blk), F32),
            pltpu.VMEM((n_streams, grp * blk, blk), F32),
            pltpu.VMEM((n_streams, grp * blk, blk), F32),
        ],
        compiler_params=_compiler_params(("parallel", "parallel", "arbitrary")),
        name="moba",
    )(slope_tbl, qat, ka, vat)


def _mlstm_kernel(q_ref, k_ref, v_ref, ob_ref, gcol_ref, grow_ref, cwq_ref, cwk_ref, cbq_ref, cbk_ref,
                  o_ref, qbuf_ref, kbuf_ref, c_ref, m_ref):
    ci = pl.program_id(1)
    L, width = q_ref.shape
    d = B_HEAD_DIM
    halo = 8

    @pl.when(ci == 0)
    def _():
        qbuf_ref[0:halo, :] = jnp.zeros((halo, width), F32)
        kbuf_ref[0:halo, :] = jnp.zeros((halo, width), F32)
        c_ref[...] = jnp.zeros_like(c_ref)
        m_ref[...] = jnp.zeros_like(m_ref)

    def conv_silu(x_ref, buf_ref, w_ref, b_ref):
        buf_ref[halo:halo + L, :] = x_ref[...].astype(F32)
        y = b_ref[...]
        for j in range(CONV_WIDTH):
            off = halo - (CONV_WIDTH - 1) + j
            y = y + buf_ref[off:off + L, :] * w_ref[j:j + 1, :]
        buf_ref[0:halo, :] = buf_ref[L:L + halo, :]
        return y * jax.nn.sigmoid(y)

    q_all = (conv_silu(q_ref, qbuf_ref, cwq_ref, cbq_ref) * (d ** -0.5)).astype(BF16)
    k_all = conv_silu(k_ref, kbuf_ref, cwk_ref, cbk_ref)

    def log_sigmoid(z):
        return jnp.minimum(z, 0.0) - jnp.log1p(jnp.exp(-jnp.abs(z)))

    rr = lax.broadcasted_iota(jnp.int32, (L, L), 0)
    cc = lax.broadcasted_iota(jnp.int32, (L, L), 1)
    causal = rr >= cc
    tri_low = causal.astype(BF16)
    tri_up = (rr <= cc).astype(BF16)
    lane_l = lax.broadcasted_iota(jnp.int32, (L, LANES), 1)
    sub8 = lax.broadcasted_iota(jnp.int32, (8, L), 0)
    ones_col = jnp.where(lax.broadcasted_iota(jnp.int32, (L, d), 1) == 0, 1.0, 0.0).astype(BF16)
    gcol = gcol_ref[...]
    for hd in range(B_HEADS):
        cols = slice(hd * d, (hd + 1) * d)
        q = q_all[:, cols]
        k = k_all[:, cols]
        v_aug = jnp.concatenate([v_ref[:, cols], ones_col], axis=1)
        i_col = gcol[:, hd:hd + 1]
        f_col = gcol[:, hd + B_HEADS:hd + B_HEADS + 1]
        i_row = grow_ref[0, hd:hd + 1, :]
        f_row = grow_ref[0, hd + B_HEADS:hd + B_HEADS + 1, :]

        c_hi, c_lo = _split_bf16(jnp.where(lane_l == 0, log_sigmoid(f_col), 0.0), 2)
        b_col = (_dot(tri_low, c_hi) + _dot(tri_low, c_lo))[:, 0:1]
        r_hi, r_lo = _split_bf16(jnp.where(sub8 == 0, log_sigmoid(f_row), 0.0), 2)
        b_row = (_dot(r_hi, tri_up) + _dot(r_lo, tri_up))[0:1, :]

        m_prev = m_ref[hd]
        log_inter = b_col + m_prev
        dmat = jnp.where(causal, b_col - b_row + i_row, -jnp.inf)
        m_t = jnp.maximum(log_inter, jnp.max(dmat, axis=-1, keepdims=True))
        w_inter = jnp.exp(log_inter - m_t)
        sc = _dot_nt(q, k.astype(BF16)) * jnp.exp(dmat - m_t)
        c_prev = c_ref[hd]
        num_aug = w_inter * _dot(q, c_prev.astype(BF16)) + _dot(sc.astype(BF16), v_aug)
        num = num_aug[:, :d]
        den = num_aug[:, d:d + 1]
        hidden = num / jnp.maximum(jnp.abs(den), jnp.exp(-m_t))
        o_ref[:, cols] = (jax.nn.sigmoid(ob_ref[:, cols].astype(F32)) * hidden).astype(o_ref.dtype)

        b_last = b_col[L - 1:L, :]
        log_old = b_last + m_prev
        log_new = b_last - b_col + i_col
        m_new = jnp.maximum(log_old, jnp.max(log_new, axis=0, keepdims=True))
        a_old = jnp.exp(log_old - m_new)
        a_new = jnp.exp(log_new - m_new)
        c_ref[hd] = a_old * c_prev + _dot_tn((a_new * k).astype(BF16), v_aug)
        m_ref[hd] = m_new


def _mlstm(qkb, vb, ob, gates, conv_w, conv_b, bsz, seq):
    t, b_width = vb.shape
    d = B_HEAD_DIM
    L = min(MLSTM_KERNEL_CHUNK, seq)
    assert seq % L == 0 and d == LANES and b_width == B_HEADS * d
    nc = seq // L
    gates_row = gates.reshape(bsz, seq, 2 * B_HEADS).transpose(0, 2, 1)
    cw = conv_w.astype(F32)
    cb = conv_b.astype(F32).reshape(1, 2 * b_width)
    tok = lambda off: (lambda b, c: (b * nc + c, off))
    return pl.pallas_call(
        _mlstm_kernel,
        grid=(bsz, nc),
        in_specs=[
            pl.BlockSpec((L, b_width), tok(0)),
            pl.BlockSpec((L, b_width), tok(1)),
            pl.BlockSpec((L, b_width), tok(0)),
            pl.BlockSpec((L, b_width), tok(0)),
            pl.BlockSpec((L, 2 * B_HEADS), tok(0)),
            pl.BlockSpec((1, 2 * B_HEADS, L), lambda b, c: (b, 0, c)),
            pl.BlockSpec((CONV_WIDTH, b_width), lambda b, c: (0, 0)),
            pl.BlockSpec((CONV_WIDTH, b_width), lambda b, c: (0, 1)),
            pl.BlockSpec((1, b_width), lambda b, c: (0, 0)),
            pl.BlockSpec((1, b_width), lambda b, c: (0, 1)),
        ],
        out_specs=pl.BlockSpec((L, b_width), tok(0)),
        out_shape=jax.ShapeDtypeStruct((t, b_width), BF16),
        scratch_shapes=[
            pltpu.VMEM((L + 8, b_width), F32),
            pltpu.VMEM((L + 8, b_width), F32),
            pltpu.VMEM((B_HEADS, d, 2 * d), F32),
            pltpu.VMEM((B_HEADS, 1, 1), F32),
        ],
        compiler_params=_compiler_params(("parallel", "arbitrary")),
        name="mlstm",
    )(qkb, qkb, vb, ob, gates, gates_row, cw, cw, cb, cb)


def _pool_kernel(x_ref, halo_ref, g_ref, w_ref, scale_ref, o_ref, sums_ref, *, tiles_per_seq):
    i = pl.program_id(0)
    tm, d = x_ref.shape
    n_win = len(POOL_WINDOWS)
    grp = d // n_win
    base = POOL_PAD + POOL_HALO
    rows = POOL_HALO + tm
    x = x_ref[...]
    seq_tile = i % tiles_per_seq
    sums_ref[:, 0:POOL_PAD, :] = jnp.zeros((n_win, POOL_PAD, d), F32)
    halo_h = _rms_normalize(halo_ref[...], g_ref[...])
    sums_ref[0, POOL_PAD:base, :] = jnp.where(seq_tile == 0, 0.0, halo_h)
    h = _rms_normalize(x, g_ref[...])
    sums_ref[0, base:base + tm, :] = h
    t1 = (lax.broadcasted_iota(jnp.int32, (tm, 1), 0) + seq_tile * tm + 1).astype(F32)
    for k, win in enumerate(POOL_WINDOWS):
        half = win // 2
        c0 = k * grp
        if k + 1 < n_win:
            both = (sums_ref[k, POOL_PAD:POOL_PAD + rows, c0:]
                    + sums_ref[k, POOL_PAD - half:POOL_PAD - half + rows, c0:])
            sums_ref[k + 1, POOL_PAD:POOL_PAD + rows, c0:] = both
            total = both[POOL_HALO:, 0:grp]
        else:
            total = (sums_ref[k, base:base + tm, c0:c0 + grp]
                     + sums_ref[k, base - half:base - half + tm, c0:c0 + grp])
        pooled = total / jnp.minimum(t1, float(win)) - h[:, c0:c0 + grp]
        y = _dot(pooled.astype(BF16), w_ref[k])
        o_ref[:, c0:c0 + grp] = x[:, c0:c0 + grp] + y * scale_ref[:, c0:c0 + grp]


def _pool(x, gain, w_grp, scale, seq):
    t, d = x.shape
    tm = min(POOL_TOKEN_TILE, seq)
    assert seq % tm == 0 and tm % POOL_HALO == 0
    assert POOL_WINDOWS == tuple(2 ** (k + 1) for k in range(len(POOL_WINDOWS)))
    assert POOL_WINDOWS[-1] <= POOL_HALO and POOL_WINDOWS[-1] // 2 <= POOL_PAD
    n_grp, grp, _ = w_grp.shape
    halo_blocks = tm // POOL_HALO
    return pl.pallas_call(
        functools.partial(_pool_kernel, tiles_per_seq=seq // tm),
        grid=(t // tm,),
        in_specs=[
            pl.BlockSpec((tm, d), lambda i: (i, 0)),
            pl.BlockSpec((POOL_HALO, d), lambda i: (jnp.maximum(i * halo_blocks - 1, 0), 0)),
            pl.BlockSpec((1, d), lambda i: (0, 0)),
            pl.BlockSpec((n_grp, grp, grp), lambda i: (0, 0, 0)),
            pl.BlockSpec((1, d), lambda i: (0, 0)),
        ],
        out_specs=pl.BlockSpec((tm, d), lambda i: (i, 0)),
        out_shape=jax.ShapeDtypeStruct((t, d), F32),
        scratch_shapes=[pltpu.VMEM((len(POOL_WINDOWS), POOL_PAD + POOL_HALO + tm, d), F32)],
        compiler_params=_compiler_params(("parallel",)),
        name="pool",
    )(x, x, gain.reshape(1, d), w_grp.astype(BF16), scale.astype(F32).reshape(1, d))


def _mixer_heads(x, gain, w_in, g_q, g_k, conv_w, conv_b, b_i, b_f, bsz, seq):
    qat, ka, vat, qkb, vb, ob, gates = _inproj(x, gain, w_in, g_q, g_k, b_i, b_f)
    ya = _moba(qat, ka, vat, bsz, seq)
    yb = _mlstm(qkb, vb, ob, gates, conv_w, conv_b, bsz, seq)
    return ya, yb


def kernel(x, norm_g, ffn_w_gate, ffn_w_up, ffn_w_down, ab_w_in, ab_w_out, ab_g_q, ab_g_k,
           ab_conv_w, ab_conv_b, ab_b_i, ab_b_f, pool_w, pool_scale):
    bsz, seq, d = x.shape
    depth = norm_g.shape[0]
    y = x.reshape(bsz * seq, d)
    for layer in range(depth):
        y = _ffn(y, norm_g[layer, 0], ffn_w_gate[layer, 0], ffn_w_up[layer, 0], ffn_w_down[layer, 0])
        mixer_out = None
        if layer % 2 == 0:
            e = layer // 2
            ya, yb = _mixer_heads(y, norm_g[layer, 1], ab_w_in[e], ab_g_q[e], ab_g_k[e],
                                  ab_conv_w[e], ab_conv_b[e], ab_b_i[e], ab_b_f[e], bsz, seq)
            mixer_out = (ya, yb, ab_w_out[e])
        else:
            o = layer // 2
            y = _pool(y, norm_g[layer, 1], pool_w[o], pool_scale[o], seq)
        y = _ffn(y, norm_g[layer, 2], ffn_w_gate[layer, 1], ffn_w_up[layer, 1], ffn_w_down[layer, 1],
                 mixer_out=mixer_out)
    return y.reshape(bsz, seq, d)
```

```python
import functools

import jax
import jax.numpy as jnp
from jax import lax
from jax.experimental import pallas as pl
from jax.experimental.pallas import tpu as pltpu

F32 = jnp.float32
BF16 = jnp.bfloat16

LANES = 128
SUBLANES_F32 = 8
SUBLANES_BF16 = 16
MXU_WIDTH = 256
VMEM_BYTES = 64 * 1024 * 1024
VMEM_LIMIT_BYTES = VMEM_BYTES * 7 // 8

RMS_EPS = 1e-6
A_HEADS = 8
A_HEAD_DIM = 64
MOBA_BLOCK = 256
MOBA_TOPK = 3
B_HEADS = 4
B_HEAD_DIM = 128
CONV_WIDTH = 4
POOL_WINDOWS = (2, 4, 8, 16)

MOBA_KEY_GROUP = 2
MOBA_HEADS_PER_STEP = 4
MLSTM_KERNEL_CHUNK = 256
POOL_HALO = 16
POOL_PAD = SUBLANES_F32
MASK_VALUE = -1e30
LOG2E = 1.4426950408889634
V_AUG_ROWS = A_HEAD_DIM + SUBLANES_BF16
FFN_TOKEN_TILE = 1024
FFN_SUB_TILE = 512
FFN_HIDDEN_TILE = MXU_WIDTH
PROJ_TOKEN_TILE = 512
POOL_TOKEN_TILE = 1024


def _compiler_params(semantics):
    return pltpu.CompilerParams(dimension_semantics=semantics,
                                vmem_limit_bytes=VMEM_LIMIT_BYTES)


def _rms_normalize(x, gain):
    ms = jnp.mean(x * x, axis=-1, keepdims=True)
    return x * lax.rsqrt(ms + RMS_EPS) * gain


def _dot(a, b):
    return jnp.dot(a, b, preferred_element_type=F32)


def _dot_nt(a, b):
    return lax.dot_general(a, b, (((1,), (1,)), ((), ())), preferred_element_type=F32)


def _dot_tn(a, b):
    return lax.dot_general(a, b, (((0,), (0,)), ((), ())), preferred_element_type=F32)


def _split_bf16(x, parts):
    out = []
    rem = x
    for _ in range(parts):
        p = rem.astype(BF16)
        out.append(p)
        rem = rem - p.astype(F32)
    return out


def _ffn_kernel(x_ref, g_ref, wg_ref, wu_ref, wd_ref, *rest, hidden_tile, sub_tile):
    o_ref = rest[-1]
    d_ff = wg_ref.shape[1]
    for r0 in range(0, x_ref.shape[0], sub_tile):
        x = x_ref[r0:r0 + sub_tile, :]
        if len(rest) > 1:
            ya_ref, yb_ref, wa_ref, wb_ref = rest[:-1]
            x = (x + _dot(ya_ref[r0:r0 + sub_tile, :], wa_ref[...])
                 + _dot(yb_ref[r0:r0 + sub_tile, :], wb_ref[...]))
        h = _rms_normalize(x, g_ref[...]).astype(BF16)
        acc = None
        for c0 in range(0, d_ff, hidden_tile):
            gate = _dot(h, wg_ref[:, c0:c0 + hidden_tile])
            up = _dot(h, wu_ref[:, c0:c0 + hidden_tile])
            act = (gate * jax.nn.sigmoid(gate) * up).astype(BF16)
            part = _dot(act, wd_ref[c0:c0 + hidden_tile, :])
            acc = part if acc is None else acc + part
        o_ref[r0:r0 + sub_tile, :] = x + 0.5 * acc


def _resident(shape):
    return pl.BlockSpec(shape, lambda *_: (0,) * len(shape), pipeline_mode=pl.Buffered(1))


def _ffn(x, gain, w_gate, w_up, w_down, mixer_out=None):
    t, d = x.shape
    d_ff = w_gate.shape[1]
    tm, tf = FFN_TOKEN_TILE, FFN_HIDDEN_TILE
    assert t % tm == 0 and d_ff % tf == 0 and tm % FFN_SUB_TILE == 0
    row = lambda i: (i, 0)
    in_specs = [pl.BlockSpec((tm, d), row), _resident((1, d)), _resident((d, d_ff)),
                _resident((d, d_ff)), _resident((d_ff, d))]
    args = [x, gain.reshape(1, d), w_gate.astype(BF16), w_up.astype(BF16), w_down.astype(BF16)]
    if mixer_out is not None:
        ya, yb, w_out = mixer_out
        wa, wb = ya.shape[1], yb.shape[1]
        in_specs += [pl.BlockSpec((tm, wa), row), pl.BlockSpec((tm, wb), row),
                     _resident((wa, d)), _resident((wb, d))]
        args += [ya, yb, w_out[:wa].astype(BF16), w_out[wa:].astype(BF16)]
    return pl.pallas_call(
        functools.partial(_ffn_kernel, hidden_tile=tf, sub_tile=FFN_SUB_TILE),
        grid=(t // tm,),
        in_specs=in_specs,
        out_specs=pl.BlockSpec((tm, d), row),
        out_shape=jax.ShapeDtypeStruct((t, d), F32),
        compiler_params=_compiler_params(("parallel",)),
        name="ffn",
    )(*args)


def _inproj_kernel(x_ref, g_ref, w_ref, wqt_ref, wvt_ref, wgate_ref, gqcol_ref, gk_ref, grp_ref, gbias_ref,
                   qat_ref, ka_ref, vat_ref, qkb_ref, vb_ref, ob_ref, gates_ref, *, a_width, b_width):
    h = _rms_normalize(x_ref[...], g_ref[...]).astype(BF16)

    qt = _dot_nt(wqt_ref[...], h)
    for hd in range(a_width // A_HEAD_DIM):
        rows = slice(hd * A_HEAD_DIM, (hd + 1) * A_HEAD_DIM)
        y = qt[rows, :]
        ms = jnp.mean(y * y, axis=0, keepdims=True)
        gain = gqcol_ref[rows, :] * (A_HEAD_DIM ** -0.5 * LOG2E)
        qat_ref[rows, :] = (y * lax.rsqrt(ms + RMS_EPS) * gain).astype(BF16)

    y = _dot(h, w_ref[:, 0:a_width])
    sq_hi, sq_lo = _split_bf16(y * y, 2)
    ssq = _dot(sq_hi, grp_ref[...]) + _dot(sq_lo, grp_ref[...])
    ka_ref[...] = (y * lax.rsqrt(ssq * (1.0 / A_HEAD_DIM) + RMS_EPS) * gk_ref[...]).astype(BF16)

    vat_ref[...] = _dot_nt(wvt_ref[...], h).astype(BF16)
    c0 = a_width
    qkb_ref[...] = _dot(h, w_ref[:, c0:c0 + 2 * b_width]).astype(BF16)
    c0 += 2 * b_width
    vb_ref[...] = _dot(h, w_ref[:, c0:c0 + b_width]).astype(BF16)
    c0 += b_width
    ob_ref[...] = _dot(h, w_ref[:, c0:c0 + b_width]).astype(BF16)
    gates = _dot(h, wgate_ref[...]) + gbias_ref[...]
    gates_ref[...] = gates[:, :2 * B_HEADS]


def _inproj(x, gain, w_in, g_q, g_k, b_i, b_f):
    t, d = x.shape
    a_width = A_HEADS * A_HEAD_DIM
    b_width = B_HEADS * B_HEAD_DIM
    n_main = a_width + 4 * b_width
    o_ib = 3 * a_width + 3 * b_width
    tm = PROJ_TOKEN_TILE
    assert t % tm == 0
    w_main = jnp.concatenate([w_in[:, a_width:2 * a_width], w_in[:, 3 * a_width:o_ib],
                              w_in[:, o_ib + 2 * B_HEADS:]], axis=1).astype(BF16)
    w_qt = w_in[:, :a_width].T.astype(BF16)
    w_vt = w_in[:, 2 * a_width:3 * a_width].T.astype(BF16)
    w_gate = jnp.pad(w_in[:, o_ib:o_ib + 2 * B_HEADS], ((0, 0), (0, LANES - 2 * B_HEADS))).astype(BF16)
    gate_bias = jnp.pad(jnp.concatenate([b_i, b_f]).astype(F32), (0, LANES - 2 * B_HEADS)).reshape(1, LANES)
    head_id = jnp.arange(a_width) // A_HEAD_DIM
    grp = (head_id[:, None] == head_id[None, :]).astype(BF16)
    gq_col = jnp.tile(g_q.astype(F32), A_HEADS).reshape(a_width, 1)
    gk = jnp.tile(g_k.astype(F32), A_HEADS).reshape(1, a_width)
    const = lambda i: (0, 0)
    row = lambda i: (i, 0)
    col = lambda i: (0, i)
    outs = pl.pallas_call(
        functools.partial(_inproj_kernel, a_width=a_width, b_width=b_width),
        grid=(t // tm,),
        in_specs=[
            pl.BlockSpec((tm, d), row),
            pl.BlockSpec((1, d), const),
            pl.BlockSpec((d, n_main), const),
            pl.BlockSpec((a_width, d), const),
            pl.BlockSpec((a_width, d), const),
            pl.BlockSpec((d, LANES), const),
            pl.BlockSpec((a_width, 1), const),
            pl.BlockSpec((1, a_width), const),
            pl.BlockSpec((a_width, a_width), const),
            pl.BlockSpec((1, LANES), const),
        ],
        out_specs=[
            pl.BlockSpec((a_width, tm), col),
            pl.BlockSpec((tm, a_width), row),
            pl.BlockSpec((a_width, tm), col),
            pl.BlockSpec((tm, 2 * b_width), row),
            pl.BlockSpec((tm, b_width), row),
            pl.BlockSpec((tm, b_width), row),
            pl.BlockSpec((tm, 2 * B_HEADS), row),
        ],
        out_shape=[
            jax.ShapeDtypeStruct((a_width, t), BF16),
            jax.ShapeDtypeStruct((t, a_width), BF16),
            jax.ShapeDtypeStruct((a_width, t), BF16),
            jax.ShapeDtypeStruct((t, 2 * b_width), BF16),
            jax.ShapeDtypeStruct((t, b_width), BF16),
            jax.ShapeDtypeStruct((t, b_width), BF16),
            jax.ShapeDtypeStruct((t, 2 * B_HEADS), F32),
        ],
        compiler_params=_compiler_params(("parallel",)),
        name="inproj",
    )(x, gain.reshape(1, d), w_main, w_qt, w_vt, w_gate, gq_col, gk, grp, gate_bias)
    return outs


def _moba_kernel(slope_ref, qt_ref, k_ref, vt_ref, o_ref, kaug_ref, vaug_ref, kmean_ref, qaug_ref, acc_ref,
                 sa_ref, sb_ref, *, nb, grp, heads, tiles):
    hg = pl.program_id(1)
    c0 = pl.program_id(2) * tiles
    blk = MOBA_BLOCK
    dh = A_HEAD_DIM
    nbp = pl.cdiv(nb, SUBLANES_BF16) * SUBLANES_BF16
    streams = [(ti, h) for ti in range(tiles) for h in range(heads)]

    @pl.when(c0 == 0)
    def _build_key_value_side():
        ones_rows = jnp.where(lax.broadcasted_iota(jnp.int32, (V_AUG_ROWS - dh, blk), 0) == 0,
                              1.0, 0.0).astype(BF16)
        for h in range(heads):
            for j in range(nb):
                g, off = divmod(j, grp)
                vaug_ref[h, g, 0:dh, off * blk:(off + 1) * blk] = vt_ref[dh * h:dh * (h + 1),
                                                                         j * blk:(j + 1) * blk]
                vaug_ref[h, g, dh:V_AUG_ROWS, off * blk:(off + 1) * blk] = ones_rows
        kmean_ref[...] = jnp.zeros_like(kmean_ref)
        lane = lax.broadcasted_iota(jnp.int32, (blk, LANES), 1)
        row = lax.broadcasted_iota(jnp.int32, (blk, LANES), 0)

        def body(j, carry):
            start = pl.multiple_of(j * blk, blk)
            pos = (row + j * blk).astype(F32)
            for pr in range(heads // 2):
                kb = k_ref[pl.ds(start, blk), pr * LANES:(pr + 1) * LANES].astype(F32)
                kmean_ref[pr, pl.ds(j, 1), :] = jnp.mean(kb, axis=0, keepdims=True)
                for hh in range(2):
                    h = 2 * pr + hh
                    slope = slope_ref[pl.ds(heads * hg + h, 1), :][:, 0:LANES] * LOG2E
                    p1, p2, p3 = _split_bf16(slope * pos, 3)
                    rel = lane - dh * (1 - hh)
                    aug = jnp.where(rel == j, 1.0, 0.0)
                    aug = jnp.where(rel == nb, p1.astype(F32), aug)
                    aug = jnp.where(rel == nb + 1, p2.astype(F32), aug)
                    aug = jnp.where(rel == nb + 2, p3.astype(F32), aug)
                    aug = jnp.where((rel >= nb + 3) & (rel < nb + 6), 1.0, aug)
                    is_data = (lane >= dh * hh) & (lane < dh * (hh + 1))
                    kaug_ref[h, pl.ds(start, blk), :] = jnp.where(is_data, kb, aug).astype(BF16)
            return carry

        lax.fori_loop(0, nb, body, 0)

    blk_ix = lax.broadcasted_iota(jnp.int32, (nbp, blk), 0)
    blk_f = blk_ix.astype(F32)
    aug_row = lax.broadcasted_iota(jnp.int32, (dh, blk), 0)
    qry_lane = lax.broadcasted_iota(jnp.int32, (dh, blk), 1)
    lane_k = lax.broadcasted_iota(jnp.int32, (nbp, LANES), 1)
    own_grp = c0 // grp
    key_ix = lax.broadcasted_iota(jnp.int32, (grp * blk, blk), 0) + own_grp * (grp * blk)
    qry_ix = lax.broadcasted_iota(jnp.int32, (grp * blk, blk), 1)
    gates = []
    for ti, h in streams:
        pr, hh = divmod(h, 2)
        qt_pair = qt_ref[pr * LANES:(pr + 1) * LANES, ti * blk:(ti + 1) * blk]
        is_data_k = (lane_k >= dh * hh) & (lane_k < dh * (hh + 1))
        km_hi, km_lo = _split_bf16(jnp.where(is_data_k, kmean_ref[pr, 0:nbp, :], 0.0), 2)
        gates.append(_dot(km_hi, qt_pair) + _dot(km_lo, qt_pair))
    for si, (ti, h) in enumerate(streams):
        pr, hh = divmod(h, 2)
        c = c0 + ti
        valid = blk_ix < c
        tq = (qry_lane + c * blk).astype(F32)
        gate = jnp.where(valid, gates[si], -jnp.inf)
        chosen = blk_ix == c
        for _ in range(MOBA_TOPK):
            best = jnp.max(gate, axis=0, keepdims=True)
            first = jnp.min(jnp.where(gate == best, blk_f, float(nbp)), axis=0, keepdims=True)
            pick = blk_f == first
            chosen = chosen | (pick & valid)
            gate = jnp.where(pick, -jnp.inf, gate)
        bias = jnp.where(chosen, 0.0, MASK_VALUE)
        if nbp < dh:
            bias = jnp.concatenate([bias, jnp.zeros((dh - nbp, blk), F32)], axis=0)
        slope = slope_ref[pl.ds(heads * hg + h, 1), :] * LOG2E
        t1, t2, t3 = _split_bf16(-slope * tq, 3)
        aug = jnp.where(aug_row < nb, bias, 0.0)
        aug = jnp.where((aug_row >= nb) & (aug_row < nb + 3), 1.0, aug)
        aug = jnp.where(aug_row == nb + 3, t1.astype(F32), aug)
        aug = jnp.where(aug_row == nb + 4, t2.astype(F32), aug)
        aug = jnp.where(aug_row == nb + 5, t3.astype(F32), aug).astype(BF16)
        data = qt_ref[dh * h:dh * (h + 1), ti * blk:(ti + 1) * blk]
        qaug_ref[si] = jnp.concatenate([data, aug] if hh == 0 else [aug, data], axis=0)

    def scores_into(buf_ref, g):
        start = pl.multiple_of(g * (grp * blk), grp * blk)
        col_max = []
        for si, (ti, h) in enumerate(streams):
            s = _dot(kaug_ref[h, pl.ds(start, grp * blk), :], qaug_ref[si])
            buf_ref[si] = s
            col_max.append(jnp.max(s, axis=0, keepdims=True))
        return tuple(col_max)

    def consume(buf_ref, g, ms, col_max, own=False):
        new_ms = []
        for si, (ti, h) in enumerate(streams):
            s = buf_ref[si]
            if own:
                s = jnp.where(key_ix <= qry_ix + (c0 + ti) * blk, s, MASK_VALUE)
                group_max = jnp.max(s, axis=0, keepdims=True)
            else:
                group_max = col_max[si]
            m_new = jnp.maximum(ms[si], group_max)
            alpha = jnp.exp2(ms[si] - m_new)
            p = jnp.exp2(s - m_new)
            acc_ref[si] = alpha * acc_ref[si] + _dot(vaug_ref[h, g], p.astype(BF16))
            new_ms.append(m_new)
        return tuple(new_ms)

    def write_output():
        for ti in range(tiles):
            outs = []
            for h in range(heads):
                acc = acc_ref[ti * heads + h]
                outs.append(acc[0:dh, :] / acc[dh:dh + 1, :])
            o_ref[ti * blk:(ti + 1) * blk, :] = jnp.concatenate(outs, axis=0).T.astype(o_ref.dtype)

    acc_ref[...] = jnp.zeros_like(acc_ref)
    masked_score = jnp.full((1, blk), MASK_VALUE, F32).astype(BF16).astype(F32)
    max_a0 = scores_into(sa_ref, 0)

    def two_groups(i, carry):
        ms, max_a = carry
        max_b = scores_into(sb_ref, 2 * i + 1)
        ms = consume(sa_ref, 2 * i, ms, max_a)
        max_a = scores_into(sa_ref, 2 * i + 2)
        return consume(sb_ref, 2 * i + 1, ms, max_b), max_a

    ms, max_a = lax.fori_loop(0, own_grp // 2, two_groups, ((masked_score,) * len(streams), max_a0))

    @pl.when(own_grp % 2 == 0)
    def _own_group_in_a():
        consume(sa_ref, own_grp, ms, None, own=True)
        write_output()

    @pl.when(own_grp % 2 == 1)
    def _own_group_in_b():
        scores_into(sb_ref, own_grp)
        consume(sb_ref, own_grp, consume(sa_ref, own_grp - 1, ms, max_a), None, own=True)
        write_output()


def _moba(qat, ka, vat, bsz, seq):
    t, a_width = ka.shape
    blk = MOBA_BLOCK
    assert seq % blk == 0 and seq // blk >= MOBA_TOPK
    nb = seq // blk
    assert nb + 6 <= A_HEAD_DIM
    grp = MOBA_KEY_GROUP if nb % MOBA_KEY_GROUP == 0 else 1
    heads = MOBA_HEADS_PER_STEP
    width = heads * A_HEAD_DIM
    assert heads % 2 == 0 and a_width % width == 0
    tiles = grp
    n_streams = tiles * heads
    steps = nb // tiles
    slopes = jnp.exp2(-8.0 * jnp.arange(1, A_HEADS + 1, dtype=F32) / A_HEADS)
    slope_tbl = jnp.broadcast_to(slopes[:, None], (A_HEADS, blk))
    return pl.pallas_call(
        functools.partial(_moba_kernel, nb=nb, grp=grp, heads=heads, tiles=tiles),
        grid=(bsz, a_width // width, steps),
        in_specs=[
            pl.BlockSpec((A_HEADS, blk), lambda b, hg, c: (0, 0)),
            pl.BlockSpec((width, tiles * blk), lambda b, hg, c: (hg, b * steps + c)),
            pl.BlockSpec((seq, width), lambda b, hg, c: (b, hg), pipeline_mode=pl.Buffered(1)),
            pl.BlockSpec((width, seq), lambda b, hg, c: (hg, b), pipeline_mode=pl.Buffered(1)),
        ],
        out_specs=pl.BlockSpec((tiles * blk, width), lambda b, hg, c: (b * steps + c, hg)),
        out_shape=jax.ShapeDtypeStruct((t, a_width), BF16),
        scratch_shapes=[
            pltpu.VMEM((heads, seq, LANES), BF16),
            pltpu.VMEM((heads, nb // grp, V_AUG_ROWS, grp * blk), BF16),
            pltpu.VMEM((heads // 2, LANES, LANES), F32),
            pltpu.VMEM((n_streams, LANES, blk), BF16),
            pltpu.VMEM((n_streams, V_AUG_ROWS, blk), F32),
            pltpu.VMEM((n_streams, grp * blk, blk), F32),
            pltpu.VMEM((n_streams, grp * blk, blk), F32),
        ],
        compiler_params=_compiler_params(("parallel", "parallel", "arbitrary")),
        name="moba",
    )(slope_tbl, qat, ka, vat)


def _mlstm_kernel(q_ref, k_ref, v_ref, ob_ref, gcol_ref, grow_ref, cwq_ref, cwk_ref, cbq_ref, cbk_ref,
                  o_ref, qbuf_ref, kbuf_ref, c_ref, m_ref):
    ci = pl.program_id(1)
    L, width = q_ref.shape
    d = B_HEAD_DIM
    halo = SUBLANES_F32
    heads = range(B_HEADS)
    cols = [slice(hd * d, (hd + 1) * d) for hd in heads]

    @pl.when(ci == 0)
    def _():
        qbuf_ref[0:halo, :] = jnp.zeros((halo, width), F32)
        kbuf_ref[0:halo, :] = jnp.zeros((halo, width), F32)
        c_ref[...] = jnp.zeros_like(c_ref)
        m_ref[...] = jnp.zeros_like(m_ref)

    def conv_silu(x_ref, buf_ref, w_ref, b_ref):
        buf_ref[halo:halo + L, :] = x_ref[...].astype(F32)
        y = b_ref[...]
        for j in range(CONV_WIDTH):
            off = halo - (CONV_WIDTH - 1) + j
            y = y + buf_ref[off:off + L, :] * w_ref[j:j + 1, :]
        buf_ref[0:halo, :] = buf_ref[L:L + halo, :]
        return y * jax.nn.sigmoid(y)

    def log_sigmoid(z):
        return jnp.minimum(z, 0.0) - jnp.log1p(jnp.exp(-jnp.abs(z)))

    rr = lax.broadcasted_iota(jnp.int32, (L, L), 0)
    cc = lax.broadcasted_iota(jnp.int32, (L, L), 1)
    causal = rr >= cc
    tri_low = causal.astype(BF16)
    tri_up = (rr <= cc).astype(BF16)
    lane_l = lax.broadcasted_iota(jnp.int32, (L, LANES), 1)
    sub8 = lax.broadcasted_iota(jnp.int32, (8, L), 0)
    ones_col = jnp.where(lax.broadcasted_iota(jnp.int32, (L, d), 1) == 0, 1.0, 0.0).astype(BF16)
    gcol = gcol_ref[...]
    q_all = (conv_silu(q_ref, qbuf_ref, cwq_ref, cbq_ref) * (d ** -0.5)).astype(BF16)
    k_all = conv_silu(k_ref, kbuf_ref, cwk_ref, cbk_ref)
    for hd in heads:
        q = q_all[:, cols[hd]]
        k = k_all[:, cols[hd]]
        v_aug = jnp.concatenate([v_ref[:, cols[hd]], ones_col], axis=1)
        i_col = gcol[:, hd:hd + 1]
        f_col = gcol[:, hd + B_HEADS:hd + B_HEADS + 1]
        i_row = grow_ref[0, hd:hd + 1, :]
        f_row = grow_ref[0, hd + B_HEADS:hd + B_HEADS + 1, :]

        c_hi, c_lo = _split_bf16(jnp.where(lane_l == 0, log_sigmoid(f_col), 0.0), 2)
        b_col = (_dot(tri_low, c_hi) + _dot(tri_low, c_lo))[:, 0:1]
        r_hi, r_lo = _split_bf16(jnp.where(sub8 == 0, log_sigmoid(f_row), 0.0), 2)
        b_row = (_dot(r_hi, tri_up) + _dot(r_lo, tri_up))[0:1, :]

        m_prev = m_ref[hd]
        log_inter = b_col + m_prev
        dmat = jnp.where(causal, b_col - b_row + i_row, -jnp.inf)
        m_t = jnp.maximum(log_inter, jnp.max(dmat, axis=-1, keepdims=True))
        w_inter = jnp.exp(log_inter - m_t)
        sc = _dot_nt(q, k.astype(BF16)) * jnp.exp(dmat - m_t)
        c_prev = c_ref[hd]
        inter = w_inter * _dot(q, c_prev.astype(BF16))
        num = inter[:, :d] + _dot(sc.astype(BF16), v_ref[:, cols[hd]])
        den = inter[:, d:d + 1] + jnp.sum(sc, axis=-1, keepdims=True)
        hidden = num / jnp.maximum(jnp.abs(den), jnp.exp(-m_t))
        o_ref[:, cols[hd]] = (jax.nn.sigmoid(ob_ref[:, cols[hd]].astype(F32)) * hidden).astype(o_ref.dtype)

        b_last = b_col[L - 1:L, :]
        log_old = b_last + m_prev
        log_new = b_last - b_col + i_col
        m_new = jnp.maximum(log_old, jnp.max(log_new, axis=0, keepdims=True))
        a_old = jnp.exp(log_old - m_new)
        a_new = jnp.exp(log_new - m_new)
        c_ref[hd] = a_old * c_prev + _dot_tn((a_new * k).astype(BF16), v_aug)
        m_ref[hd] = m_new


def _mlstm(qkb, vb, ob, gates, conv_w, conv_b, bsz, seq):
    t, b_width = vb.shape
    d = B_HEAD_DIM
    L = min(MLSTM_KERNEL_CHUNK, seq)
    assert seq % L == 0 and d == LANES and b_width == B_HEADS * d
    nc = seq // L
    gates_row = gates.reshape(bsz, seq, 2 * B_HEADS).transpose(0, 2, 1)
    cw = conv_w.astype(F32)
    cb = conv_b.astype(F32).reshape(1, 2 * b_width)
    tok = lambda off: (lambda b, c: (b * nc + c, off))
    return pl.pallas_call(
        _mlstm_kernel,
        grid=(bsz, nc),
        in_specs=[
            pl.BlockSpec((L, b_width), tok(0)),
            pl.BlockSpec((L, b_width), tok(1)),
            pl.BlockSpec((L, b_width), tok(0)),
            pl.BlockSpec((L, b_width), tok(0)),
            pl.BlockSpec((L, 2 * B_HEADS), tok(0)),
            pl.BlockSpec((1, 2 * B_HEADS, L), lambda b, c: (b, 0, c)),
            pl.BlockSpec((CONV_WIDTH, b_width), lambda b, c: (0, 0)),
            pl.BlockSpec((CONV_WIDTH, b_width), lambda b, c: (0, 1)),
            pl.BlockSpec((1, b_width), lambda b, c: (0, 0)),
            pl.BlockSpec((1, b_width), lambda b, c: (0, 1)),
        ],
        out_specs=pl.BlockSpec((L, b_width), tok(0)),
        out_shape=jax.ShapeDtypeStruct((t, b_width), BF16),
        scratch_shapes=[
            pltpu.VMEM((L + SUBLANES_F32, b_width), F32),
            pltpu.VMEM((L + SUBLANES_F32, b_width), F32),
            pltpu.VMEM((B_HEADS, d, 2 * d), F32),
            pltpu.VMEM((B_HEADS, 1, 1), F32),
        ],
        compiler_params=_compiler_params(("parallel", "arbitrary")),
        name="mlstm",
    )(qkb, qkb, vb, ob, gates, gates_row, cw, cw, cb, cb)


def _pool_kernel(x_ref, halo_ref, g_ref, w_ref, scale_ref, o_ref, sums_ref, *, tiles_per_seq):
    i = pl.program_id(0)
    tm, d = x_ref.shape
    n_win = len(POOL_WINDOWS)
    grp = d // n_win
    base = POOL_PAD + POOL_HALO
    rows = POOL_HALO + tm
    x = x_ref[...]
    seq_tile = i % tiles_per_seq
    sums_ref[:, 0:POOL_PAD, :] = jnp.zeros((n_win, POOL_PAD, d), F32)
    halo_h = _rms_normalize(halo_ref[...], g_ref[...])
    sums_ref[0, POOL_PAD:base, :] = jnp.where(seq_tile == 0, 0.0, halo_h)
    h = _rms_normalize(x, g_ref[...])
    sums_ref[0, base:base + tm, :] = h
    t1 = (lax.broadcasted_iota(jnp.int32, (tm, 1), 0) + seq_tile * tm + 1).astype(F32)
    for k, win in enumerate(POOL_WINDOWS):
        half = win // 2
        c0 = k * grp
        if k + 1 < n_win:
            both = (sums_ref[k, POOL_PAD:POOL_PAD + rows, c0:]
                    + sums_ref[k, POOL_PAD - half:POOL_PAD - half + rows, c0:])
            sums_ref[k + 1, POOL_PAD:POOL_PAD + rows, c0:] = both
            total = both[POOL_HALO:, 0:grp]
        else:
            total = (sums_ref[k, base:base + tm, c0:c0 + grp]
                     + sums_ref[k, base - half:base - half + tm, c0:c0 + grp])
        pooled = total / jnp.minimum(t1, float(win)) - h[:, c0:c0 + grp]
        y = _dot(pooled.astype(BF16), w_ref[k])
        o_ref[:, c0:c0 + grp] = x[:, c0:c0 + grp] + y * scale_ref[:, c0:c0 + grp]


def _pool(x, gain, w_grp, scale, seq):
    t, d = x.shape
    tm = min(POOL_TOKEN_TILE, seq)
    assert seq % tm == 0 and tm % POOL_HALO == 0
    assert POOL_WINDOWS == tuple(2 ** (k + 1) for k in range(len(POOL_WINDOWS)))
    assert POOL_WINDOWS[-1] <= POOL_HALO and POOL_WINDOWS[-1] // 2 <= POOL_PAD
    n_grp, grp, _ = w_grp.shape
    halo_blocks = tm // POOL_HALO
    return pl.pallas_call(
        functools.partial(_pool_kernel, tiles_per_seq=seq // tm),
        grid=(t // tm,),
        in_specs=[
            pl.BlockSpec((tm, d), lambda i: (i, 0)),
            pl.BlockSpec((POOL_HALO, d), lambda i: (jnp.maximum(i * halo_blocks - 1, 0), 0)),
            pl.BlockSpec((1, d), lambda i: (0, 0)),
            pl.BlockSpec((n_grp, grp, grp), lambda i: (0, 0, 0)),
            pl.BlockSpec((1, d), lambda i: (0, 0)),
        ],
        out_specs=pl.BlockSpec((tm, d), lambda i: (i, 0)),
        out_shape=jax.ShapeDtypeStruct((t, d), F32),
        scratch_shapes=[pltpu.VMEM((len(POOL_WINDOWS), POOL_PAD + POOL_HALO + tm, d), F32)],
        compiler_params=_compiler_params(("parallel",)),
        name="pool",
    )(x, x, gain.reshape(1, d), w_grp.astype(BF16), scale.astype(F32).reshape(1, d))


def _mixer_heads(x, gain, w_in, g_q, g_k, conv_w, conv_b, b_i, b_f, bsz, seq):
    qat, ka, vat, qkb, vb, ob, gates = _inproj(x, gain, w_in, g_q, g_k, b_i, b_f)
    ya = _moba(qat, ka, vat, bsz, seq)
    yb = _mlstm(qkb, vb, ob, gates, conv_w, conv_b, bsz, seq)
    return ya, yb


def kernel(x, norm_g, ffn_w_gate, ffn_w_up, ffn_w_down, ab_w_in, ab_w_out, ab_g_q, ab_g_k,
           ab_conv_w, ab_conv_b, ab_b_i, ab_b_f, pool_w, pool_scale):
    bsz, seq, d = x.shape
    depth = norm_g.shape[0]
    y = x.reshape(bsz * seq, d)
    for layer in range(depth):
        y = _ffn(y, norm_g[layer, 0], ffn_w_gate[layer, 0], ffn_w_up[layer, 0], ffn_w_down[layer, 0])
        mixer_out = None
        if layer % 2 == 0:
            e = layer // 2
            ya, yb = _mixer_heads(y, norm_g[layer, 1], ab_w_in[e], ab_g_q[e], ab_g_k[e],
                                  ab_conv_w[e], ab_conv_b[e], ab_b_i[e], ab_b_f[e], bsz, seq)
            mixer_out = (ya, yb, ab_w_out[e])
        else:
            o = layer // 2
            y = _pool(y, norm_g[layer, 1], pool_w[o], pool_scale[o], seq)
        y = _ffn(y, norm_g[layer, 2], ffn_w_gate[layer, 1], ffn_w_up[layer, 1], ffn_w_down[layer, 1],
                 mixer_out=mixer_out)
    return y.reshape(bsz, seq, d)
```

```python
import functools

import jax
import jax.numpy as jnp
from jax import lax
from jax.experimental import pallas as pl
from jax.experimental.pallas import tpu as pltpu

F32 = jnp.float32
BF16 = jnp.bfloat16

LANES = 128
SUBLANES_F32 = 8
SUBLANES_BF16 = 16
MXU_WIDTH = 256
VMEM_BYTES = 64 * 1024 * 1024
VMEM_LIMIT_BYTES = VMEM_BYTES * 7 // 8

RMS_EPS = 1e-6
A_HEADS = 8
A_HEAD_DIM = 64
MOBA_BLOCK = 256
MOBA_TOPK = 3
B_HEADS = 4
B_HEAD_DIM = 128
CONV_WIDTH = 4
POOL_WINDOWS = (2, 4, 8, 16)

MOBA_KEY_GROUP = 2
MOBA_HEADS_PER_STEP = 4
MLSTM_KERNEL_CHUNK = 256
POOL_HALO = 16
POOL_PAD = SUBLANES_F32
MASK_VALUE = -1e30
LOG2E = 1.4426950408889634
V_AUG_ROWS = A_HEAD_DIM + SUBLANES_BF16
FFN_TOKEN_TILE = 1024
FFN_SUB_TILE = 512
FFN_HIDDEN_TILE = MXU_WIDTH
PROJ_TOKEN_TILE = 512
POOL_TOKEN_TILE = 1024


def _compiler_params(semantics):
    return pltpu.CompilerParams(dimension_semantics=semantics,
                                vmem_limit_bytes=VMEM_LIMIT_BYTES)


def _rms_normalize(x, gain):
    ms = jnp.mean(x * x, axis=-1, keepdims=True)
    return x * lax.rsqrt(ms + RMS_EPS) * gain


def _dot(a, b):
    return jnp.dot(a, b, preferred_element_type=F32)


def _dot_nt(a, b):
    return lax.dot_general(a, b, (((1,), (1,)), ((), ())), preferred_element_type=F32)


def _dot_tn(a, b):
    return lax.dot_general(a, b, (((0,), (0,)), ((), ())), preferred_element_type=F32)


def _split_bf16(x, parts):
    out = []
    rem = x
    for _ in range(parts):
        p = rem.astype(BF16)
        out.append(p)
        rem = rem - p.astype(F32)
    return out


def _ffn_kernel(x_ref, g_ref, wg_ref, wu_ref, wd_ref, *rest, hidden_tile, sub_tile):
    o_ref = rest[-1]
    d_ff = wg_ref.shape[1]
    for r0 in range(0, x_ref.shape[0], sub_tile):
        x = x_ref[r0:r0 + sub_tile, :]
        if len(rest) > 1:
            ya_ref, yb_ref, wa_ref, wb_ref = rest[:-1]
            x = (x + _dot(ya_ref[r0:r0 + sub_tile, :], wa_ref[...])
                 + _dot(yb_ref[r0:r0 + sub_tile, :], wb_ref[...]))
        h = _rms_normalize(x, g_ref[...]).astype(BF16)
        acc = None
        for c0 in range(0, d_ff, hidden_tile):
            gate = _dot(h, wg_ref[:, c0:c0 + hidden_tile])
            up = _dot(h, wu_ref[:, c0:c0 + hidden_tile])
            act = (gate * jax.nn.sigmoid(gate) * up).astype(BF16)
            part = _dot(act, wd_ref[c0:c0 + hidden_tile, :])
            acc = part if acc is None else acc + part
        o_ref[r0:r0 + sub_tile, :] = x + 0.5 * acc


def _resident(shape):
    return pl.BlockSpec(shape, lambda *_: (0,) * len(shape), pipeline_mode=pl.Buffered(1))


def _ffn(x, gain, w_gate, w_up, w_down, which, mixer_out=None):
    t, d = x.shape
    d_ff = w_gate.shape[-1]
    tm, tf = FFN_TOKEN_TILE, FFN_HIDDEN_TILE
    assert t % tm == 0 and d_ff % tf == 0 and tm % FFN_SUB_TILE == 0
    row = lambda i: (i, 0)
    picked = lambda r, c: pl.BlockSpec((None, None, r, c), lambda i: (*which, 0, 0),
                                       pipeline_mode=pl.Buffered(1))
    in_specs = [pl.BlockSpec((tm, d), row), _resident((1, d)), picked(d, d_ff), picked(d, d_ff),
                picked(d_ff, d)]
    args = [x, gain.reshape(1, d), w_gate, w_up, w_down]
    if mixer_out is not None:
        ya, yb, w_out = mixer_out
        wa, wb = ya.shape[1], yb.shape[1]
        in_specs += [pl.BlockSpec((tm, wa), row), pl.BlockSpec((tm, wb), row),
                     _resident((wa, d)), _resident((wb, d))]
        args += [ya, yb, w_out[:wa].astype(BF16), w_out[wa:].astype(BF16)]
    return pl.pallas_call(
        functools.partial(_ffn_kernel, hidden_tile=tf, sub_tile=FFN_SUB_TILE),
        grid=(t // tm,),
        in_specs=in_specs,
        out_specs=pl.BlockSpec((tm, d), row),
        out_shape=jax.ShapeDtypeStruct((t, d), F32),
        compiler_params=_compiler_params(("parallel",)),
        name="ffn",
    )(*args)


def _inproj_kernel(x_ref, g_ref, w_ref, wqt_ref, wvt_ref, wgate_ref, gqcol_ref, gk_ref, grp_ref, gbias_ref,
                   qat_ref, ka_ref, vat_ref, qkb_ref, vb_ref, ob_ref, gates_ref, *, a_width, b_width):
    h = _rms_normalize(x_ref[...], g_ref[...]).astype(BF16)

    qt = _dot_nt(wqt_ref[...], h)
    for hd in range(a_width // A_HEAD_DIM):
        rows = slice(hd * A_HEAD_DIM, (hd + 1) * A_HEAD_DIM)
        y = qt[rows, :]
        ms = jnp.mean(y * y, axis=0, keepdims=True)
        gain = gqcol_ref[rows, :] * (A_HEAD_DIM ** -0.5 * LOG2E)
        qat_ref[rows, :] = (y * lax.rsqrt(ms + RMS_EPS) * gain).astype(BF16)

    y = _dot(h, w_ref[:, 0:a_width])
    sq_hi, sq_lo = _split_bf16(y * y, 2)
    ssq = _dot(sq_hi, grp_ref[...]) + _dot(sq_lo, grp_ref[...])
    ka_ref[...] = (y * lax.rsqrt(ssq * (1.0 / A_HEAD_DIM) + RMS_EPS) * gk_ref[...]).astype(BF16)

    vat_ref[...] = _dot_nt(wvt_ref[...], h).astype(BF16)
    c0 = a_width
    qkb_ref[...] = _dot(h, w_ref[:, c0:c0 + 2 * b_width]).astype(BF16)
    c0 += 2 * b_width
    vb_ref[...] = _dot(h, w_ref[:, c0:c0 + b_width]).astype(BF16)
    c0 += b_width
    ob_ref[...] = _dot(h, w_ref[:, c0:c0 + b_width]).astype(BF16)
    gates = _dot(h, wgate_ref[...]) + gbias_ref[...]
    gates_ref[...] = gates[:, :2 * B_HEADS]


def _inproj(x, gain, w_in, g_q, g_k, b_i, b_f):
    t, d = x.shape
    a_width = A_HEADS * A_HEAD_DIM
    b_width = B_HEADS * B_HEAD_DIM
    n_main = a_width + 4 * b_width
    o_ib = 3 * a_width + 3 * b_width
    tm = PROJ_TOKEN_TILE
    assert t % tm == 0
    w_main = jnp.concatenate([w_in[:, a_width:2 * a_width], w_in[:, 3 * a_width:o_ib],
                              w_in[:, o_ib + 2 * B_HEADS:]], axis=1).astype(BF16)
    w_qt = w_in[:, :a_width].T.astype(BF16)
    w_vt = w_in[:, 2 * a_width:3 * a_width].T.astype(BF16)
    w_gate = jnp.pad(w_in[:, o_ib:o_ib + 2 * B_HEADS], ((0, 0), (0, LANES - 2 * B_HEADS))).astype(BF16)
    gate_bias = jnp.pad(jnp.concatenate([b_i, b_f]).astype(F32), (0, LANES - 2 * B_HEADS)).reshape(1, LANES)
    head_id = jnp.arange(a_width) // A_HEAD_DIM
    grp = (head_id[:, None] == head_id[None, :]).astype(BF16)
    gq_col = jnp.tile(g_q.astype(F32), A_HEADS).reshape(a_width, 1)
    gk = jnp.tile(g_k.astype(F32), A_HEADS).reshape(1, a_width)
    const = lambda i: (0, 0)
    row = lambda i: (i, 0)
    col = lambda i: (0, i)
    outs = pl.pallas_call(
        functools.partial(_inproj_kernel, a_width=a_width, b_width=b_width),
        grid=(t // tm,),
        in_specs=[
            pl.BlockSpec((tm, d), row),
            pl.BlockSpec((1, d), const),
            pl.BlockSpec((d, n_main), const),
            pl.BlockSpec((a_width, d), const),
            pl.BlockSpec((a_width, d), const),
            pl.BlockSpec((d, LANES), const),
            pl.BlockSpec((a_width, 1), const),
            pl.BlockSpec((1, a_width), const),
            pl.BlockSpec((a_width, a_width), const),
            pl.BlockSpec((1, LANES), const),
        ],
        out_specs=[
            pl.BlockSpec((a_width, tm), col),
            pl.BlockSpec((tm, a_width), row),
            pl.BlockSpec((a_width, tm), col),
            pl.BlockSpec((tm, 2 * b_width), row),
            pl.BlockSpec((tm, b_width), row),
            pl.BlockSpec((tm, b_width), row),
            pl.BlockSpec((tm, 2 * B_HEADS), row),
        ],
        out_shape=[
            jax.ShapeDtypeStruct((a_width, t), BF16),
            jax.ShapeDtypeStruct((t, a_width), BF16),
            jax.ShapeDtypeStruct((a_width, t), BF16),
            jax.ShapeDtypeStruct((t, 2 * b_width), BF16),
            jax.ShapeDtypeStruct((t, b_width), BF16),
            jax.ShapeDtypeStruct((t, b_width), BF16),
            jax.ShapeDtypeStruct((t, 2 * B_HEADS), F32),
        ],
        compiler_params=_compiler_params(("parallel",)),
        name="inproj",
    )(x, gain.reshape(1, d), w_main, w_qt, w_vt, w_gate, gq_col, gk, grp, gate_bias)
    return outs


def _moba_kernel(slope_ref, qt_ref, k_ref, vt_ref, o_ref, kaug_ref, vaug_ref, kmean_ref, qaug_ref, acc_ref,
                 sa_ref, sb_ref, *, nb, grp, heads, tiles):
    hg = pl.program_id(1)
    c0 = pl.program_id(2) * tiles
    blk = MOBA_BLOCK
    dh = A_HEAD_DIM
    nbp = pl.cdiv(nb, SUBLANES_BF16) * SUBLANES_BF16
    streams = [(ti, h) for ti in range(tiles) for h in range(heads)]

    @pl.when(c0 == 0)
    def _build_key_value_side():
        ones_rows = jnp.where(lax.broadcasted_iota(jnp.int32, (V_AUG_ROWS - dh, blk), 0) == 0,
                              1.0, 0.0).astype(BF16)
        for h in range(heads):
            for j in range(nb):
                g, off = divmod(j, grp)
                vaug_ref[h, g, 0:dh, off * blk:(off + 1) * blk] = vt_ref[dh * h:dh * (h + 1),
                                                                         j * blk:(j + 1) * blk]
                vaug_ref[h, g, dh:V_AUG_ROWS, off * blk:(off + 1) * blk] = ones_rows
        kmean_ref[...] = jnp.zeros_like(kmean_ref)
        lane = lax.broadcasted_iota(jnp.int32, (blk, LANES), 1)
        row = lax.broadcasted_iota(jnp.int32, (blk, LANES), 0)

        def body(j, carry):
            start = pl.multiple_of(j * blk, blk)
            pos = (row + j * blk).astype(F32)
            for pr in range(heads // 2):
                kb = k_ref[pl.ds(start, blk), pr * LANES:(pr + 1) * LANES].astype(F32)
                kmean_ref[pr, pl.ds(j, 1), :] = jnp.mean(kb, axis=0, keepdims=True)
                for hh in range(2):
                    h = 2 * pr + hh
                    slope = slope_ref[pl.ds(heads * hg + h, 1), :][:, 0:LANES] * LOG2E
                    p1, p2, p3 = _split_bf16(slope * pos, 3)
                    rel = lane - dh * (1 - hh)
                    aug = jnp.where(rel == j, 1.0, 0.0)
                    aug = jnp.where(rel == nb, p1.astype(F32), aug)
                    aug = jnp.where(rel == nb + 1, p2.astype(F32), aug)
                    aug = jnp.where(rel == nb + 2, p3.astype(F32), aug)
                    aug = jnp.where((rel >= nb + 3) & (rel < nb + 6), 1.0, aug)
                    is_data = (lane >= dh * hh) & (lane < dh * (hh + 1))
                    kaug_ref[h, pl.ds(start, blk), :] = jnp.where(is_data, kb, aug).astype(BF16)
            return carry

        lax.fori_loop(0, nb, body, 0)

    blk_ix = lax.broadcasted_iota(jnp.int32, (nbp, blk), 0)
    blk_f = blk_ix.astype(F32)
    aug_row = lax.broadcasted_iota(jnp.int32, (dh, blk), 0)
    qry_lane = lax.broadcasted_iota(jnp.int32, (dh, blk), 1)
    lane_k = lax.broadcasted_iota(jnp.int32, (nbp, LANES), 1)
    own_grp = c0 // grp
    key_ix = lax.broadcasted_iota(jnp.int32, (grp * blk, blk), 0) + own_grp * (grp * blk)
    qry_ix = lax.broadcasted_iota(jnp.int32, (grp * blk, blk), 1)
    gates = []
    for ti, h in streams:
        pr, hh = divmod(h, 2)
        qt_pair = qt_ref[pr * LANES:(pr + 1) * LANES, ti * blk:(ti + 1) * blk]
        is_data_k = (lane_k >= dh * hh) & (lane_k < dh * (hh + 1))
        km_hi, km_lo = _split_bf16(jnp.where(is_data_k, kmean_ref[pr, 0:nbp, :], 0.0), 2)
        gates.append(_dot(km_hi, qt_pair) + _dot(km_lo, qt_pair))
    for si, (ti, h) in enumerate(streams):
        pr, hh = divmod(h, 2)
        c = c0 + ti
        valid = blk_ix < c
        tq = (qry_lane + c * blk).astype(F32)
        gate = jnp.where(valid, gates[si], -jnp.inf)
        chosen = blk_ix == c
        for _ in range(MOBA_TOPK):
            best = jnp.max(gate, axis=0, keepdims=True)
            first = jnp.min(jnp.where(gate == best, blk_f, float(nbp)), axis=0, keepdims=True)
            pick = blk_f == first
            chosen = chosen | (pick & valid)
            gate = jnp.where(pick, -jnp.inf, gate)
        bias = jnp.where(chosen, 0.0, MASK_VALUE)
        if nbp < dh:
            bias = jnp.concatenate([bias, jnp.zeros((dh - nbp, blk), F32)], axis=0)
        slope = slope_ref[pl.ds(heads * hg + h, 1), :] * LOG2E
        t1, t2, t3 = _split_bf16(-slope * tq, 3)
        aug = jnp.where(aug_row < nb, bias, 0.0)
        aug = jnp.where((aug_row >= nb) & (aug_row < nb + 3), 1.0, aug)
        aug = jnp.where(aug_row == nb + 3, t1.astype(F32), aug)
        aug = jnp.where(aug_row == nb + 4, t2.astype(F32), aug)
        aug = jnp.where(aug_row == nb + 5, t3.astype(F32), aug).astype(BF16)
        data = qt_ref[dh * h:dh * (h + 1), ti * blk:(ti + 1) * blk]
        qaug_ref[si] = jnp.concatenate([data, aug] if hh == 0 else [aug, data], axis=0)

    def scores_into(buf_ref, g):
        start = pl.multiple_of(g * (grp * blk), grp * blk)
        col_max = []
        for si, (ti, h) in enumerate(streams):
            s = _dot(kaug_ref[h, pl.ds(start, grp * blk), :], qaug_ref[si])
            buf_ref[si] = s
            col_max.append(jnp.max(s, axis=0, keepdims=True))
        return tuple(col_max)

    def consume(buf_ref, g, ms, col_max, own=False):
        new_ms = []
        for si, (ti, h) in enumerate(streams):
            s = buf_ref[si]
            if own:
                s = jnp.where(key_ix <= qry_ix + (c0 + ti) * blk, s, MASK_VALUE)
                group_max = jnp.max(s, axis=0, keepdims=True)
            else:
                group_max = col_max[si]
            m_new = jnp.maximum(ms[si], group_max)
            alpha = jnp.exp2(ms[si] - m_new)
            p = jnp.exp2(s - m_new)
            acc_ref[si] = alpha * acc_ref[si] + _dot(vaug_ref[h, g], p.astype(BF16))
            new_ms.append(m_new)
        return tuple(new_ms)

    def write_output():
        for ti in range(tiles):
            outs = []
            for h in range(heads):
                acc = acc_ref[ti * heads + h]
                outs.append(acc[0:dh, :] / acc[dh:dh + 1, :])
            o_ref[ti * blk:(ti + 1) * blk, :] = jnp.concatenate(outs, axis=0).T.astype(o_ref.dtype)

    acc_ref[...] = jnp.zeros_like(acc_ref)
    masked_score = jnp.full((1, blk), MASK_VALUE, F32).astype(BF16).astype(F32)
    max_a0 = scores_into(sa_ref, 0)

    def two_groups(i, carry):
        ms, max_a = carry
        max_b = scores_into(sb_ref, 2 * i + 1)
        ms = consume(sa_ref, 2 * i, ms, max_a)
        max_a = scores_into(sa_ref, 2 * i + 2)
        return consume(sb_ref, 2 * i + 1, ms, max_b), max_a

    ms, max_a = lax.fori_loop(0, own_grp // 2, two_groups, ((masked_score,) * len(streams), max_a0))

    @pl.when(own_grp % 2 == 0)
    def _own_group_in_a():
        consume(sa_ref, own_grp, ms, None, own=True)
        write_output()

    @pl.when(own_grp % 2 == 1)
    def _own_group_in_b():
        scores_into(sb_ref, own_grp)
        consume(sb_ref, own_grp, consume(sa_ref, own_grp - 1, ms, max_a), None, own=True)
        write_output()


def _moba(qat, ka, vat, bsz, seq):
    t, a_width = ka.shape
    blk = MOBA_BLOCK
    assert seq % blk == 0 and seq // blk >= MOBA_TOPK
    nb = seq // blk
    assert nb + 6 <= A_HEAD_DIM
    grp = MOBA_KEY_GROUP if nb % MOBA_KEY_GROUP == 0 else 1
    heads = MOBA_HEADS_PER_STEP
    width = heads * A_HEAD_DIM
    assert heads % 2 == 0 and a_width % width == 0
    tiles = grp
    n_streams = tiles * heads
    steps = nb // tiles
    slopes = jnp.exp2(-8.0 * jnp.arange(1, A_HEADS + 1, dtype=F32) / A_HEADS)
    slope_tbl = jnp.broadcast_to(slopes[:, None], (A_HEADS, blk))
    return pl.pallas_call(
        functools.partial(_moba_kernel, nb=nb, grp=grp, heads=heads, tiles=tiles),
        grid=(bsz, a_width // width, steps),
        in_specs=[
            pl.BlockSpec((A_HEADS, blk), lambda b, hg, c: (0, 0)),
            pl.BlockSpec((width, tiles * blk), lambda b, hg, c: (hg, b * steps + c)),
            pl.BlockSpec((seq, width), lambda b, hg, c: (b, hg), pipeline_mode=pl.Buffered(1)),
            pl.BlockSpec((width, seq), lambda b, hg, c: (hg, b), pipeline_mode=pl.Buffered(1)),
        ],
        out_specs=pl.BlockSpec((tiles * blk, width), lambda b, hg, c: (b * steps + c, hg)),
        out_shape=jax.ShapeDtypeStruct((t, a_width), BF16),
        scratch_shapes=[
            pltpu.VMEM((heads, seq, LANES), BF16),
            pltpu.VMEM((heads, nb // grp, V_AUG_ROWS, grp * blk), BF16),
            pltpu.VMEM((heads // 2, LANES, LANES), F32),
            pltpu.VMEM((n_streams, LANES, blk), BF16),
            pltpu.VMEM((n_streams, V_AUG_ROWS, blk), F32),
            pltpu.VMEM((n_streams, grp * blk, blk), F32),
            pltpu.VMEM((n_streams, grp * blk, blk), F32),
        ],
        compiler_params=_compiler_params(("parallel", "parallel", "arbitrary")),
        name="moba",
    )(slope_tbl, qat, ka, vat)


def _mlstm_kernel(q_ref, k_ref, v_ref, ob_ref, gcol_ref, grow_ref, cwq_ref, cwk_ref, cbq_ref, cbk_ref,
                  o_ref, qbuf_ref, kbuf_ref, c_ref, m_ref):
    ci = pl.program_id(1)
    L, width = q_ref.shape
    d = B_HEAD_DIM
    halo = SUBLANES_F32
    heads = range(B_HEADS)
    cols = [slice(hd * d, (hd + 1) * d) for hd in heads]

    @pl.when(ci == 0)
    def _():
        qbuf_ref[0:halo, :] = jnp.zeros((halo, width), F32)
        kbuf_ref[0:halo, :] = jnp.zeros((halo, width), F32)
        c_ref[...] = jnp.zeros_like(c_ref)
        m_ref[...] = jnp.zeros_like(m_ref)

    def conv_silu(x_ref, buf_ref, w_ref, b_ref):
        buf_ref[halo:halo + L, :] = x_ref[...].astype(F32)
        y = b_ref[...]
        for j in range(CONV_WIDTH):
            off = halo - (CONV_WIDTH - 1) + j
            y = y + buf_ref[off:off + L, :] * w_ref[j:j + 1, :]
        buf_ref[0:halo, :] = buf_ref[L:L + halo, :]
        return y * jax.nn.sigmoid(y)

    def log_sigmoid(z):
        return jnp.minimum(z, 0.0) - jnp.log1p(jnp.exp(-jnp.abs(z)))

    rr = lax.broadcasted_iota(jnp.int32, (L, L), 0)
    cc = lax.broadcasted_iota(jnp.int32, (L, L), 1)
    causal = rr >= cc
    tri_low = causal.astype(BF16)
    tri_up = (rr <= cc).astype(BF16)
    lane_l = lax.broadcasted_iota(jnp.int32, (L, LANES), 1)
    sub8 = lax.broadcasted_iota(jnp.int32, (8, L), 0)
    ones_col = jnp.where(lax.broadcasted_iota(jnp.int32, (L, d), 1) == 0, 1.0, 0.0).astype(BF16)
    gcol = gcol_ref[...]
    q_all = (conv_silu(q_ref, qbuf_ref, cwq_ref, cbq_ref) * (d ** -0.5)).astype(BF16)
    k_all = conv_silu(k_ref, kbuf_ref, cwk_ref, cbk_ref)
    for hd in heads:
        q = q_all[:, cols[hd]]
        k = k_all[:, cols[hd]]
        v_aug = jnp.concatenate([v_ref[:, cols[hd]], ones_col], axis=1)
        i_col = gcol[:, hd:hd + 1]
        f_col = gcol[:, hd + B_HEADS:hd + B_HEADS + 1]
        i_row = grow_ref[0, hd:hd + 1, :]
        f_row = grow_ref[0, hd + B_HEADS:hd + B_HEADS + 1, :]

        c_hi, c_lo = _split_bf16(jnp.where(lane_l == 0, log_sigmoid(f_col), 0.0), 2)
        b_col = (_dot(tri_low, c_hi) + _dot(tri_low, c_lo))[:, 0:1]
        r_hi, r_lo = _split_bf16(jnp.where(sub8 == 0, log_sigmoid(f_row), 0.0), 2)
        b_row = (_dot(r_hi, tri_up) + _dot(r_lo, tri_up))[0:1, :]

        m_prev = m_ref[hd]
        log_inter = b_col + m_prev
        dmat = jnp.where(causal, b_col - b_row + i_row, -jnp.inf)
        m_t = jnp.maximum(log_inter, jnp.max(dmat, axis=-1, keepdims=True))
        w_inter = jnp.exp(log_inter - m_t)
        sc = _dot_nt(q, k.astype(BF16)) * jnp.exp(dmat - m_t)
        c_prev = c_ref[hd]
        inter = w_inter * _dot(q, c_prev.astype(BF16))
        num = inter[:, :d] + _dot(sc.astype(BF16), v_ref[:, cols[hd]])
        den = inter[:, d:d + 1] + jnp.sum(sc, axis=-1, keepdims=True)
        hidden = num / jnp.maximum(jnp.abs(den), jnp.exp(-m_t))
        o_ref[:, cols[hd]] = (jax.nn.sigmoid(ob_ref[:, cols[hd]].astype(F32)) * hidden).astype(o_ref.dtype)

        b_last = b_col[L - 1:L, :]
        log_old = b_last + m_prev
        log_new = b_last - b_col + i_col
        m_new = jnp.maximum(log_old, jnp.max(log_new, axis=0, keepdims=True))
        a_old = jnp.exp(log_old - m_new)
        a_new = jnp.exp(log_new - m_new)
        c_ref[hd] = a_old * c_prev + _dot_tn((a_new * k).astype(BF16), v_aug)
        m_ref[hd] = m_new


def _mlstm(qkb, vb, ob, gates, conv_w, conv_b, bsz, seq):
    t, b_width = vb.shape
    d = B_HEAD_DIM
    L = min(MLSTM_KERNEL_CHUNK, seq)
    assert seq % L == 0 and d == LANES and b_width == B_HEADS * d
    nc = seq // L
    gates_row = gates.reshape(bsz, seq, 2 * B_HEADS).transpose(0, 2, 1)
    cw = conv_w.astype(F32)
    cb = conv_b.astype(F32).reshape(1, 2 * b_width)
    tok = lambda off: (lambda b, c: (b * nc + c, off))
    return pl.pallas_call(
        _mlstm_kernel,
        grid=(bsz, nc),
        in_specs=[
            pl.BlockSpec((L, b_width), tok(0)),
            pl.BlockSpec((L, b_width), tok(1)),
            pl.BlockSpec((L, b_width), tok(0)),
            pl.BlockSpec((L, b_width), tok(0)),
            pl.BlockSpec((L, 2 * B_HEADS), tok(0)),
            pl.BlockSpec((1, 2 * B_HEADS, L), lambda b, c: (b, 0, c)),
            pl.BlockSpec((CONV_WIDTH, b_width), lambda b, c: (0, 0)),
            pl.BlockSpec((CONV_WIDTH, b_width), lambda b, c: (0, 1)),
            pl.BlockSpec((1, b_width), lambda b, c: (0, 0)),
            pl.BlockSpec((1, b_width), lambda b, c: (0, 1)),
        ],
        out_specs=pl.BlockSpec((L, b_width), tok(0)),
        out_shape=jax.ShapeDtypeStruct((t, b_width), BF16),
        scratch_shapes=[
            pltpu.VMEM((L + SUBLANES_F32, b_width), F32),
            pltpu.VMEM((L + SUBLANES_F32, b_width), F32),
            pltpu.VMEM((B_HEADS, d, 2 * d), F32),
            pltpu.VMEM((B_HEADS, 1, 1), F32),
        ],
        compiler_params=_compiler_params(("parallel", "arbitrary")),
        name="mlstm",
    )(qkb, qkb, vb, ob, gates, gates_row, cw, cw, cb, cb)


def _pool_kernel(x_ref, halo_ref, g_ref, w_ref, scale_ref, o_ref, sums_ref, *, tiles_per_seq):
    i = pl.program_id(0)
    tm, d = x_ref.shape
    n_win = len(POOL_WINDOWS)
    grp = d // n_win
    base = POOL_PAD + POOL_HALO
    rows = POOL_HALO + tm
    x = x_ref[...]
    seq_tile = i % tiles_per_seq
    sums_ref[:, 0:POOL_PAD, :] = jnp.zeros((n_win, POOL_PAD, d), F32)
    halo_h = _rms_normalize(halo_ref[...], g_ref[...])
    sums_ref[0, POOL_PAD:base, :] = jnp.where(seq_tile == 0, 0.0, halo_h)
    h = _rms_normalize(x, g_ref[...])
    sums_ref[0, base:base + tm, :] = h
    t1 = (lax.broadcasted_iota(jnp.int32, (tm, 1), 0) + seq_tile * tm + 1).astype(F32)
    for k, win in enumerate(POOL_WINDOWS):
        half = win // 2
        c0 = k * grp
        if k + 1 < n_win:
            both = (sums_ref[k, POOL_PAD:POOL_PAD + rows, c0:]
                    + sums_ref[k, POOL_PAD - half:POOL_PAD - half + rows, c0:])
            sums_ref[k + 1, POOL_PAD:POOL_PAD + rows, c0:] = both
            total = both[POOL_HALO:, 0:grp]
        else:
            total = (sums_ref[k, base:base + tm, c0:c0 + grp]
                     + sums_ref[k, base - half:base - half + tm, c0:c0 + grp])
        pooled = total / jnp.minimum(t1, float(win)) - h[:, c0:c0 + grp]
        y = _dot(pooled.astype(BF16), w_ref[k])
        o_ref[:, c0:c0 + grp] = x[:, c0:c0 + grp] + y * scale_ref[:, c0:c0 + grp]


def _pool(x, gain, w_grp, scale, seq):
    t, d = x.shape
    tm = min(POOL_TOKEN_TILE, seq)
    assert seq % tm == 0 and tm % POOL_HALO == 0
    assert POOL_WINDOWS == tuple(2 ** (k + 1) for k in range(len(POOL_WINDOWS)))
    assert POOL_WINDOWS[-1] <= POOL_HALO and POOL_WINDOWS[-1] // 2 <= POOL_PAD
    n_grp, grp, _ = w_grp.shape
    halo_blocks = tm // POOL_HALO
    return pl.pallas_call(
        functools.partial(_pool_kernel, tiles_per_seq=seq // tm),
        grid=(t // tm,),
        in_specs=[
            pl.BlockSpec((tm, d), lambda i: (i, 0)),
            pl.BlockSpec((POOL_HALO, d), lambda i: (jnp.maximum(i * halo_blocks - 1, 0), 0)),
            pl.BlockSpec((1, d), lambda i: (0, 0)),
            pl.BlockSpec((n_grp, grp, grp), lambda i: (0, 0, 0)),
            pl.BlockSpec((1, d), lambda i: (0, 0)),
        ],
        out_specs=pl.BlockSpec((tm, d), lambda i: (i, 0)),
        out_shape=jax.ShapeDtypeStruct((t, d), F32),
        scratch_shapes=[pltpu.VMEM((len(POOL_WINDOWS), POOL_PAD + POOL_HALO + tm, d), F32)],
        compiler_params=_compiler_params(("parallel",)),
        name="pool",
    )(x, x, gain.reshape(1, d), w_grp.astype(BF16), scale.astype(F32).reshape(1, d))


def _mixer_heads(x, gain, w_in, g_q, g_k, conv_w, conv_b, b_i, b_f, bsz, seq):
    qat, ka, vat, qkb, vb, ob, gates = _inproj(x, gain, w_in, g_q, g_k, b_i, b_f)
    ya = _moba(qat, ka, vat, bsz, seq)
    yb = _mlstm(qkb, vb, ob, gates, conv_w, conv_b, bsz, seq)
    return ya, yb


def kernel(x, norm_g, ffn_w_gate, ffn_w_up, ffn_w_down, ab_w_in, ab_w_out, ab_g_q, ab_g_k,
           ab_conv_w, ab_conv_b, ab_b_i, ab_b_f, pool_w, pool_scale):
    bsz, seq, d = x.shape
    depth = norm_g.shape[0]
    y = x.reshape(bsz * seq, d)
    w_gate, w_up, w_down = (w.astype(BF16) for w in (ffn_w_gate, ffn_w_up, ffn_w_down))
    for layer in range(depth):
        y = _ffn(y, norm_g[layer, 0], w_gate, w_up, w_down, (layer, 0))
        mixer_out = None
        if layer % 2 == 0:
            e = layer // 2
            ya, yb = _mixer_heads(y, norm_g[layer, 1], ab_w_in[e], ab_g_q[e], ab_g_k[e],
                                  ab_conv_w[e], ab_conv_b[e], ab_b_i[e], ab_b_f[e], bsz, seq)
            mixer_out = (ya, yb, ab_w_out[e])
        else:
            o = layer // 2
            y = _pool(y, norm_g[layer, 1], pool_w[o], pool_scale[o], seq)
        y = _ffn(y, norm_g[layer, 2], w_gate, w_up, w_down, (layer, 1), mixer_out=mixer_out)
    return y.reshape(bsz, seq, d)
```

```python
import functools

import jax
import jax.numpy as jnp
from jax import lax
from jax.experimental import pallas as pl
from jax.experimental.pallas import tpu as pltpu

F32 = jnp.float32
BF16 = jnp.bfloat16

LANES = 128
SUBLANES_F32 = 8
SUBLANES_BF16 = 16
MXU_WIDTH = 256
VMEM_BYTES = 64 * 1024 * 1024
VMEM_LIMIT_BYTES = VMEM_BYTES * 7 // 8

RMS_EPS = 1e-6
A_HEADS = 8
A_HEAD_DIM = 64
MOBA_BLOCK = 256
MOBA_TOPK = 3
B_HEADS = 4
B_HEAD_DIM = 128
CONV_WIDTH = 4
POOL_WINDOWS = (2, 4, 8, 16)

MOBA_KEY_GROUP = 2
MOBA_HEADS_PER_STEP = 4
MLSTM_KERNEL_CHUNK = 256
POOL_HALO = 16
POOL_PAD = SUBLANES_F32
MASK_VALUE = -1e30
LOG2E = 1.4426950408889634
V_AUG_ROWS = A_HEAD_DIM + SUBLANES_BF16
FFN_TOKEN_TILE = 1024
FFN_SUB_TILE = 512
FFN_HIDDEN_TILE = MXU_WIDTH
PROJ_TOKEN_TILE = 1024
POOL_TOKEN_TILE = 1024


def _compiler_params(semantics):
    return pltpu.CompilerParams(dimension_semantics=semantics,
                                vmem_limit_bytes=VMEM_LIMIT_BYTES)


def _rms_normalize(x, gain):
    ms = jnp.mean(x * x, axis=-1, keepdims=True)
    return x * lax.rsqrt(ms + RMS_EPS) * gain


def _dot(a, b):
    return jnp.dot(a, b, preferred_element_type=F32)


def _dot_nt(a, b):
    return lax.dot_general(a, b, (((1,), (1,)), ((), ())), preferred_element_type=F32)


def _dot_tn(a, b):
    return lax.dot_general(a, b, (((0,), (0,)), ((), ())), preferred_element_type=F32)


def _split_bf16(x, parts):
    out = []
    rem = x
    for _ in range(parts):
        p = rem.astype(BF16)
        out.append(p)
        rem = rem - p.astype(F32)
    return out


def _ffn_kernel(x_ref, g_ref, wg_ref, wu_ref, wd_ref, *rest, hidden_tile, sub_tile):
    o_ref = rest[-1]
    d_ff = wg_ref.shape[1]
    for r0 in range(0, x_ref.shape[0], sub_tile):
        x = x_ref[r0:r0 + sub_tile, :]
        if len(rest) > 1:
            ya_ref, yb_ref, wa_ref, wb_ref = rest[:-1]
            x = (x + _dot(ya_ref[r0:r0 + sub_tile, :], wa_ref[...])
                 + _dot(yb_ref[r0:r0 + sub_tile, :], wb_ref[...]))
        h = _rms_normalize(x, g_ref[...]).astype(BF16)
        acc = None
        for c0 in range(0, d_ff, hidden_tile):
            gate = _dot(h, wg_ref[:, c0:c0 + hidden_tile])
            up = _dot(h, wu_ref[:, c0:c0 + hidden_tile])
            act = (gate * jax.nn.sigmoid(gate) * up).astype(BF16)
            part = _dot(act, wd_ref[c0:c0 + hidden_tile, :])
            acc = part if acc is None else acc + part
        o_ref[r0:r0 + sub_tile, :] = x + 0.5 * acc


def _resident(shape):
    return pl.BlockSpec(shape, lambda *_: (0,) * len(shape), pipeline_mode=pl.Buffered(1))


def _ffn(x, gain, w_gate, w_up, w_down, which, mixer_out=None):
    t, d = x.shape
    d_ff = w_gate.shape[-1]
    tm, tf = FFN_TOKEN_TILE, FFN_HIDDEN_TILE
    assert t % tm == 0 and d_ff % tf == 0 and tm % FFN_SUB_TILE == 0
    row = lambda i: (i, 0)
    picked = lambda r, c: pl.BlockSpec((None, None, r, c), lambda i: (*which, 0, 0),
                                       pipeline_mode=pl.Buffered(1))
    in_specs = [pl.BlockSpec((tm, d), row), _resident((1, d)), picked(d, d_ff), picked(d, d_ff),
                picked(d_ff, d)]
    args = [x, gain.reshape(1, d), w_gate, w_up, w_down]
    if mixer_out is not None:
        ya, yb, w_out = mixer_out
        wa, wb = ya.shape[1], yb.shape[1]
        in_specs += [pl.BlockSpec((tm, wa), row), pl.BlockSpec((tm, wb), row),
                     _resident((wa, d)), _resident((wb, d))]
        args += [ya, yb, w_out[:wa].astype(BF16), w_out[wa:].astype(BF16)]
    return pl.pallas_call(
        functools.partial(_ffn_kernel, hidden_tile=tf, sub_tile=FFN_SUB_TILE),
        grid=(t // tm,),
        in_specs=in_specs,
        out_specs=pl.BlockSpec((tm, d), row),
        out_shape=jax.ShapeDtypeStruct((t, d), F32),
        compiler_params=_compiler_params(("parallel",)),
        name="ffn",
    )(*args)


def _inproj_kernel(x_ref, g_ref, w_ref, wqt_ref, wvt_ref, wgate_ref, gqcol_ref, gk_ref, grp_ref, gbias_ref,
                   qat_ref, ka_ref, vat_ref, qkb_ref, vb_ref, ob_ref, gates_ref, *, a_width, b_width):
    h = _rms_normalize(x_ref[...], g_ref[...]).astype(BF16)

    qt = _dot_nt(wqt_ref[...], h)
    for hd in range(a_width // A_HEAD_DIM):
        rows = slice(hd * A_HEAD_DIM, (hd + 1) * A_HEAD_DIM)
        y = qt[rows, :]
        ms = jnp.mean(y * y, axis=0, keepdims=True)
        gain = gqcol_ref[rows, :] * (A_HEAD_DIM ** -0.5 * LOG2E)
        qat_ref[rows, :] = (y * lax.rsqrt(ms + RMS_EPS) * gain).astype(BF16)

    y = _dot(h, w_ref[:, 0:a_width])
    sq_hi, sq_lo = _split_bf16(y * y, 2)
    ssq = _dot(sq_hi, grp_ref[...]) + _dot(sq_lo, grp_ref[...])
    ka_ref[...] = (y * lax.rsqrt(ssq * (1.0 / A_HEAD_DIM) + RMS_EPS) * gk_ref[...]).astype(BF16)

    vat_ref[...] = _dot_nt(wvt_ref[...], h).astype(BF16)
    c0 = a_width
    qkb_ref[...] = _dot(h, w_ref[:, c0:c0 + 2 * b_width]).astype(BF16)
    c0 += 2 * b_width
    vb_ref[...] = _dot(h, w_ref[:, c0:c0 + b_width]).astype(BF16)
    c0 += b_width
    ob_ref[...] = _dot(h, w_ref[:, c0:c0 + b_width]).astype(BF16)
    gates = _dot(h, wgate_ref[...]) + gbias_ref[...]
    gates_ref[...] = gates[:, :2 * B_HEADS]


def _inproj(x, gain, w_in, g_q, g_k, b_i, b_f):
    t, d = x.shape
    a_width = A_HEADS * A_HEAD_DIM
    b_width = B_HEADS * B_HEAD_DIM
    n_main = a_width + 4 * b_width
    o_ib = 3 * a_width + 3 * b_width
    tm = PROJ_TOKEN_TILE
    assert t % tm == 0
    w_main = jnp.concatenate([w_in[:, a_width:2 * a_width], w_in[:, 3 * a_width:o_ib],
                              w_in[:, o_ib + 2 * B_HEADS:]], axis=1).astype(BF16)
    w_qt = w_in[:, :a_width].T.astype(BF16)
    w_vt = w_in[:, 2 * a_width:3 * a_width].T.astype(BF16)
    w_gate = jnp.pad(w_in[:, o_ib:o_ib + 2 * B_HEADS], ((0, 0), (0, LANES - 2 * B_HEADS))).astype(BF16)
    gate_bias = jnp.pad(jnp.concatenate([b_i, b_f]).astype(F32), (0, LANES - 2 * B_HEADS)).reshape(1, LANES)
    head_id = jnp.arange(a_width) // A_HEAD_DIM
    grp = (head_id[:, None] == head_id[None, :]).astype(BF16)
    gq_col = jnp.tile(g_q.astype(F32), A_HEADS).reshape(a_width, 1)
    gk = jnp.tile(g_k.astype(F32), A_HEADS).reshape(1, a_width)
    const = lambda i: (0, 0)
    row = lambda i: (i, 0)
    col = lambda i: (0, i)
    outs = pl.pallas_call(
        functools.partial(_inproj_kernel, a_width=a_width, b_width=b_width),
        grid=(t // tm,),
        in_specs=[
            pl.BlockSpec((tm, d), row),
            pl.BlockSpec((1, d), const),
            pl.BlockSpec((d, n_main), const),
            pl.BlockSpec((a_width, d), const),
            pl.BlockSpec((a_width, d), const),
            pl.BlockSpec((d, LANES), const),
            pl.BlockSpec((a_width, 1), const),
            pl.BlockSpec((1, a_width), const),
            pl.BlockSpec((a_width, a_width), const),
            pl.BlockSpec((1, LANES), const),
        ],
        out_specs=[
            pl.BlockSpec((a_width, tm), col),
            pl.BlockSpec((tm, a_width), row),
            pl.BlockSpec((a_width, tm), col),
            pl.BlockSpec((tm, 2 * b_width), row),
            pl.BlockSpec((tm, b_width), row),
            pl.BlockSpec((tm, b_width), row),
            pl.BlockSpec((tm, 2 * B_HEADS), row),
        ],
        out_shape=[
            jax.ShapeDtypeStruct((a_width, t), BF16),
            jax.ShapeDtypeStruct((t, a_width), BF16),
            jax.ShapeDtypeStruct((a_width, t), BF16),
            jax.ShapeDtypeStruct((t, 2 * b_width), BF16),
            jax.ShapeDtypeStruct((t, b_width), BF16),
            jax.ShapeDtypeStruct((t, b_width), BF16),
            jax.ShapeDtypeStruct((t, 2 * B_HEADS), F32),
        ],
        compiler_params=_compiler_params(("parallel",)),
        name="inproj",
    )(x, gain.reshape(1, d), w_main, w_qt, w_vt, w_gate, gq_col, gk, grp, gate_bias)
    return outs


def _moba_kernel(slope_ref, qt_ref, k_ref, vt_ref, o_ref, kaug_ref, vaug_ref, kmean_ref, qaug_ref, acc_ref,
                 sa_ref, sb_ref, *, nb, grp, heads, tiles):
    hg = pl.program_id(1)
    c0 = pl.program_id(2) * tiles
    blk = MOBA_BLOCK
    dh = A_HEAD_DIM
    nbp = pl.cdiv(nb, SUBLANES_BF16) * SUBLANES_BF16
    streams = [(ti, h) for ti in range(tiles) for h in range(heads)]

    @pl.when(c0 == 0)
    def _build_key_value_side():
        ones_rows = jnp.where(lax.broadcasted_iota(jnp.int32, (V_AUG_ROWS - dh, blk), 0) == 0,
                              1.0, 0.0).astype(BF16)
        for h in range(heads):
            for j in range(nb):
                g, off = divmod(j, grp)
                vaug_ref[h, g, 0:dh, off * blk:(off + 1) * blk] = vt_ref[dh * h:dh * (h + 1),
                                                                         j * blk:(j + 1) * blk]
                vaug_ref[h, g, dh:V_AUG_ROWS, off * blk:(off + 1) * blk] = ones_rows
        kmean_ref[...] = jnp.zeros_like(kmean_ref)
        lane = lax.broadcasted_iota(jnp.int32, (blk, LANES), 1)
        row = lax.broadcasted_iota(jnp.int32, (blk, LANES), 0)

        def body(j, carry):
            start = pl.multiple_of(j * blk, blk)
            pos = (row + j * blk).astype(F32)
            for pr in range(heads // 2):
                kb = k_ref[pl.ds(start, blk), pr * LANES:(pr + 1) * LANES].astype(F32)
                kmean_ref[pr, pl.ds(j, 1), :] = jnp.mean(kb, axis=0, keepdims=True)
                for hh in range(2):
                    h = 2 * pr + hh
                    slope = slope_ref[pl.ds(heads * hg + h, 1), :][:, 0:LANES] * LOG2E
                    p1, p2, p3 = _split_bf16(slope * pos, 3)
                    rel = lane - dh * (1 - hh)
                    aug = jnp.where(rel == j, 1.0, 0.0)
                    aug = jnp.where(rel == nb, p1.astype(F32), aug)
                    aug = jnp.where(rel == nb + 1, p2.astype(F32), aug)
                    aug = jnp.where(rel == nb + 2, p3.astype(F32), aug)
                    aug = jnp.where((rel >= nb + 3) & (rel < nb + 6), 1.0, aug)
                    is_data = (lane >= dh * hh) & (lane < dh * (hh + 1))
                    kaug_ref[h, pl.ds(start, blk), :] = jnp.where(is_data, kb, aug).astype(BF16)
            return carry

        lax.fori_loop(0, nb, body, 0)

    blk_ix = lax.broadcasted_iota(jnp.int32, (nbp, blk), 0)
    blk_f = blk_ix.astype(F32)
    aug_row = lax.broadcasted_iota(jnp.int32, (dh, blk), 0)
    qry_lane = lax.broadcasted_iota(jnp.int32, (dh, blk), 1)
    lane_k = lax.broadcasted_iota(jnp.int32, (nbp, LANES), 1)
    own_grp = c0 // grp
    key_ix = lax.broadcasted_iota(jnp.int32, (grp * blk, blk), 0) + own_grp * (grp * blk)
    qry_ix = lax.broadcasted_iota(jnp.int32, (grp * blk, blk), 1)
    gates = []
    for ti, h in streams:
        pr, hh = divmod(h, 2)
        qt_pair = qt_ref[pr * LANES:(pr + 1) * LANES, ti * blk:(ti + 1) * blk]
        is_data_k = (lane_k >= dh * hh) & (lane_k < dh * (hh + 1))
        km_hi, km_lo = _split_bf16(jnp.where(is_data_k, kmean_ref[pr, 0:nbp, :], 0.0), 2)
        gates.append(_dot(km_hi, qt_pair) + _dot(km_lo, qt_pair))
    for si, (ti, h) in enumerate(streams):
        pr, hh = divmod(h, 2)
        c = c0 + ti
        valid = blk_ix < c
        tq = (qry_lane + c * blk).astype(F32)
        gate = jnp.where(valid, gates[si], -jnp.inf)
        chosen = blk_ix == c
        for _ in range(MOBA_TOPK):
            best = jnp.max(gate, axis=0, keepdims=True)
            first = jnp.min(jnp.where(gate == best, blk_f, float(nbp)), axis=0, keepdims=True)
            pick = blk_f == first
            chosen = chosen | (pick & valid)
            gate = jnp.where(pick, -jnp.inf, gate)
        bias = jnp.where(chosen, 0.0, MASK_VALUE)
        if nbp < dh:
            bias = jnp.concatenate([bias, jnp.zeros((dh - nbp, blk), F32)], axis=0)
        slope = slope_ref[pl.ds(heads * hg + h, 1), :] * LOG2E
        t1, t2, t3 = _split_bf16(-slope * tq, 3)
        aug = jnp.where(aug_row < nb, bias, 0.0)
        aug = jnp.where((aug_row >= nb) & (aug_row < nb + 3), 1.0, aug)
        aug = jnp.where(aug_row == nb + 3, t1.astype(F32), aug)
        aug = jnp.where(aug_row == nb + 4, t2.astype(F32), aug)
        aug = jnp.where(aug_row == nb + 5, t3.astype(F32), aug).astype(BF16)
        data = qt_ref[dh * h:dh * (h + 1), ti * blk:(ti + 1) * blk]
        qaug_ref[si] = jnp.concatenate([data, aug] if hh == 0 else [aug, data], axis=0)

    def scores_into(buf_ref, g):
        start = pl.multiple_of(g * (grp * blk), grp * blk)
        col_max = []
        for si, (ti, h) in enumerate(streams):
            s = _dot(kaug_ref[h, pl.ds(start, grp * blk), :], qaug_ref[si])
            buf_ref[si] = s
            col_max.append(jnp.max(s, axis=0, keepdims=True))
        return tuple(col_max)

    def consume(buf_ref, g, ms, col_max, own=False):
        new_ms = []
        for si, (ti, h) in enumerate(streams):
            s = buf_ref[si]
            if own:
                s = jnp.where(key_ix <= qry_ix + (c0 + ti) * blk, s, MASK_VALUE)
                group_max = jnp.max(s, axis=0, keepdims=True)
            else:
                group_max = col_max[si]
            m_new = jnp.maximum(ms[si], group_max)
            alpha = jnp.exp2(ms[si] - m_new)
            p = jnp.exp2(s - m_new)
            acc_ref[si] = alpha * acc_ref[si] + _dot(vaug_ref[h, g], p.astype(BF16))
            new_ms.append(m_new)
        return tuple(new_ms)

    def write_output():
        for ti in range(tiles):
            outs = []
            for h in range(heads):
                acc = acc_ref[ti * heads + h]
                outs.append(acc[0:dh, :] / acc[dh:dh + 1, :])
            o_ref[ti * blk:(ti + 1) * blk, :] = jnp.concatenate(outs, axis=0).T.astype(o_ref.dtype)

    acc_ref[...] = jnp.zeros_like(acc_ref)
    masked_score = jnp.full((1, blk), MASK_VALUE, F32).astype(BF16).astype(F32)
    max_a0 = scores_into(sa_ref, 0)

    def two_groups(i, carry):
        ms, max_a = carry
        max_b = scores_into(sb_ref, 2 * i + 1)
        ms = consume(sa_ref, 2 * i, ms, max_a)
        max_a = scores_into(sa_ref, 2 * i + 2)
        return consume(sb_ref, 2 * i + 1, ms, max_b), max_a

    ms, max_a = lax.fori_loop(0, own_grp // 2, two_groups, ((masked_score,) * len(streams), max_a0))

    @pl.when(own_grp % 2 == 0)
    def _own_group_in_a():
        consume(sa_ref, own_grp, ms, None, own=True)
        write_output()

    @pl.when(own_grp % 2 == 1)
    def _own_group_in_b():
        scores_into(sb_ref, own_grp)
        consume(sb_ref, own_grp, consume(sa_ref, own_grp - 1, ms, max_a), None, own=True)
        write_output()


def _moba(qat, ka, vat, bsz, seq):
    t, a_width = ka.shape
    blk = MOBA_BLOCK
    assert seq % blk == 0 and seq // blk >= MOBA_TOPK
    nb = seq // blk
    assert nb + 6 <= A_HEAD_DIM
    grp = MOBA_KEY_GROUP if nb % MOBA_KEY_GROUP == 0 else 1
    heads = MOBA_HEADS_PER_STEP
    width = heads * A_HEAD_DIM
    assert heads % 2 == 0 and a_width % width == 0
    tiles = grp
    n_streams = tiles * heads
    steps = nb // tiles
    slopes = jnp.exp2(-8.0 * jnp.arange(1, A_HEADS + 1, dtype=F32) / A_HEADS)
    slope_tbl = jnp.broadcast_to(slopes[:, None], (A_HEADS, blk))
    return pl.pallas_call(
        functools.partial(_moba_kernel, nb=nb, grp=grp, heads=heads, tiles=tiles),
        grid=(bsz, a_width // width, steps),
        in_specs=[
            pl.BlockSpec((A_HEADS, blk), lambda b, hg, c: (0, 0)),
            pl.BlockSpec((width, tiles * blk), lambda b, hg, c: (hg, b * steps + c)),
            pl.BlockSpec((seq, width), lambda b, hg, c: (b, hg), pipeline_mode=pl.Buffered(1)),
            pl.BlockSpec((width, seq), lambda b, hg, c: (hg, b), pipeline_mode=pl.Buffered(1)),
        ],
        out_specs=pl.BlockSpec((tiles * blk, width), lambda b, hg, c: (b * steps + c, hg)),
        out_shape=jax.ShapeDtypeStruct((t, a_width), BF16),
        scratch_shapes=[
            pltpu.VMEM((heads, seq, LANES), BF16),
            pltpu.VMEM((heads, nb // grp, V_AUG_ROWS, grp * blk), BF16),
            pltpu.VMEM((heads // 2, LANES, LANES), F32),
            pltpu.VMEM((n_streams, LANES, blk), BF16),
            pltpu.VMEM((n_streams, V_AUG_ROWS, blk), F32),
            pltpu.VMEM((n_streams, grp * blk, blk), F32),
            pltpu.VMEM((n_streams, grp * blk, blk), F32),
        ],
        compiler_params=_compiler_params(("parallel", "parallel", "arbitrary")),
        name="moba",
    )(slope_tbl, qat, ka, vat)


def _mlstm_kernel(q_ref, k_ref, v_ref, ob_ref, gcol_ref, grow_ref, cwq_ref, cwk_ref, cbq_ref, cbk_ref,
                  o_ref, qbuf_ref, kbuf_ref, c_ref, m_ref):
    ci = pl.program_id(1)
    L, width = q_ref.shape
    d = B_HEAD_DIM
    halo = SUBLANES_F32
    heads = range(B_HEADS)
    cols = [slice(hd * d, (hd + 1) * d) for hd in heads]

    @pl.when(ci == 0)
    def _():
        qbuf_ref[0:halo, :] = jnp.zeros((halo, width), F32)
        kbuf_ref[0:halo, :] = jnp.zeros((halo, width), F32)
        c_ref[...] = jnp.zeros_like(c_ref)
        m_ref[...] = jnp.zeros_like(m_ref)

    def conv_silu(x_ref, buf_ref, w_ref, b_ref):
        buf_ref[halo:halo + L, :] = x_ref[...].astype(F32)
        y = b_ref[...]
        for j in range(CONV_WIDTH):
            off = halo - (CONV_WIDTH - 1) + j
            y = y + buf_ref[off:off + L, :] * w_ref[j:j + 1, :]
        buf_ref[0:halo, :] = buf_ref[L:L + halo, :]
        return y * jax.nn.sigmoid(y)

    def log_sigmoid(z):
        return jnp.minimum(z, 0.0) - jnp.log1p(jnp.exp(-jnp.abs(z)))

    rr = lax.broadcasted_iota(jnp.int32, (L, L), 0)
    cc = lax.broadcasted_iota(jnp.int32, (L, L), 1)
    causal = rr >= cc
    tri_low = causal.astype(BF16)
    tri_up = (rr <= cc).astype(BF16)
    lane_l = lax.broadcasted_iota(jnp.int32, (L, LANES), 1)
    sub8 = lax.broadcasted_iota(jnp.int32, (8, L), 0)
    ones_col = jnp.where(lax.broadcasted_iota(jnp.int32, (L, d), 1) == 0, 1.0, 0.0).astype(BF16)
    gcol = gcol_ref[...]
    q_all = (conv_silu(q_ref, qbuf_ref, cwq_ref, cbq_ref) * (d ** -0.5)).astype(BF16)
    k_all = conv_silu(k_ref, kbuf_ref, cwk_ref, cbk_ref)
    for hd in heads:
        q = q_all[:, cols[hd]]
        k = k_all[:, cols[hd]]
        v_aug = jnp.concatenate([v_ref[:, cols[hd]], ones_col], axis=1)
        i_col = gcol[:, hd:hd + 1]
        f_col = gcol[:, hd + B_HEADS:hd + B_HEADS + 1]
        i_row = grow_ref[0, hd:hd + 1, :]
        f_row = grow_ref[0, hd + B_HEADS:hd + B_HEADS + 1, :]

        c_hi, c_lo = _split_bf16(jnp.where(lane_l == 0, log_sigmoid(f_col), 0.0), 2)
        b_col = (_dot(tri_low, c_hi) + _dot(tri_low, c_lo))[:, 0:1]
        r_hi, r_lo = _split_bf16(jnp.where(sub8 == 0, log_sigmoid(f_row), 0.0), 2)
        b_row = (_dot(r_hi, tri_up) + _dot(r_lo, tri_up))[0:1, :]

        m_prev = m_ref[hd]
        log_inter = b_col + m_prev
        dmat = jnp.where(causal, b_col - b_row + i_row, -jnp.inf)
        m_t = jnp.maximum(log_inter, jnp.max(dmat, axis=-1, keepdims=True))
        w_inter = jnp.exp(log_inter - m_t)
        sc = _dot_nt(q, k.astype(BF16)) * jnp.exp(dmat - m_t)
        c_prev = c_ref[hd]
        inter = w_inter * _dot(q, c_prev.astype(BF16))
        num = inter[:, :d] + _dot(sc.astype(BF16), v_ref[:, cols[hd]])
        den = inter[:, d:d + 1] + jnp.sum(sc, axis=-1, keepdims=True)
        hidden = num / jnp.maximum(jnp.abs(den), jnp.exp(-m_t))
        o_ref[:, cols[hd]] = (jax.nn.sigmoid(ob_ref[:, cols[hd]].astype(F32)) * hidden).astype(o_ref.dtype)

        b_last = b_col[L - 1:L, :]
        log_old = b_last + m_prev
        log_new = b_last - b_col + i_col
        m_new = jnp.maximum(log_old, jnp.max(log_new, axis=0, keepdims=True))
        a_old = jnp.exp(log_old - m_new)
        a_new = jnp.exp(log_new - m_new)
        c_ref[hd] = a_old * c_prev + _dot_tn((a_new * k).astype(BF16), v_aug)
        m_ref[hd] = m_new


def _mlstm(qkb, vb, ob, gates, conv_w, conv_b, bsz, seq):
    t, b_width = vb.shape
    d = B_HEAD_DIM
    L = min(MLSTM_KERNEL_CHUNK, seq)
    assert seq % L == 0 and d == LANES and b_width == B_HEADS * d
    nc = seq // L
    gates_row = gates.reshape(bsz, seq, 2 * B_HEADS).transpose(0, 2, 1)
    cw = conv_w.astype(F32)
    cb = conv_b.astype(F32).reshape(1, 2 * b_width)
    tok = lambda off: (lambda b, c: (b * nc + c, off))
    return pl.pallas_call(
        _mlstm_kernel,
        grid=(bsz, nc),
        in_specs=[
            pl.BlockSpec((L, b_width), tok(0)),
            pl.BlockSpec((L, b_width), tok(1)),
            pl.BlockSpec((L, b_width), tok(0)),
            pl.BlockSpec((L, b_width), tok(0)),
            pl.BlockSpec((L, 2 * B_HEADS), tok(0)),
            pl.BlockSpec((1, 2 * B_HEADS, L), lambda b, c: (b, 0, c)),
            pl.BlockSpec((CONV_WIDTH, b_width), lambda b, c: (0, 0)),
            pl.BlockSpec((CONV_WIDTH, b_width), lambda b, c: (0, 1)),
            pl.BlockSpec((1, b_width), lambda b, c: (0, 0)),
            pl.BlockSpec((1, b_width), lambda b, c: (0, 1)),
        ],
        out_specs=pl.BlockSpec((L, b_width), tok(0)),
        out_shape=jax.ShapeDtypeStruct((t, b_width), BF16),
        scratch_shapes=[
            pltpu.VMEM((L + SUBLANES_F32, b_width), F32),
            pltpu.VMEM((L + SUBLANES_F32, b_width), F32),
            pltpu.VMEM((B_HEADS, d, 2 * d), F32),
            pltpu.VMEM((B_HEADS, 1, 1), F32),
        ],
        compiler_params=_compiler_params(("parallel", "arbitrary")),
        name="mlstm",
    )(qkb, qkb, vb, ob, gates, gates_row, cw, cw, cb, cb)


def _pool_kernel(x_ref, halo_ref, g_ref, w_ref, scale_ref, o_ref, sums_ref, *, tiles_per_seq):
    i = pl.program_id(0)
    tm, d = x_ref.shape
    n_win = len(POOL_WINDOWS)
    grp = d // n_win
    base = POOL_PAD + POOL_HALO
    rows = POOL_HALO + tm
    x = x_ref[...]
    seq_tile = i % tiles_per_seq
    sums_ref[:, 0:POOL_PAD, :] = jnp.zeros((n_win, POOL_PAD, d), F32)
    halo_h = _rms_normalize(halo_ref[...], g_ref[...])
    sums_ref[0, POOL_PAD:base, :] = jnp.where(seq_tile == 0, 0.0, halo_h)
    h = _rms_normalize(x, g_ref[...])
    sums_ref[0, base:base + tm, :] = h
    t1 = (lax.broadcasted_iota(jnp.int32, (tm, 1), 0) + seq_tile * tm + 1).astype(F32)
    for k, win in enumerate(POOL_WINDOWS):
        half = win // 2
        c0 = k * grp
        if k + 1 < n_win:
            both = (sums_ref[k, POOL_PAD:POOL_PAD + rows, c0:]
                    + sums_ref[k, POOL_PAD - half:POOL_PAD - half + rows, c0:])
            sums_ref[k + 1, POOL_PAD:POOL_PAD + rows, c0:] = both
            total = both[POOL_HALO:, 0:grp]
        else:
            total = (sums_ref[k, base:base + tm, c0:c0 + grp]
                     + sums_ref[k, base - half:base - half + tm, c0:c0 + grp])
        pooled = total / jnp.minimum(t1, float(win)) - h[:, c0:c0 + grp]
        y = _dot(pooled.astype(BF16), w_ref[k])
        o_ref[:, c0:c0 + grp] = x[:, c0:c0 + grp] + y * scale_ref[:, c0:c0 + grp]


def _pool(x, gain, w_grp, scale, seq):
    t, d = x.shape
    tm = min(POOL_TOKEN_TILE, seq)
    assert seq % tm == 0 and tm % POOL_HALO == 0
    assert POOL_WINDOWS == tuple(2 ** (k + 1) for k in range(len(POOL_WINDOWS)))
    assert POOL_WINDOWS[-1] <= POOL_HALO and POOL_WINDOWS[-1] // 2 <= POOL_PAD
    n_grp, grp, _ = w_grp.shape
    halo_blocks = tm // POOL_HALO
    return pl.pallas_call(
        functools.partial(_pool_kernel, tiles_per_seq=seq // tm),
        grid=(t // tm,),
        in_specs=[
            pl.BlockSpec((tm, d), lambda i: (i, 0)),
            pl.BlockSpec((POOL_HALO, d), lambda i: (jnp.maximum(i * halo_blocks - 1, 0), 0)),
            pl.BlockSpec((1, d), lambda i: (0, 0)),
            pl.BlockSpec((n_grp, grp, grp), lambda i: (0, 0, 0)),
            pl.BlockSpec((1, d), lambda i: (0, 0)),
        ],
        out_specs=pl.BlockSpec((tm, d), lambda i: (i, 0)),
        out_shape=jax.ShapeDtypeStruct((t, d), F32),
        scratch_shapes=[pltpu.VMEM((len(POOL_WINDOWS), POOL_PAD + POOL_HALO + tm, d), F32)],
        compiler_params=_compiler_params(("parallel",)),
        name="pool",
    )(x, x, gain.reshape(1, d), w_grp.astype(BF16), scale.astype(F32).reshape(1, d))


def _mixer_heads(x, gain, w_in, g_q, g_k, conv_w, conv_b, b_i, b_f, bsz, seq):
    qat, ka, vat, qkb, vb, ob, gates = _inproj(x, gain, w_in, g_q, g_k, b_i, b_f)
    ya = _moba(qat, ka, vat, bsz, seq)
    yb = _mlstm(qkb, vb, ob, gates, conv_w, conv_b, bsz, seq)
    return ya, yb


def kernel(x, norm_g, ffn_w_gate, ffn_w_up, ffn_w_down, ab_w_in, ab_w_out, ab_g_q, ab_g_k,
           ab_conv_w, ab_conv_b, ab_b_i, ab_b_f, pool_w, pool_scale):
    bsz, seq, d = x.shape
    depth = norm_g.shape[0]
    y = x.reshape(bsz * seq, d)
    w_gate, w_up, w_down = (w.astype(BF16) for w in (ffn_w_gate, ffn_w_up, ffn_w_down))
    for layer in range(depth):
        y = _ffn(y, norm_g[layer, 0], w_gate, w_up, w_down, (layer, 0))
        mixer_out = None
        if layer % 2 == 0:
            e = layer // 2
            ya, yb = _mixer_heads(y, norm_g[layer, 1], ab_w_in[e], ab_g_q[e], ab_g_k[e],
                                  ab_conv_w[e], ab_conv_b[e], ab_b_i[e], ab_b_f[e], bsz, seq)
            mixer_out = (ya, yb, ab_w_out[e])
        else:
            o = layer // 2
            y = _pool(y, norm_g[layer, 1], pool_w[o], pool_scale[o], seq)
        y = _ffn(y, norm_g[layer, 2], w_gate, w_up, w_down, (layer, 1), mixer_out=mixer_out)
    return y.reshape(bsz, seq, d)
```

```python
import functools

import jax
import jax.numpy as jnp
from jax import lax
from jax.experimental import pallas as pl
from jax.experimental.pallas import tpu as pltpu

F32 = jnp.float32
BF16 = jnp.bfloat16

LANES = 128
SUBLANES_F32 = 8
SUBLANES_BF16 = 16
MXU_WIDTH = 256
VMEM_BYTES = 64 * 1024 * 1024
VMEM_LIMIT_BYTES = VMEM_BYTES * 7 // 8

RMS_EPS = 1e-6
A_HEADS = 8
A_HEAD_DIM = 64
MOBA_BLOCK = 256
MOBA_TOPK = 3
B_HEADS = 4
B_HEAD_DIM = 128
CONV_WIDTH = 4
POOL_WINDOWS = (2, 4, 8, 16)

MOBA_KEY_GROUP = 2
MOBA_HEADS_PER_STEP = 4
MLSTM_KERNEL_CHUNK = 256
POOL_HALO = 16
POOL_PAD = SUBLANES_F32
MASK_VALUE = -1e30
LOG2E = 1.4426950408889634
V_AUG_ROWS = A_HEAD_DIM + SUBLANES_BF16
FFN_TOKEN_TILE = 1024
FFN_SUB_TILE = 512
FFN_HIDDEN_TILE = MXU_WIDTH
PROJ_TOKEN_TILE = 1024
POOL_TOKEN_TILE = 1024


def _compiler_params(semantics):
    return pltpu.CompilerParams(dimension_semantics=semantics,
                                vmem_limit_bytes=VMEM_LIMIT_BYTES)


def _rms_normalize(x, gain):
    ms = jnp.mean(x * x, axis=-1, keepdims=True)
    return x * lax.rsqrt(ms + RMS_EPS) * gain


def _dot(a, b):
    return jnp.dot(a, b, preferred_element_type=F32)


def _dot_nt(a, b):
    return lax.dot_general(a, b, (((1,), (1,)), ((), ())), preferred_element_type=F32)


def _dot_tn(a, b):
    return lax.dot_general(a, b, (((0,), (0,)), ((), ())), preferred_element_type=F32)


def _split_bf16(x, parts):
    out = []
    rem = x
    for _ in range(parts):
        p = rem.astype(BF16)
        out.append(p)
        rem = rem - p.astype(F32)
    return out


def _ffn_kernel(x_ref, g_ref, wg_ref, wu_ref, wd_ref, *rest, hidden_tile, sub_tile):
    o_ref = rest[-1]
    d_ff = wg_ref.shape[1]
    for r0 in range(0, x_ref.shape[0], sub_tile):
        x = x_ref[r0:r0 + sub_tile, :]
        if len(rest) > 1:
            ya_ref, yb_ref, wa_ref, wb_ref = rest[:-1]
            x = (x + _dot(ya_ref[r0:r0 + sub_tile, :], wa_ref[...])
                 + _dot(yb_ref[r0:r0 + sub_tile, :], wb_ref[...]))
        h = _rms_normalize(x, g_ref[...]).astype(BF16)
        acc = None
        for c0 in range(0, d_ff, hidden_tile):
            gate = _dot(h, wg_ref[:, c0:c0 + hidden_tile])
            up = _dot(h, wu_ref[:, c0:c0 + hidden_tile])
            act = (gate * jax.nn.sigmoid(gate) * up).astype(BF16)
            part = _dot(act, wd_ref[c0:c0 + hidden_tile, :])
            acc = part if acc is None else acc + part
        o_ref[r0:r0 + sub_tile, :] = x + 0.5 * acc


def _resident(shape):
    return pl.BlockSpec(shape, lambda *_: (0,) * len(shape), pipeline_mode=pl.Buffered(1))


def _ffn(x, gain, w_gate, w_up, w_down, which, mixer_out=None):
    t, d = x.shape
    d_ff = w_gate.shape[-1]
    tm, tf = FFN_TOKEN_TILE, FFN_HIDDEN_TILE
    assert t % tm == 0 and d_ff % tf == 0 and tm % FFN_SUB_TILE == 0
    row = lambda i: (i, 0)
    picked = lambda r, c: pl.BlockSpec((None, None, r, c), lambda i: (*which, 0, 0),
                                       pipeline_mode=pl.Buffered(1))
    in_specs = [pl.BlockSpec((tm, d), row), _resident((1, d)), picked(d, d_ff), picked(d, d_ff),
                picked(d_ff, d)]
    args = [x, gain.reshape(1, d), w_gate, w_up, w_down]
    if mixer_out is not None:
        ya, yb, w_out = mixer_out
        wa, wb = ya.shape[1], yb.shape[1]
        in_specs += [pl.BlockSpec((tm, wa), row), pl.BlockSpec((tm, wb), row),
                     _resident((wa, d)), _resident((wb, d))]
        args += [ya, yb, w_out[:wa].astype(BF16), w_out[wa:].astype(BF16)]
    return pl.pallas_call(
        functools.partial(_ffn_kernel, hidden_tile=tf, sub_tile=FFN_SUB_TILE),
        grid=(t // tm,),
        in_specs=in_specs,
        out_specs=pl.BlockSpec((tm, d), row),
        out_shape=jax.ShapeDtypeStruct((t, d), F32),
        compiler_params=_compiler_params(("parallel",)),
        name="ffn",
    )(*args)


def _inproj_kernel(x_ref, g_ref, w_ref, wqt_ref, wvt_ref, wbt_ref, wgate_ref, gqcol_ref, gk_ref, grp_ref,
                   gbias_ref, qat_ref, ka_ref, vat_ref, kb_ref, qbt_ref, vbt_ref, obt_ref, gates_ref,
                   *, a_width, b_width):
    h = _rms_normalize(x_ref[...], g_ref[...]).astype(BF16)

    qt = _dot_nt(wqt_ref[...], h)
    for hd in range(a_width // A_HEAD_DIM):
        rows = slice(hd * A_HEAD_DIM, (hd + 1) * A_HEAD_DIM)
        y = qt[rows, :]
        ms = jnp.mean(y * y, axis=0, keepdims=True)
        gain = gqcol_ref[rows, :] * (A_HEAD_DIM ** -0.5 * LOG2E)
        qat_ref[rows, :] = (y * lax.rsqrt(ms + RMS_EPS) * gain).astype(BF16)

    y = _dot(h, w_ref[:, 0:a_width])
    sq_hi, sq_lo = _split_bf16(y * y, 2)
    ssq = _dot(sq_hi, grp_ref[...]) + _dot(sq_lo, grp_ref[...])
    ka_ref[...] = (y * lax.rsqrt(ssq * (1.0 / A_HEAD_DIM) + RMS_EPS) * gk_ref[...]).astype(BF16)

    vat_ref[...] = _dot_nt(wvt_ref[...], h).astype(BF16)
    kb_ref[...] = _dot(h, w_ref[:, a_width:a_width + b_width]).astype(BF16)
    for piece, out_ref in enumerate((qbt_ref, vbt_ref, obt_ref)):
        out_ref[...] = _dot_nt(wbt_ref[piece * b_width:(piece + 1) * b_width, :], h).astype(BF16)
    gates = _dot(h, wgate_ref[...]) + gbias_ref[...]
    gates_ref[...] = gates[:, :2 * B_HEADS]


def _inproj(x, gain, w_in, g_q, g_k, b_i, b_f):
    t, d = x.shape
    a_width = A_HEADS * A_HEAD_DIM
    b_width = B_HEADS * B_HEAD_DIM
    n_main = a_width + b_width
    o_qb = 3 * a_width
    o_ib = 3 * a_width + 3 * b_width
    tm = PROJ_TOKEN_TILE
    assert t % tm == 0
    w_main = jnp.concatenate([w_in[:, a_width:2 * a_width], w_in[:, o_qb + b_width:o_qb + 2 * b_width]],
                             axis=1).astype(BF16)
    w_qt = w_in[:, :a_width].T.astype(BF16)
    w_vt = w_in[:, 2 * a_width:3 * a_width].T.astype(BF16)
    w_bt = jnp.concatenate([w_in[:, o_qb:o_qb + b_width], w_in[:, o_qb + 2 * b_width:o_ib],
                            w_in[:, o_ib + 2 * B_HEADS:]], axis=1).T.astype(BF16)
    w_gate = jnp.pad(w_in[:, o_ib:o_ib + 2 * B_HEADS], ((0, 0), (0, LANES - 2 * B_HEADS))).astype(BF16)
    gate_bias = jnp.pad(jnp.concatenate([b_i, b_f]).astype(F32), (0, LANES - 2 * B_HEADS)).reshape(1, LANES)
    head_id = jnp.arange(a_width) // A_HEAD_DIM
    grp = (head_id[:, None] == head_id[None, :]).astype(BF16)
    gq_col = jnp.tile(g_q.astype(F32), A_HEADS).reshape(a_width, 1)
    gk = jnp.tile(g_k.astype(F32), A_HEADS).reshape(1, a_width)
    const = lambda i: (0, 0)
    row = lambda i: (i, 0)
    col = lambda i: (0, i)
    outs = pl.pallas_call(
        functools.partial(_inproj_kernel, a_width=a_width, b_width=b_width),
        grid=(t // tm,),
        in_specs=[
            pl.BlockSpec((tm, d), row),
            pl.BlockSpec((1, d), const),
            pl.BlockSpec((d, n_main), const),
            pl.BlockSpec((a_width, d), const),
            pl.BlockSpec((a_width, d), const),
            pl.BlockSpec((3 * b_width, d), const),
            pl.BlockSpec((d, LANES), const),
            pl.BlockSpec((a_width, 1), const),
            pl.BlockSpec((1, a_width), const),
            pl.BlockSpec((a_width, a_width), const),
            pl.BlockSpec((1, LANES), const),
        ],
        out_specs=[
            pl.BlockSpec((a_width, tm), col),
            pl.BlockSpec((tm, a_width), row),
            pl.BlockSpec((a_width, tm), col),
            pl.BlockSpec((tm, b_width), row),
            pl.BlockSpec((b_width, tm), col),
            pl.BlockSpec((b_width, tm), col),
            pl.BlockSpec((b_width, tm), col),
            pl.BlockSpec((tm, 2 * B_HEADS), row),
        ],
        out_shape=[
            jax.ShapeDtypeStruct((a_width, t), BF16),
            jax.ShapeDtypeStruct((t, a_width), BF16),
            jax.ShapeDtypeStruct((a_width, t), BF16),
            jax.ShapeDtypeStruct((t, b_width), BF16),
            jax.ShapeDtypeStruct((b_width, t), BF16),
            jax.ShapeDtypeStruct((b_width, t), BF16),
            jax.ShapeDtypeStruct((b_width, t), BF16),
            jax.ShapeDtypeStruct((t, 2 * B_HEADS), F32),
        ],
        compiler_params=_compiler_params(("parallel",)),
        name="inproj",
    )(x, gain.reshape(1, d), w_main, w_qt, w_vt, w_bt, w_gate, gq_col, gk, grp, gate_bias)
    return outs


def _moba_kernel(slope_ref, qt_ref, k_ref, vt_ref, o_ref, kaug_ref, vaug_ref, kmean_ref, qaug_ref, acc_ref,
                 sa_ref, sb_ref, *, nb, grp, heads, tiles):
    hg = pl.program_id(1)
    c0 = pl.program_id(2) * tiles
    blk = MOBA_BLOCK
    dh = A_HEAD_DIM
    nbp = pl.cdiv(nb, SUBLANES_BF16) * SUBLANES_BF16
    streams = [(ti, h) for ti in range(tiles) for h in range(heads)]

    @pl.when(c0 == 0)
    def _build_key_value_side():
        ones_rows = jnp.where(lax.broadcasted_iota(jnp.int32, (V_AUG_ROWS - dh, blk), 0) == 0,
                              1.0, 0.0).astype(BF16)
        for h in range(heads):
            for j in range(nb):
                g, off = divmod(j, grp)
                vaug_ref[h, g, 0:dh, off * blk:(off + 1) * blk] = vt_ref[dh * h:dh * (h + 1),
                                                                         j * blk:(j + 1) * blk]
                vaug_ref[h, g, dh:V_AUG_ROWS, off * blk:(off + 1) * blk] = ones_rows
        kmean_ref[...] = jnp.zeros_like(kmean_ref)
        lane = lax.broadcasted_iota(jnp.int32, (blk, LANES), 1)
        row = lax.broadcasted_iota(jnp.int32, (blk, LANES), 0)

        def body(j, carry):
            start = pl.multiple_of(j * blk, blk)
            pos = (row + j * blk).astype(F32)
            for pr in range(heads // 2):
                kb = k_ref[pl.ds(start, blk), pr * LANES:(pr + 1) * LANES].astype(F32)
                kmean_ref[pr, pl.ds(j, 1), :] = jnp.mean(kb, axis=0, keepdims=True)
                for hh in range(2):
                    h = 2 * pr + hh
                    slope = slope_ref[pl.ds(heads * hg + h, 1), :][:, 0:LANES] * LOG2E
                    p1, p2, p3 = _split_bf16(slope * pos, 3)
                    rel = lane - dh * (1 - hh)
                    aug = jnp.where(rel == j, 1.0, 0.0)
                    aug = jnp.where(rel == nb, p1.astype(F32), aug)
                    aug = jnp.where(rel == nb + 1, p2.astype(F32), aug)
                    aug = jnp.where(rel == nb + 2, p3.astype(F32), aug)
                    aug = jnp.where((rel >= nb + 3) & (rel < nb + 6), 1.0, aug)
                    is_data = (lane >= dh * hh) & (lane < dh * (hh + 1))
                    kaug_ref[h, pl.ds(start, blk), :] = jnp.where(is_data, kb, aug).astype(BF16)
            return carry

        lax.fori_loop(0, nb, body, 0)

    blk_ix = lax.broadcasted_iota(jnp.int32, (nbp, blk), 0)
    blk_f = blk_ix.astype(F32)
    aug_row = lax.broadcasted_iota(jnp.int32, (dh, blk), 0)
    qry_lane = lax.broadcasted_iota(jnp.int32, (dh, blk), 1)
    lane_k = lax.broadcasted_iota(jnp.int32, (nbp, LANES), 1)
    own_grp = c0 // grp
    key_ix = lax.broadcasted_iota(jnp.int32, (grp * blk, blk), 0) + own_grp * (grp * blk)
    qry_ix = lax.broadcasted_iota(jnp.int32, (grp * blk, blk), 1)
    gates = []
    for ti, h in streams:
        pr, hh = divmod(h, 2)
        qt_pair = qt_ref[pr * LANES:(pr + 1) * LANES, ti * blk:(ti + 1) * blk]
        is_data_k = (lane_k >= dh * hh) & (lane_k < dh * (hh + 1))
        km_hi, km_lo = _split_bf16(jnp.where(is_data_k, kmean_ref[pr, 0:nbp, :], 0.0), 2)
        gates.append(_dot(km_hi, qt_pair) + _dot(km_lo, qt_pair))
    for si, (ti, h) in enumerate(streams):
        pr, hh = divmod(h, 2)
        c = c0 + ti
        valid = blk_ix < c
        tq = (qry_lane + c * blk).astype(F32)
        gate = jnp.where(valid, gates[si], -jnp.inf)
        chosen = blk_ix == c
        for _ in range(MOBA_TOPK):
            best = jnp.max(gate, axis=0, keepdims=True)
            first = jnp.min(jnp.where(gate == best, blk_f, float(nbp)), axis=0, keepdims=True)
            pick = blk_f == first
            chosen = chosen | (pick & valid)
            gate = jnp.where(pick, -jnp.inf, gate)
        bias = jnp.where(chosen, 0.0, MASK_VALUE)
        if nbp < dh:
            bias = jnp.concatenate([bias, jnp.zeros((dh - nbp, blk), F32)], axis=0)
        slope = slope_ref[pl.ds(heads * hg + h, 1), :] * LOG2E
        t1, t2, t3 = _split_bf16(-slope * tq, 3)
        aug = jnp.where(aug_row < nb, bias, 0.0)
        aug = jnp.where((aug_row >= nb) & (aug_row < nb + 3), 1.0, aug)
        aug = jnp.where(aug_row == nb + 3, t1.astype(F32), aug)
        aug = jnp.where(aug_row == nb + 4, t2.astype(F32), aug)
        aug = jnp.where(aug_row == nb + 5, t3.astype(F32), aug).astype(BF16)
        data = qt_ref[dh * h:dh * (h + 1), ti * blk:(ti + 1) * blk]
        qaug_ref[si] = jnp.concatenate([data, aug] if hh == 0 else [aug, data], axis=0)

    def scores_into(buf_ref, g):
        start = pl.multiple_of(g * (grp * blk), grp * blk)
        col_max = []
        for si, (ti, h) in enumerate(streams):
            s = _dot(kaug_ref[h, pl.ds(start, grp * blk), :], qaug_ref[si])
            buf_ref[si] = s
            col_max.append(jnp.max(s, axis=0, keepdims=True))
        return tuple(col_max)

    def consume(buf_ref, g, ms, col_max, own=False):
        new_ms = []
        for si, (ti, h) in enumerate(streams):
            s = buf_ref[si]
            if own:
                s = jnp.where(key_ix <= qry_ix + (c0 + ti) * blk, s, MASK_VALUE)
                group_max = jnp.max(s, axis=0, keepdims=True)
            else:
                group_max = col_max[si]
            m_new = jnp.maximum(ms[si], group_max)
            alpha = jnp.exp2(ms[si] - m_new)
            p = jnp.exp2(s - m_new)
            acc_ref[si] = alpha * acc_ref[si] + _dot(vaug_ref[h, g], p.astype(BF16))
            new_ms.append(m_new)
        return tuple(new_ms)

    def write_output():
        for ti in range(tiles):
            outs = []
            for h in range(heads):
                acc = acc_ref[ti * heads + h]
                outs.append(acc[0:dh, :] / acc[dh:dh + 1, :])
            o_ref[ti * blk:(ti + 1) * blk, :] = jnp.concatenate(outs, axis=0).T.astype(o_ref.dtype)

    acc_ref[...] = jnp.zeros_like(acc_ref)
    masked_score = jnp.full((1, blk), MASK_VALUE, F32).astype(BF16).astype(F32)
    max_a0 = scores_into(sa_ref, 0)

    def two_groups(i, carry):
        ms, max_a = carry
        max_b = scores_into(sb_ref, 2 * i + 1)
        ms = consume(sa_ref, 2 * i, ms, max_a)
        max_a = scores_into(sa_ref, 2 * i + 2)
        return consume(sb_ref, 2 * i + 1, ms, max_b), max_a

    ms, max_a = lax.fori_loop(0, own_grp // 2, two_groups, ((masked_score,) * len(streams), max_a0))

    @pl.when(own_grp % 2 == 0)
    def _own_group_in_a():
        consume(sa_ref, own_grp, ms, None, own=True)
        write_output()

    @pl.when(own_grp % 2 == 1)
    def _own_group_in_b():
        scores_into(sb_ref, own_grp)
        consume(sb_ref, own_grp, consume(sa_ref, own_grp - 1, ms, max_a), None, own=True)
        write_output()


def _moba(qat, ka, vat, bsz, seq):
    t, a_width = ka.shape
    blk = MOBA_BLOCK
    assert seq % blk == 0 and seq // blk >= MOBA_TOPK
    nb = seq // blk
    assert nb + 6 <= A_HEAD_DIM
    grp = MOBA_KEY_GROUP if nb % MOBA_KEY_GROUP == 0 else 1
    heads = MOBA_HEADS_PER_STEP
    width = heads * A_HEAD_DIM
    assert heads % 2 == 0 and a_width % width == 0
    tiles = grp
    n_streams = tiles * heads
    steps = nb // tiles
    slopes = jnp.exp2(-8.0 * jnp.arange(1, A_HEADS + 1, dtype=F32) / A_HEADS)
    slope_tbl = jnp.broadcast_to(slopes[:, None], (A_HEADS, blk))
    return pl.pallas_call(
        functools.partial(_moba_kernel, nb=nb, grp=grp, heads=heads, tiles=tiles),
        grid=(bsz, a_width // width, steps),
        in_specs=[
            pl.BlockSpec((A_HEADS, blk), lambda b, hg, c: (0, 0)),
            pl.BlockSpec((width, tiles * blk), lambda b, hg, c: (hg, b * steps + c)),
            pl.BlockSpec((seq, width), lambda b, hg, c: (b, hg), pipeline_mode=pl.Buffered(1)),
            pl.BlockSpec((width, seq), lambda b, hg, c: (hg, b), pipeline_mode=pl.Buffered(1)),
        ],
        out_specs=pl.BlockSpec((tiles * blk, width), lambda b, hg, c: (b * steps + c, hg)),
        out_shape=jax.ShapeDtypeStruct((t, a_width), BF16),
        scratch_shapes=[
            pltpu.VMEM((heads, seq, LANES), BF16),
            pltpu.VMEM((heads, nb // grp, V_AUG_ROWS, grp * blk), BF16),
            pltpu.VMEM((heads // 2, LANES, LANES), F32),
            pltpu.VMEM((n_streams, LANES, blk), BF16),
            pltpu.VMEM((n_streams, V_AUG_ROWS, blk), F32),
            pltpu.VMEM((n_streams, grp * blk, blk), F32),
            pltpu.VMEM((n_streams, grp * blk, blk), F32),
        ],
        compiler_params=_compiler_params(("parallel", "parallel", "arbitrary")),
        name="moba",
    )(slope_tbl, qat, ka, vat)


def _mlstm_kernel(qt_ref, k_ref, vt_ref, obt_ref, grow_ref, cwq_ref, cbq_ref, cwk_ref, cbk_ref,
                  o_ref, qtail_ref, kbuf_ref, ct_ref, m_ref):
    ci = pl.program_id(1)
    L, width = k_ref.shape
    d = B_HEAD_DIM
    halo = SUBLANES_F32

    @pl.when(ci == 0)
    def _():
        qtail_ref[...] = jnp.zeros_like(qtail_ref)
        kbuf_ref[0:halo, :] = jnp.zeros((halo, width), F32)
        ct_ref[...] = jnp.zeros_like(ct_ref)
        m_ref[...] = jnp.zeros_like(m_ref)

    xq = qt_ref[...].astype(F32)
    lane = lax.broadcasted_iota(jnp.int32, (width, LANES), 1)
    tail = qtail_ref[...]
    yq = cbq_ref[...] + xq * cwq_ref[CONV_WIDTH - 1]
    for s in range(1, CONV_WIDTH):
        shifted = pltpu.roll(xq, s, axis=1)
        first = jnp.where(lane < s, pltpu.roll(tail, s, axis=1), shifted[:, :LANES])
        shifted = jnp.concatenate([first, shifted[:, LANES:]], axis=1)
        yq = yq + shifted * cwq_ref[CONV_WIDTH - 1 - s]
    qtail_ref[...] = xq[:, L - LANES:]
    q_all = (yq * jax.nn.sigmoid(yq) * (d ** -0.5)).astype(BF16)

    kbuf_ref[halo:halo + L, :] = k_ref[...].astype(F32)
    yk = cbk_ref[...]
    for j in range(CONV_WIDTH):
        off = halo - (CONV_WIDTH - 1) + j
        yk = yk + kbuf_ref[off:off + L, :] * cwk_ref[j:j + 1, :]
    kbuf_ref[0:halo, :] = kbuf_ref[L:L + halo, :]
    k_all = (yk * jax.nn.sigmoid(yk)).astype(BF16)

    def log_sigmoid(z):
        return jnp.minimum(z, 0.0) - jnp.log1p(jnp.exp(-jnp.abs(z)))

    src = lax.broadcasted_iota(jnp.int32, (L, L), 0)
    dst = lax.broadcasted_iota(jnp.int32, (L, L), 1)
    causal = src <= dst
    tri_up = causal.astype(BF16)
    gates_row = grow_ref[0]
    gates_col = gates_row.T
    lf_hi, lf_lo = _split_bf16(log_sigmoid(gates_row), 2)
    b_rows = _dot(lf_hi, tri_up) + _dot(lf_lo, tri_up)
    b_cols = b_rows.T
    pad_rows = ct_ref.shape[1] - d
    ones_rows = jnp.where(lax.broadcasted_iota(jnp.int32, (pad_rows, L), 0) == 0, 1.0, 0.0).astype(BF16)
    for hd in range(B_HEADS):
        rows = slice(hd * d, (hd + 1) * d)
        qt = q_all[rows, :]
        k = k_all[:, rows]
        vt_aug = jnp.concatenate([vt_ref[rows, :], ones_rows], axis=0)
        i_col = gates_col[:, hd:hd + 1]
        i_row = gates_row[hd:hd + 1, :]
        b_col = b_cols[:, hd + B_HEADS:hd + B_HEADS + 1]
        b_row = b_rows[hd + B_HEADS:hd + B_HEADS + 1, :]

        m_prev = m_ref[hd]
        log_inter = b_row + m_prev
        dmat = jnp.where(causal, b_row + (i_col - b_col), -jnp.inf)
        m_t = jnp.maximum(log_inter, jnp.max(dmat, axis=0, keepdims=True))
        w_inter = jnp.exp(log_inter - m_t)
        sc = _dot(k, qt) * jnp.exp(dmat - m_t)
        ct_prev = ct_ref[hd]
        inter = w_inter * _dot(ct_prev.astype(BF16), qt)
        num = inter[:d, :] + _dot(vt_ref[rows, :], sc.astype(BF16))
        den = inter[d:d + 1, :] + jnp.sum(sc, axis=0, keepdims=True)
        hidden = num / jnp.maximum(jnp.abs(den), jnp.exp(-m_t))
        out_t = jax.nn.sigmoid(obt_ref[rows, :].astype(F32)) * hidden
        o_ref[:, rows] = out_t.T.astype(o_ref.dtype)

        b_last = b_row[:, L - 1:L]
        log_old = b_last + m_prev
        log_new = b_last - b_row + i_row
        m_new = jnp.maximum(log_old, jnp.max(log_new, axis=1, keepdims=True))
        a_old = jnp.exp(log_old - m_new)
        a_new = jnp.exp(log_new - m_new)
        ct_ref[hd] = a_old * ct_prev + _dot((vt_aug.astype(F32) * a_new).astype(BF16), k)
        m_ref[hd] = m_new


def _mlstm(qbt, kb, vbt, obt, gates, conv_w, conv_b, bsz, seq):
    b_width, t = qbt.shape
    d = B_HEAD_DIM
    L = min(MLSTM_KERNEL_CHUNK, seq)
    assert seq % L == 0 and L % LANES == 0 and d == LANES and b_width == B_HEADS * d
    nc = seq // L
    gates_row = gates.reshape(bsz, seq, 2 * B_HEADS).transpose(0, 2, 1)
    cwq = jnp.broadcast_to(conv_w[:, :b_width, None].astype(F32), (CONV_WIDTH, b_width, L))
    cbq = jnp.broadcast_to(conv_b[:b_width, None].astype(F32), (b_width, L))
    cwk = conv_w[:, b_width:].astype(F32)
    cbk = conv_b[b_width:].astype(F32).reshape(1, b_width)
    tposed = lambda b, c: (0, b * nc + c)
    natural = lambda b, c: (b * nc + c, 0)
    const2 = lambda b, c: (0, 0)
    return pl.pallas_call(
        _mlstm_kernel,
        grid=(bsz, nc),
        in_specs=[
            pl.BlockSpec((b_width, L), tposed),
            pl.BlockSpec((L, b_width), natural),
            pl.BlockSpec((b_width, L), tposed),
            pl.BlockSpec((b_width, L), tposed),
            pl.BlockSpec((1, 2 * B_HEADS, L), lambda b, c: (b, 0, c)),
            pl.BlockSpec((CONV_WIDTH, b_width, L), lambda b, c: (0, 0, 0)),
            pl.BlockSpec((b_width, L), const2),
            pl.BlockSpec((CONV_WIDTH, b_width), const2),
            pl.BlockSpec((1, b_width), const2),
        ],
        out_specs=pl.BlockSpec((L, b_width), natural),
        out_shape=jax.ShapeDtypeStruct((t, b_width), BF16),
        scratch_shapes=[
            pltpu.VMEM((b_width, LANES), F32),
            pltpu.VMEM((L + SUBLANES_F32, b_width), F32),
            pltpu.VMEM((B_HEADS, d + SUBLANES_BF16, d), F32),
            pltpu.VMEM((B_HEADS, 1, 1), F32),
        ],
        compiler_params=_compiler_params(("parallel", "arbitrary")),
        name="mlstm",
    )(qbt, kb, vbt, obt, gates_row, cwq, cbq, cwk, cbk)


def _pool_kernel(x_ref, halo_ref, g_ref, w_ref, scale_ref, o_ref, sums_ref, *, tiles_per_seq):
    i = pl.program_id(0)
    tm, d = x_ref.shape
    n_win = len(POOL_WINDOWS)
    grp = d // n_win
    base = POOL_PAD + POOL_HALO
    rows = POOL_HALO + tm
    x = x_ref[...]
    seq_tile = i % tiles_per_seq
    sums_ref[:, 0:POOL_PAD, :] = jnp.zeros((n_win, POOL_PAD, d), F32)
    halo_h = _rms_normalize(halo_ref[...], g_ref[...])
    sums_ref[0, POOL_PAD:base, :] = jnp.where(seq_tile == 0, 0.0, halo_h)
    h = _rms_normalize(x, g_ref[...])
    sums_ref[0, base:base + tm, :] = h
    t1 = (lax.broadcasted_iota(jnp.int32, (tm, 1), 0) + seq_tile * tm + 1).astype(F32)
    for k, win in enumerate(POOL_WINDOWS):
        half = win // 2
        c0 = k * grp
        if k + 1 < n_win:
            both = (sums_ref[k, POOL_PAD:POOL_PAD + rows, c0:]
                    + sums_ref[k, POOL_PAD - half:POOL_PAD - half + rows, c0:])
            sums_ref[k + 1, POOL_PAD:POOL_PAD + rows, c0:] = both
            total = both[POOL_HALO:, 0:grp]
        else:
            total = (sums_ref[k, base:base + tm, c0:c0 + grp]
                     + sums_ref[k, base - half:base - half + tm, c0:c0 + grp])
        pooled = total / jnp.minimum(t1, float(win)) - h[:, c0:c0 + grp]
        y = _dot(pooled.astype(BF16), w_ref[k])
        o_ref[:, c0:c0 + grp] = x[:, c0:c0 + grp] + y * scale_ref[:, c0:c0 + grp]


def _pool(x, gain, w_grp, scale, seq):
    t, d = x.shape
    tm = min(POOL_TOKEN_TILE, seq)
    assert seq % tm == 0 and tm % POOL_HALO == 0
    assert POOL_WINDOWS == tuple(2 ** (k + 1) for k in range(len(POOL_WINDOWS)))
    assert POOL_WINDOWS[-1] <= POOL_HALO and POOL_WINDOWS[-1] // 2 <= POOL_PAD
    n_grp, grp, _ = w_grp.shape
    halo_blocks = tm // POOL_HALO
    return pl.pallas_call(
        functools.partial(_pool_kernel, tiles_per_seq=seq // tm),
        grid=(t // tm,),
        in_specs=[
            pl.BlockSpec((tm, d), lambda i: (i, 0)),
            pl.BlockSpec((POOL_HALO, d), lambda i: (jnp.maximum(i * halo_blocks - 1, 0), 0)),
            pl.BlockSpec((1, d), lambda i: (0, 0)),
            pl.BlockSpec((n_grp, grp, grp), lambda i: (0, 0, 0)),
            pl.BlockSpec((1, d), lambda i: (0, 0)),
        ],
        out_specs=pl.BlockSpec((tm, d), lambda i: (i, 0)),
        out_shape=jax.ShapeDtypeStruct((t, d), F32),
        scratch_shapes=[pltpu.VMEM((len(POOL_WINDOWS), POOL_PAD + POOL_HALO + tm, d), F32)],
        compiler_params=_compiler_params(("parallel",)),
        name="pool",
    )(x, x, gain.reshape(1, d), w_grp.astype(BF16), scale.astype(F32).reshape(1, d))


def _mixer_heads(x, gain, w_in, g_q, g_k, conv_w, conv_b, b_i, b_f, bsz, seq):
    qat, ka, vat, kb, qbt, vbt, obt, gates = _inproj(x, gain, w_in, g_q, g_k, b_i, b_f)
    ya = _moba(qat, ka, vat, bsz, seq)
    yb = _mlstm(qbt, kb, vbt, obt, gates, conv_w, conv_b, bsz, seq)
    return ya, yb


def kernel(x, norm_g, ffn_w_gate, ffn_w_up, ffn_w_down, ab_w_in, ab_w_out, ab_g_q, ab_g_k,
           ab_conv_w, ab_conv_b, ab_b_i, ab_b_f, pool_w, pool_scale):
    bsz, seq, d = x.shape
    depth = norm_g.shape[0]
    y = x.reshape(bsz * seq, d)
    w_gate, w_up, w_down = (w.astype(BF16) for w in (ffn_w_gate, ffn_w_up, ffn_w_down))
    for layer in range(depth):
        y = _ffn(y, norm_g[layer, 0], w_gate, w_up, w_down, (layer, 0))
        mixer_out = None
        if layer % 2 == 0:
            e = layer // 2
            ya, yb = _mixer_heads(y, norm_g[layer, 1], ab_w_in[e], ab_g_q[e], ab_g_k[e],
                                  ab_conv_w[e], ab_conv_b[e], ab_b_i[e], ab_b_f[e], bsz, seq)
            mixer_out = (ya, yb, ab_w_out[e])
        else:
            o = layer // 2
            y = _pool(y, norm_g[layer, 1], pool_w[o], pool_scale[o], seq)
        y = _ffn(y, norm_g[layer, 2], w_gate, w_up, w_down, (layer, 1), mixer_out=mixer_out)
    return y.reshape(bsz, seq, d)
```

```python
import functools

import jax
import jax.numpy as jnp
from jax import lax
from jax.experimental import pallas as pl
from jax.experimental.pallas import tpu as pltpu

F32 = jnp.float32
BF16 = jnp.bfloat16

LANES = 128
SUBLANES_F32 = 8
SUBLANES_BF16 = 16
MXU_WIDTH = 256
VMEM_BYTES = 64 * 1024 * 1024
VMEM_LIMIT_BYTES = VMEM_BYTES * 7 // 8

RMS_EPS = 1e-6
A_HEADS = 8
A_HEAD_DIM = 64
MOBA_BLOCK = 256
MOBA_TOPK = 3
B_HEADS = 4
B_HEAD_DIM = 128
CONV_WIDTH = 4
POOL_WINDOWS = (2, 4, 8, 16)

MOBA_KEY_GROUP = 2
MOBA_HEADS_PER_STEP = 4
MLSTM_KERNEL_CHUNK = 256
POOL_HALO = 16
POOL_PAD = SUBLANES_F32
MASK_VALUE = -1e30
LOG2E = 1.4426950408889634
V_AUG_ROWS = A_HEAD_DIM + SUBLANES_BF16
FFN_TOKEN_TILE = 1024
FFN_SUB_TILE = 512
FFN_HIDDEN_TILE = MXU_WIDTH
PROJ_TOKEN_TILE = 1024
POOL_TOKEN_TILE = 1024


def _compiler_params(semantics):
    return pltpu.CompilerParams(dimension_semantics=semantics,
                                vmem_limit_bytes=VMEM_LIMIT_BYTES)


def _rms_normalize(x, gain):
    ms = jnp.mean(x * x, axis=-1, keepdims=True)
    return x * lax.rsqrt(ms + RMS_EPS) * gain


def _dot(a, b):
    return jnp.dot(a, b, preferred_element_type=F32)


def _dot_nt(a, b):
    return lax.dot_general(a, b, (((1,), (1,)), ((), ())), preferred_element_type=F32)


def _dot_tn(a, b):
    return lax.dot_general(a, b, (((0,), (0,)), ((), ())), preferred_element_type=F32)


def _split_bf16(x, parts):
    out = []
    rem = x
    for _ in range(parts):
        p = rem.astype(BF16)
        out.append(p)
        rem = rem - p.astype(F32)
    return out


def _ffn_kernel(x_ref, g_ref, wg_ref, wu_ref, wd_ref, *rest, hidden_tile, sub_tile):
    o_ref = rest[-1]
    d_ff = wg_ref.shape[1]
    for r0 in range(0, x_ref.shape[0], sub_tile):
        x = x_ref[r0:r0 + sub_tile, :]
        if len(rest) > 1:
            ya_ref, yb_ref, wa_ref, wb_ref = rest[:-1]
            x = (x + _dot(ya_ref[r0:r0 + sub_tile, :], wa_ref[...])
                 + _dot(yb_ref[r0:r0 + sub_tile, :], wb_ref[...]))
        h = _rms_normalize(x, g_ref[...]).astype(BF16)
        acc = None
        for c0 in range(0, d_ff, hidden_tile):
            gate = _dot(h, wg_ref[:, c0:c0 + hidden_tile])
            up = _dot(h, wu_ref[:, c0:c0 + hidden_tile])
            act = (gate * jax.nn.sigmoid(gate) * up).astype(BF16)
            part = _dot(act, wd_ref[c0:c0 + hidden_tile, :])
            acc = part if acc is None else acc + part
        o_ref[r0:r0 + sub_tile, :] = x + 0.5 * acc


def _resident(shape):
    return pl.BlockSpec(shape, lambda *_: (0,) * len(shape), pipeline_mode=pl.Buffered(1))


def _ffn(x, gain, w_gate, w_up, w_down, which, mixer_out=None):
    t, d = x.shape
    d_ff = w_gate.shape[-1]
    tm, tf = FFN_TOKEN_TILE, FFN_HIDDEN_TILE
    assert t % tm == 0 and d_ff % tf == 0 and tm % FFN_SUB_TILE == 0
    row = lambda i: (i, 0)
    picked = lambda r, c: pl.BlockSpec((None, None, r, c), lambda i: (*which, 0, 0),
                                       pipeline_mode=pl.Buffered(1))
    in_specs = [pl.BlockSpec((tm, d), row), _resident((1, d)), picked(d, d_ff), picked(d, d_ff),
                picked(d_ff, d)]
    args = [x, gain.reshape(1, d), w_gate, w_up, w_down]
    if mixer_out is not None:
        ya, yb, w_out = mixer_out
        wa, wb = ya.shape[1], yb.shape[1]
        in_specs += [pl.BlockSpec((tm, wa), row), pl.BlockSpec((tm, wb), row),
                     _resident((wa, d)), _resident((wb, d))]
        args += [ya, yb, w_out[:wa].astype(BF16), w_out[wa:].astype(BF16)]
    return pl.pallas_call(
        functools.partial(_ffn_kernel, hidden_tile=tf, sub_tile=FFN_SUB_TILE),
        grid=(t // tm,),
        in_specs=in_specs,
        out_specs=pl.BlockSpec((tm, d), row),
        out_shape=jax.ShapeDtypeStruct((t, d), F32),
        compiler_params=_compiler_params(("parallel",)),
        name="ffn",
    )(*args)


def _inproj_kernel(x_ref, g_ref, w_ref, wqt_ref, wvt_ref, wbt_ref, wgate_ref, gqcol_ref, gk_ref, grp_ref,
                   gbias_ref, qat_ref, ka_ref, vat_ref, kb_ref, qbt_ref, vbt_ref, obt_ref, gates_ref,
                   *, a_width, b_width):
    h = _rms_normalize(x_ref[...], g_ref[...]).astype(BF16)

    qt = _dot_nt(wqt_ref[...], h)
    for hd in range(a_width // A_HEAD_DIM):
        rows = slice(hd * A_HEAD_DIM, (hd + 1) * A_HEAD_DIM)
        y = qt[rows, :]
        ms = jnp.mean(y * y, axis=0, keepdims=True)
        gain = gqcol_ref[rows, :] * (A_HEAD_DIM ** -0.5 * LOG2E)
        qat_ref[rows, :] = (y * lax.rsqrt(ms + RMS_EPS) * gain).astype(BF16)

    y = _dot(h, w_ref[:, 0:a_width])
    sq_hi, sq_lo = _split_bf16(y * y, 2)
    ssq = _dot(sq_hi, grp_ref[...]) + _dot(sq_lo, grp_ref[...])
    ka_ref[...] = (y * lax.rsqrt(ssq * (1.0 / A_HEAD_DIM) + RMS_EPS) * gk_ref[...]).astype(BF16)

    vat_ref[...] = _dot_nt(wvt_ref[...], h).astype(BF16)
    kb_ref[...] = _dot(h, w_ref[:, a_width:a_width + b_width]).astype(BF16)
    for piece, out_ref in enumerate((qbt_ref, vbt_ref, obt_ref)):
        out_ref[...] = _dot_nt(wbt_ref[piece * b_width:(piece + 1) * b_width, :], h).astype(BF16)
    gates = _dot(h, wgate_ref[...]) + gbias_ref[...]
    gates_ref[...] = gates[:, :2 * B_HEADS]


def _inproj(x, gain, w_in, g_q, g_k, b_i, b_f):
    t, d = x.shape
    a_width = A_HEADS * A_HEAD_DIM
    b_width = B_HEADS * B_HEAD_DIM
    n_main = a_width + b_width
    o_qb = 3 * a_width
    o_ib = 3 * a_width + 3 * b_width
    tm = PROJ_TOKEN_TILE
    assert t % tm == 0
    w_main = jnp.concatenate([w_in[:, a_width:2 * a_width], w_in[:, o_qb + b_width:o_qb + 2 * b_width]],
                             axis=1).astype(BF16)
    w_qt = w_in[:, :a_width].T.astype(BF16)
    w_vt = w_in[:, 2 * a_width:3 * a_width].T.astype(BF16)
    w_bt = jnp.concatenate([w_in[:, o_qb:o_qb + b_width], w_in[:, o_qb + 2 * b_width:o_ib],
                            w_in[:, o_ib + 2 * B_HEADS:]], axis=1).T.astype(BF16)
    w_gate = jnp.pad(w_in[:, o_ib:o_ib + 2 * B_HEADS], ((0, 0), (0, LANES - 2 * B_HEADS))).astype(BF16)
    gate_bias = jnp.pad(jnp.concatenate([b_i, b_f]).astype(F32), (0, LANES - 2 * B_HEADS)).reshape(1, LANES)
    head_id = jnp.arange(a_width) // A_HEAD_DIM
    grp = (head_id[:, None] == head_id[None, :]).astype(BF16)
    gq_col = jnp.tile(g_q.astype(F32), A_HEADS).reshape(a_width, 1)
    gk = jnp.tile(g_k.astype(F32), A_HEADS).reshape(1, a_width)
    const = lambda i: (0, 0)
    row = lambda i: (i, 0)
    col = lambda i: (0, i)
    outs = pl.pallas_call(
        functools.partial(_inproj_kernel, a_width=a_width, b_width=b_width),
        grid=(t // tm,),
        in_specs=[
            pl.BlockSpec((tm, d), row),
            pl.BlockSpec((1, d), const),
            pl.BlockSpec((d, n_main), const),
            pl.BlockSpec((a_width, d), const),
            pl.BlockSpec((a_width, d), const),
            pl.BlockSpec((3 * b_width, d), const),
            pl.BlockSpec((d, LANES), const),
            pl.BlockSpec((a_width, 1), const),
            pl.BlockSpec((1, a_width), const),
            pl.BlockSpec((a_width, a_width), const),
            pl.BlockSpec((1, LANES), const),
        ],
        out_specs=[
            pl.BlockSpec((a_width, tm), col),
            pl.BlockSpec((tm, a_width), row),
            pl.BlockSpec((a_width, tm), col),
            pl.BlockSpec((tm, b_width), row),
            pl.BlockSpec((b_width, tm), col),
            pl.BlockSpec((b_width, tm), col),
            pl.BlockSpec((b_width, tm), col),
            pl.BlockSpec((tm, 2 * B_HEADS), row),
        ],
        out_shape=[
            jax.ShapeDtypeStruct((a_width, t), BF16),
            jax.ShapeDtypeStruct((t, a_width), BF16),
            jax.ShapeDtypeStruct((a_width, t), BF16),
            jax.ShapeDtypeStruct((t, b_width), BF16),
            jax.ShapeDtypeStruct((b_width, t), BF16),
            jax.ShapeDtypeStruct((b_width, t), BF16),
            jax.ShapeDtypeStruct((b_width, t), BF16),
            jax.ShapeDtypeStruct((t, 2 * B_HEADS), F32),
        ],
        compiler_params=_compiler_params(("parallel",)),
        name="inproj",
    )(x, gain.reshape(1, d), w_main, w_qt, w_vt, w_bt, w_gate, gq_col, gk, grp, gate_bias)
    return outs


def _moba_kernel(slope_ref, qt_ref, k_ref, vt_ref, o_ref, kaug_ref, vaug_ref, kmean_ref, qaug_ref, acc_ref,
                 sa_ref, sb_ref, *, nb, grp, heads, tiles):
    hg = pl.program_id(0)
    batch = pl.program_id(1)
    c0 = pl.program_id(2) * tiles
    blk = MOBA_BLOCK
    dh = A_HEAD_DIM
    nbp = pl.cdiv(nb, SUBLANES_BF16) * SUBLANES_BF16
    streams = [(ti, h) for ti in range(tiles) for h in range(heads)]

    @pl.when(c0 == 0)
    def _build_key_value_side():
        ones_rows = jnp.where(lax.broadcasted_iota(jnp.int32, (V_AUG_ROWS - dh, blk), 0) == 0,
                              1.0, 0.0).astype(BF16)
        for h in range(heads):
            for j in range(nb):
                g, off = divmod(j, grp)
                vaug_ref[h, g, 0:dh, off * blk:(off + 1) * blk] = vt_ref[dh * h:dh * (h + 1),
                                                                         j * blk:(j + 1) * blk]
                vaug_ref[h, g, dh:V_AUG_ROWS, off * blk:(off + 1) * blk] = ones_rows
        kmean_ref[...] = jnp.zeros_like(kmean_ref)
        lane = lax.broadcasted_iota(jnp.int32, (blk, LANES), 1)
        row = lax.broadcasted_iota(jnp.int32, (blk, LANES), 0)

        def key_block(j, carry, *, with_extra_terms):
            start = pl.multiple_of(j * blk, blk)
            pos = (row + j * blk).astype(F32)
            for pr in range(heads // 2):
                kb = k_ref[pl.ds(start, blk), pr * LANES:(pr + 1) * LANES]
                kmean_ref[pr, pl.ds(j, 1), :] = jnp.mean(kb.astype(F32), axis=0, keepdims=True)
                for hh in range(2):
                    h = 2 * pr + hh
                    is_data = (lane >= dh * hh) & (lane < dh * (hh + 1))
                    if with_extra_terms:
                        slope = slope_ref[pl.ds(heads * hg + h, 1), :][:, 0:LANES] * LOG2E
                        p1, p2, p3 = _split_bf16(slope * pos, 3)
                        rel = lane - dh * (1 - hh)
                        aug = jnp.where(rel == j, 1.0, 0.0)
                        aug = jnp.where(rel == nb, p1.astype(F32), aug)
                        aug = jnp.where(rel == nb + 1, p2.astype(F32), aug)
                        aug = jnp.where(rel == nb + 2, p3.astype(F32), aug)
                        aug = jnp.where((rel >= nb + 3) & (rel < nb + 6), 1.0, aug).astype(BF16)
                    else:
                        aug = kaug_ref[h, pl.ds(start, blk), :]
                    kaug_ref[h, pl.ds(start, blk), :] = jnp.where(is_data, kb, aug)
            return carry

        @pl.when(batch == 0)
        def _():
            lax.fori_loop(0, nb, functools.partial(key_block, with_extra_terms=True), 0)

        @pl.when(batch != 0)
        def _():
            lax.fori_loop(0, nb, functools.partial(key_block, with_extra_terms=False), 0)

    blk_ix = lax.broadcasted_iota(jnp.int32, (nbp, blk), 0)
    blk_f = blk_ix.astype(F32)
    aug_row = lax.broadcasted_iota(jnp.int32, (dh, blk), 0)
    qry_lane = lax.broadcasted_iota(jnp.int32, (dh, blk), 1)
    lane_k = lax.broadcasted_iota(jnp.int32, (nbp, LANES), 1)
    own_grp = c0 // grp
    key_ix = lax.broadcasted_iota(jnp.int32, (grp * blk, blk), 0) + own_grp * (grp * blk)
    qry_ix = lax.broadcasted_iota(jnp.int32, (grp * blk, blk), 1)
    gates = []
    for ti, h in streams:
        pr, hh = divmod(h, 2)
        qt_pair = qt_ref[pr * LANES:(pr + 1) * LANES, ti * blk:(ti + 1) * blk]
        is_data_k = (lane_k >= dh * hh) & (lane_k < dh * (hh + 1))
        km_hi, km_lo = _split_bf16(jnp.where(is_data_k, kmean_ref[pr, 0:nbp, :], 0.0), 2)
        gates.append(_dot(km_hi, qt_pair) + _dot(km_lo, qt_pair))
    for si, (ti, h) in enumerate(streams):
        pr, hh = divmod(h, 2)
        c = c0 + ti
        valid = blk_ix < c
        tq = (qry_lane + c * blk).astype(F32)
        gate = jnp.where(valid, gates[si], -jnp.inf)
        chosen = blk_ix == c
        for _ in range(MOBA_TOPK):
            best = jnp.max(gate, axis=0, keepdims=True)
            first = jnp.min(jnp.where(gate == best, blk_f, float(nbp)), axis=0, keepdims=True)
            pick = blk_f == first
            chosen = chosen | (pick & valid)
            gate = jnp.where(pick, -jnp.inf, gate)
        bias = jnp.where(chosen, 0.0, MASK_VALUE)
        if nbp < dh:
            bias = jnp.concatenate([bias, jnp.zeros((dh - nbp, blk), F32)], axis=0)
        slope = slope_ref[pl.ds(heads * hg + h, 1), :] * LOG2E
        t1, t2, t3 = _split_bf16(-slope * tq, 3)
        aug = jnp.where(aug_row < nb, bias, 0.0)
        aug = jnp.where((aug_row >= nb) & (aug_row < nb + 3), 1.0, aug)
        aug = jnp.where(aug_row == nb + 3, t1.astype(F32), aug)
        aug = jnp.where(aug_row == nb + 4, t2.astype(F32), aug)
        aug = jnp.where(aug_row == nb + 5, t3.astype(F32), aug).astype(BF16)
        data = qt_ref[dh * h:dh * (h + 1), ti * blk:(ti + 1) * blk]
        qaug_ref[si] = jnp.concatenate([data, aug] if hh == 0 else [aug, data], axis=0)

    def scores_into(buf_ref, g):
        start = pl.multiple_of(g * (grp * blk), grp * blk)
        col_max = []
        for si, (ti, h) in enumerate(streams):
            s = _dot(kaug_ref[h, pl.ds(start, grp * blk), :], qaug_ref[si])
            buf_ref[si] = s
            col_max.append(jnp.max(s, axis=0, keepdims=True))
        return tuple(col_max)

    def consume(buf_ref, g, ms, col_max, own=False):
        new_ms = []
        for si, (ti, h) in enumerate(streams):
            s = buf_ref[si]
            if own:
                s = jnp.where(key_ix <= qry_ix + (c0 + ti) * blk, s, MASK_VALUE)
                group_max = jnp.max(s, axis=0, keepdims=True)
            else:
                group_max = col_max[si]
            m_new = jnp.maximum(ms[si], group_max)
            alpha = jnp.exp2(ms[si] - m_new)
            p = jnp.exp2(s - m_new)
            acc_ref[si] = alpha * acc_ref[si] + _dot(vaug_ref[h, g], p.astype(BF16))
            new_ms.append(m_new)
        return tuple(new_ms)

    def write_output():
        for ti in range(tiles):
            outs = []
            for h in range(heads):
                acc = acc_ref[ti * heads + h]
                outs.append(acc[0:dh, :] / acc[dh:dh + 1, :])
            o_ref[ti * blk:(ti + 1) * blk, :] = jnp.concatenate(outs, axis=0).T.astype(o_ref.dtype)

    acc_ref[...] = jnp.zeros_like(acc_ref)
    masked_score = jnp.full((1, blk), MASK_VALUE, F32).astype(BF16).astype(F32)
    max_a0 = scores_into(sa_ref, 0)

    def two_groups(i, carry):
        ms, max_a = carry
        max_b = scores_into(sb_ref, 2 * i + 1)
        ms = consume(sa_ref, 2 * i, ms, max_a)
        max_a = scores_into(sa_ref, 2 * i + 2)
        return consume(sb_ref, 2 * i + 1, ms, max_b), max_a

    ms, max_a = lax.fori_loop(0, own_grp // 2, two_groups, ((masked_score,) * len(streams), max_a0))

    @pl.when(own_grp % 2 == 0)
    def _own_group_in_a():
        consume(sa_ref, own_grp, ms, None, own=True)
        write_output()

    @pl.when(own_grp % 2 == 1)
    def _own_group_in_b():
        scores_into(sb_ref, own_grp)
        consume(sb_ref, own_grp, consume(sa_ref, own_grp - 1, ms, max_a), None, own=True)
        write_output()


def _moba(qat, ka, vat, bsz, seq):
    t, a_width = ka.shape
    blk = MOBA_BLOCK
    assert seq % blk == 0 and seq // blk >= MOBA_TOPK
    nb = seq // blk
    assert nb + 6 <= A_HEAD_DIM
    grp = MOBA_KEY_GROUP if nb % MOBA_KEY_GROUP == 0 else 1
    heads = MOBA_HEADS_PER_STEP
    width = heads * A_HEAD_DIM
    assert heads % 2 == 0 and a_width % width == 0
    tiles = grp
    n_streams = tiles * heads
    steps = nb // tiles
    slopes = jnp.exp2(-8.0 * jnp.arange(1, A_HEADS + 1, dtype=F32) / A_HEADS)
    slope_tbl = jnp.broadcast_to(slopes[:, None], (A_HEADS, blk))
    return pl.pallas_call(
        functools.partial(_moba_kernel, nb=nb, grp=grp, heads=heads, tiles=tiles),
        grid=(a_width // width, bsz, steps),
        in_specs=[
            pl.BlockSpec((A_HEADS, blk), lambda hg, b, c: (0, 0)),
            pl.BlockSpec((width, tiles * blk), lambda hg, b, c: (hg, b * steps + c)),
            pl.BlockSpec((seq, width), lambda hg, b, c: (b, hg), pipeline_mode=pl.Buffered(1)),
            pl.BlockSpec((width, seq), lambda hg, b, c: (hg, b), pipeline_mode=pl.Buffered(1)),
        ],
        out_specs=pl.BlockSpec((tiles * blk, width), lambda hg, b, c: (b * steps + c, hg)),
        out_shape=jax.ShapeDtypeStruct((t, a_width), BF16),
        scratch_shapes=[
            pltpu.VMEM((heads, seq, LANES), BF16),
            pltpu.VMEM((heads, nb // grp, V_AUG_ROWS, grp * blk), BF16),
            pltpu.VMEM((heads // 2, LANES, LANES), F32),
            pltpu.VMEM((n_streams, LANES, blk), BF16),
            pltpu.VMEM((n_streams, V_AUG_ROWS, blk), F32),
            pltpu.VMEM((n_streams, grp * blk, blk), F32),
            pltpu.VMEM((n_streams, grp * blk, blk), F32),
        ],
        compiler_params=_compiler_params(("parallel", "arbitrary", "arbitrary")),
        name="moba",
    )(slope_tbl, qat, ka, vat)


def _mlstm_kernel(qt_ref, k_ref, vt_ref, obt_ref, grow_ref, cwq_ref, cbq_ref, cwk_ref, cbk_ref,
                  o_ref, qtail_ref, kbuf_ref, ct_ref, m_ref):
    ci = pl.program_id(1)
    L, width = k_ref.shape
    d = B_HEAD_DIM
    halo = SUBLANES_F32

    @pl.when(ci == 0)
    def _():
        qtail_ref[...] = jnp.zeros_like(qtail_ref)
        kbuf_ref[0:halo, :] = jnp.zeros((halo, width), F32)
        ct_ref[...] = jnp.zeros_like(ct_ref)
        m_ref[...] = jnp.zeros_like(m_ref)

    xq = qt_ref[...].astype(F32)
    lane = lax.broadcasted_iota(jnp.int32, (width, LANES), 1)
    tail = qtail_ref[...]
    yq = cbq_ref[...] + xq * cwq_ref[CONV_WIDTH - 1]
    for s in range(1, CONV_WIDTH):
        shifted = pltpu.roll(xq, s, axis=1)
        first = jnp.where(lane < s, pltpu.roll(tail, s, axis=1), shifted[:, :LANES])
        shifted = jnp.concatenate([first, shifted[:, LANES:]], axis=1)
        yq = yq + shifted * cwq_ref[CONV_WIDTH - 1 - s]
    qtail_ref[...] = xq[:, L - LANES:]
    q_all = (yq * jax.nn.sigmoid(yq) * (d ** -0.5)).astype(BF16)

    kbuf_ref[halo:halo + L, :] = k_ref[...].astype(F32)
    yk = cbk_ref[...]
    for j in range(CONV_WIDTH):
        off = halo - (CONV_WIDTH - 1) + j
        yk = yk + kbuf_ref[off:off + L, :] * cwk_ref[j:j + 1, :]
    kbuf_ref[0:halo, :] = kbuf_ref[L:L + halo, :]
    k_all = (yk * jax.nn.sigmoid(yk)).astype(BF16)

    def log_sigmoid(z):
        return jnp.minimum(z, 0.0) - jnp.log1p(jnp.exp(-jnp.abs(z)))

    src = lax.broadcasted_iota(jnp.int32, (L, L), 0)
    dst = lax.broadcasted_iota(jnp.int32, (L, L), 1)
    causal = src <= dst
    tri_up = causal.astype(BF16)
    gates_row = grow_ref[0]
    gates_col = gates_row.T
    lf_hi, lf_lo = _split_bf16(log_sigmoid(gates_row), 2)
    b_rows = _dot(lf_hi, tri_up) + _dot(lf_lo, tri_up)
    b_cols = b_rows.T
    pad_rows = ct_ref.shape[1] - d
    ones_rows = jnp.where(lax.broadcasted_iota(jnp.int32, (pad_rows, L), 0) == 0, 1.0, 0.0).astype(BF16)
    for hd in range(B_HEADS):
        rows = slice(hd * d, (hd + 1) * d)
        qt = q_all[rows, :]
        k = k_all[:, rows]
        vt_aug = jnp.concatenate([vt_ref[rows, :], ones_rows], axis=0)
        i_col = gates_col[:, hd:hd + 1]
        i_row = gates_row[hd:hd + 1, :]
        b_col = b_cols[:, hd + B_HEADS:hd + B_HEADS + 1]
        b_row = b_rows[hd + B_HEADS:hd + B_HEADS + 1, :]

        m_prev = m_ref[hd]
        log_inter = b_row + m_prev
        dmat = jnp.where(causal, b_row + (i_col - b_col), -jnp.inf)
        m_t = jnp.maximum(log_inter, jnp.max(dmat, axis=0, keepdims=True))
        w_inter = jnp.exp(log_inter - m_t)
        sc = _dot(k, qt) * jnp.exp(dmat - m_t)
        ct_prev = ct_ref[hd]
        inter = w_inter * _dot(ct_prev.astype(BF16), qt)
        num = inter[:d, :] + _dot(vt_ref[rows, :], sc.astype(BF16))
        den = inter[d:d + 1, :] + jnp.sum(sc, axis=0, keepdims=True)
        hidden = num / jnp.maximum(jnp.abs(den), jnp.exp(-m_t))
        out_t = jax.nn.sigmoid(obt_ref[rows, :].astype(F32)) * hidden
        o_ref[:, rows] = out_t.T.astype(o_ref.dtype)

        b_last = b_row[:, L - 1:L]
        log_old = b_last + m_prev
        log_new = b_last - b_row + i_row
        m_new = jnp.maximum(log_old, jnp.max(log_new, axis=1, keepdims=True))
        a_old = jnp.exp(log_old - m_new)
        a_new = jnp.exp(log_new - m_new)
        ct_ref[hd] = a_old * ct_prev + _dot((vt_aug.astype(F32) * a_new).astype(BF16), k)
        m_ref[hd] = m_new


def _mlstm(qbt, kb, vbt, obt, gates, conv_w, conv_b, bsz, seq):
    b_width, t = qbt.shape
    d = B_HEAD_DIM
    L = min(MLSTM_KERNEL_CHUNK, seq)
    assert seq % L == 0 and L % LANES == 0 and d == LANES and b_width == B_HEADS * d
    nc = seq // L
    gates_row = gates.reshape(bsz, seq, 2 * B_HEADS).transpose(0, 2, 1)
    cwq = jnp.broadcast_to(conv_w[:, :b_width, None].astype(F32), (CONV_WIDTH, b_width, L))
    cbq = jnp.broadcast_to(conv_b[:b_width, None].astype(F32), (b_width, L))
    cwk = conv_w[:, b_width:].astype(F32)
    cbk = conv_b[b_width:].astype(F32).reshape(1, b_width)
    tposed = lambda b, c: (0, b * nc + c)
    natural = lambda b, c: (b * nc + c, 0)
    const2 = lambda b, c: (0, 0)
    return pl.pallas_call(
        _mlstm_kernel,
        grid=(bsz, nc),
        in_specs=[
            pl.BlockSpec((b_width, L), tposed),
            pl.BlockSpec((L, b_width), natural),
            pl.BlockSpec((b_width, L), tposed),
            pl.BlockSpec((b_width, L), tposed),
            pl.BlockSpec((1, 2 * B_HEADS, L), lambda b, c: (b, 0, c)),
            pl.BlockSpec((CONV_WIDTH, b_width, L), lambda b, c: (0, 0, 0)),
            pl.BlockSpec((b_width, L), const2),
            pl.BlockSpec((CONV_WIDTH, b_width), const2),
            pl.BlockSpec((1, b_width), const2),
        ],
        out_specs=pl.BlockSpec((L, b_width), natural),
        out_shape=jax.ShapeDtypeStruct((t, b_width), BF16),
        scratch_shapes=[
            pltpu.VMEM((b_width, LANES), F32),
            pltpu.VMEM((L + SUBLANES_F32, b_width), F32),
            pltpu.VMEM((B_HEADS, d + SUBLANES_BF16, d), F32),
            pltpu.VMEM((B_HEADS, 1, 1), F32),
        ],
        compiler_params=_compiler_params(("parallel", "arbitrary")),
        name="mlstm",
    )(qbt, kb, vbt, obt, gates_row, cwq, cbq, cwk, cbk)


def _pool_kernel(x_ref, halo_ref, g_ref, w_ref, scale_ref, o_ref, sums_ref, *, tiles_per_seq):
    i = pl.program_id(0)
    tm, d = x_ref.shape
    n_win = len(POOL_WINDOWS)
    grp = d // n_win
    base = POOL_PAD + POOL_HALO
    rows = POOL_HALO + tm
    x = x_ref[...]
    seq_tile = i % tiles_per_seq
    sums_ref[:, 0:POOL_PAD, :] = jnp.zeros((n_win, POOL_PAD, d), F32)
    halo_h = _rms_normalize(halo_ref[...], g_ref[...])
    sums_ref[0, POOL_PAD:base, :] = jnp.where(seq_tile == 0, 0.0, halo_h)
    h = _rms_normalize(x, g_ref[...])
    sums_ref[0, base:base + tm, :] = h
    t1 = (lax.broadcasted_iota(jnp.int32, (tm, 1), 0) + seq_tile * tm + 1).astype(F32)
    for k, win in enumerate(POOL_WINDOWS):
        half = win // 2
        c0 = k * grp
        if k + 1 < n_win:
            both = (sums_ref[k, POOL_PAD:POOL_PAD + rows, c0:]
                    + sums_ref[k, POOL_PAD - half:POOL_PAD - half + rows, c0:])
            sums_ref[k + 1, POOL_PAD:POOL_PAD + rows, c0:] = both
            total = both[POOL_HALO:, 0:grp]
        else:
            total = (sums_ref[k, base:base + tm, c0:c0 + grp]
                     + sums_ref[k, base - half:base - half + tm, c0:c0 + grp])
        pooled = total / jnp.minimum(t1, float(win)) - h[:, c0:c0 + grp]
        y = _dot(pooled.astype(BF16), w_ref[k])
        o_ref[:, c0:c0 + grp] = x[:, c0:c0 + grp] + y * scale_ref[:, c0:c0 + grp]


def _pool(x, gain, w_grp, scale, seq):
    t, d = x.shape
    tm = min(POOL_TOKEN_TILE, seq)
    assert seq % tm == 0 and tm % POOL_HALO == 0
    assert POOL_WINDOWS == tuple(2 ** (k + 1) for k in range(len(POOL_WINDOWS)))
    assert POOL_WINDOWS[-1] <= POOL_HALO and POOL_WINDOWS[-1] // 2 <= POOL_PAD
    n_grp, grp, _ = w_grp.shape
    halo_blocks = tm // POOL_HALO
    return pl.pallas_call(
        functools.partial(_pool_kernel, tiles_per_seq=seq // tm),
        grid=(t // tm,),
        in_specs=[
            pl.BlockSpec((tm, d), lambda i: (i, 0)),
            pl.BlockSpec((POOL_HALO, d), lambda i: (jnp.maximum(i * halo_blocks - 1, 0), 0)),
            pl.BlockSpec((1, d), lambda i: (0, 0)),
            pl.BlockSpec((n_grp, grp, grp), lambda i: (0, 0, 0)),
            pl.BlockSpec((1, d), lambda i: (0, 0)),
        ],
        out_specs=pl.BlockSpec((tm, d), lambda i: (i, 0)),
        out_shape=jax.ShapeDtypeStruct((t, d), F32),
        scratch_shapes=[pltpu.VMEM((len(POOL_WINDOWS), POOL_PAD + POOL_HALO + tm, d), F32)],
        compiler_params=_compiler_params(("parallel",)),
        name="pool",
    )(x, x, gain.reshape(1, d), w_grp.astype(BF16), scale.astype(F32).reshape(1, d))


def _mixer_heads(x, gain, w_in, g_q, g_k, conv_w, conv_b, b_i, b_f, bsz, seq):
    qat, ka, vat, kb, qbt, vbt, obt, gates = _inproj(x, gain, w_in, g_q, g_k, b_i, b_f)
    ya = _moba(qat, ka, vat, bsz, seq)
    yb = _mlstm(qbt, kb, vbt, obt, gates, conv_w, conv_b, bsz, seq)
    return ya, yb


def kernel(x, norm_g, ffn_w_gate, ffn_w_up, ffn_w_down, ab_w_in, ab_w_out, ab_g_q, ab_g_k,
           ab_conv_w, ab_conv_b, ab_b_i, ab_b_f, pool_w, pool_scale):
    bsz, seq, d = x.shape
    depth = norm_g.shape[0]
    y = x.reshape(bsz * seq, d)
    w_gate, w_up, w_down = (w.astype(BF16) for w in (ffn_w_gate, ffn_w_up, ffn_w_down))
    for layer in range(depth):
        y = _ffn(y, norm_g[layer, 0], w_gate, w_up, w_down, (layer, 0))
        mixer_out = None
        if layer % 2 == 0:
            e = layer // 2
            ya, yb = _mixer_heads(y, norm_g[layer, 1], ab_w_in[e], ab_g_q[e], ab_g_k[e],
                                  ab_conv_w[e], ab_conv_b[e], ab_b_i[e], ab_b_f[e], bsz, seq)
            mixer_out = (ya, yb, ab_w_out[e])
        else:
            o = layer // 2
            y = _pool(y, norm_g[layer, 1], pool_w[o], pool_scale[o], seq)
        y = _ffn(y, norm_g[layer, 2], w_gate, w_up, w_down, (layer, 1), mixer_out=mixer_out)
    return y.reshape(bsz, seq, d)
```

```python
import functools

import jax
import jax.numpy as jnp
from jax import lax
from jax.experimental import pallas as pl
from jax.experimental.pallas import tpu as pltpu

F32 = jnp.float32
BF16 = jnp.bfloat16

LANES = 128
SUBLANES_F32 = 8
SUBLANES_BF16 = 16
MXU_WIDTH = 256
VMEM_BYTES = 64 * 1024 * 1024
VMEM_LIMIT_BYTES = VMEM_BYTES * 7 // 8

RMS_EPS = 1e-6
A_HEADS = 8
A_HEAD_DIM = 64
MOBA_BLOCK = 256
MOBA_TOPK = 3
B_HEADS = 4
B_HEAD_DIM = 128
CONV_WIDTH = 4
POOL_WINDOWS = (2, 4, 8, 16)

MOBA_KEY_GROUP = 2
MOBA_HEADS_PER_STEP = 4
MLSTM_KERNEL_CHUNK = 256
POOL_HALO = 16
POOL_PAD = SUBLANES_F32
MASK_VALUE = -1e30
LOG2E = 1.4426950408889634
V_AUG_ROWS = A_HEAD_DIM + SUBLANES_BF16
FFN_TOKEN_TILE = 1024
FFN_SUB_TILE = 512
FFN_HIDDEN_TILE = MXU_WIDTH
PROJ_TOKEN_TILE = 1024
POOL_TOKEN_TILE = 1024


def _compiler_params(semantics):
    return pltpu.CompilerParams(dimension_semantics=semantics,
                                vmem_limit_bytes=VMEM_LIMIT_BYTES)


def _rms_normalize(x, gain):
    ms = jnp.mean(x * x, axis=-1, keepdims=True)
    return x * lax.rsqrt(ms + RMS_EPS) * gain


def _dot(a, b):
    return jnp.dot(a, b, preferred_element_type=F32)


def _dot_nt(a, b):
    return lax.dot_general(a, b, (((1,), (1,)), ((), ())), preferred_element_type=F32)


def _split_bf16(x, parts):
    out = []
    rem = x
    for _ in range(parts):
        p = rem.astype(BF16)
        out.append(p)
        rem = rem - p.astype(F32)
    return out


def _ffn_kernel(x_ref, g_ref, wg_ref, wu_ref, wd_ref, *rest, hidden_tile, sub_tile):
    o_ref = rest[-1]
    d_ff = wg_ref.shape[1]
    for r0 in range(0, x_ref.shape[0], sub_tile):
        x = x_ref[r0:r0 + sub_tile, :]
        if len(rest) > 1:
            ya_ref, yb_ref, wa_ref, wb_ref = rest[:-1]
            x = (x + _dot(ya_ref[r0:r0 + sub_tile, :], wa_ref[...])
                 + _dot(yb_ref[r0:r0 + sub_tile, :], wb_ref[...]))
        h = _rms_normalize(x, g_ref[...]).astype(BF16)
        acc = None
        for c0 in range(0, d_ff, hidden_tile):
            gate = _dot(h, wg_ref[:, c0:c0 + hidden_tile])
            up = _dot(h, wu_ref[:, c0:c0 + hidden_tile])
            act = (gate * jax.nn.sigmoid(gate) * up).astype(BF16)
            part = _dot(act, wd_ref[c0:c0 + hidden_tile, :])
            acc = part if acc is None else acc + part
        o_ref[r0:r0 + sub_tile, :] = x + 0.5 * acc


def _resident(shape):
    return pl.BlockSpec(shape, lambda *_: (0,) * len(shape), pipeline_mode=pl.Buffered(1))


def _ffn(x, gain, w_gate, w_up, w_down, which, mixer_out=None):
    t, d = x.shape
    d_ff = w_gate.shape[-1]
    tm, tf = FFN_TOKEN_TILE, FFN_HIDDEN_TILE
    assert t % tm == 0 and d_ff % tf == 0 and tm % FFN_SUB_TILE == 0
    row = lambda i: (i, 0)
    picked = lambda r, c: pl.BlockSpec((None, None, r, c), lambda i: (*which, 0, 0),
                                       pipeline_mode=pl.Buffered(1))
    in_specs = [pl.BlockSpec((tm, d), row), _resident((1, d)), picked(d, d_ff), picked(d, d_ff),
                picked(d_ff, d)]
    args = [x, gain.reshape(1, d), w_gate, w_up, w_down]
    if mixer_out is not None:
        ya, yb, w_out = mixer_out
        wa, wb = ya.shape[1], yb.shape[1]
        in_specs += [pl.BlockSpec((tm, wa), row), pl.BlockSpec((tm, wb), row),
                     _resident((wa, d)), _resident((wb, d))]
        args += [ya, yb, w_out[:wa].astype(BF16), w_out[wa:].astype(BF16)]
    return pl.pallas_call(
        functools.partial(_ffn_kernel, hidden_tile=tf, sub_tile=FFN_SUB_TILE),
        grid=(t // tm,),
        in_specs=in_specs,
        out_specs=pl.BlockSpec((tm, d), row),
        out_shape=jax.ShapeDtypeStruct((t, d), F32),
        compiler_params=_compiler_params(("parallel",)),
        name="ffn",
    )(*args)


def _inproj_kernel(x_ref, g_ref, w_ref, wqt_ref, wvt_ref, wbt_ref, wgate_ref, gqcol_ref, gk_ref, grp_ref,
                   gbias_ref, qat_ref, ka_ref, vat_ref, kb_ref, qbt_ref, vbt_ref, obt_ref, gates_ref,
                   *, a_width, b_width):
    h = _rms_normalize(x_ref[...], g_ref[...]).astype(BF16)

    qt = _dot_nt(wqt_ref[...], h)
    for hd in range(a_width // A_HEAD_DIM):
        rows = slice(hd * A_HEAD_DIM, (hd + 1) * A_HEAD_DIM)
        y = qt[rows, :]
        ms = jnp.mean(y * y, axis=0, keepdims=True)
        gain = gqcol_ref[rows, :] * (A_HEAD_DIM ** -0.5 * LOG2E)
        qat_ref[rows, :] = (y * lax.rsqrt(ms + RMS_EPS) * gain).astype(BF16)

    y = _dot(h, w_ref[:, 0:a_width])
    sq_hi, sq_lo = _split_bf16(y * y, 2)
    ssq = _dot(sq_hi, grp_ref[...]) + _dot(sq_lo, grp_ref[...])
    ka_ref[...] = (y * lax.rsqrt(ssq * (1.0 / A_HEAD_DIM) + RMS_EPS) * gk_ref[...]).astype(BF16)

    vat_ref[...] = _dot_nt(wvt_ref[...], h).astype(BF16)
    kb_ref[...] = _dot(h, w_ref[:, a_width:a_width + b_width]).astype(BF16)
    for piece, out_ref in enumerate((qbt_ref, vbt_ref, obt_ref)):
        out_ref[...] = _dot_nt(wbt_ref[piece * b_width:(piece + 1) * b_width, :], h).astype(BF16)
    gates = _dot(h, wgate_ref[...]) + gbias_ref[...]
    gates_ref[...] = gates[:, :2 * B_HEADS]


def _inproj(x, gain, w_in, g_q, g_k, b_i, b_f):
    t, d = x.shape
    a_width = A_HEADS * A_HEAD_DIM
    b_width = B_HEADS * B_HEAD_DIM
    n_main = a_width + b_width
    o_qb = 3 * a_width
    o_ib = 3 * a_width + 3 * b_width
    tm = PROJ_TOKEN_TILE
    assert t % tm == 0
    w_main = jnp.concatenate([w_in[:, a_width:2 * a_width], w_in[:, o_qb + b_width:o_qb + 2 * b_width]],
                             axis=1).astype(BF16)
    w_qt = w_in[:, :a_width].T.astype(BF16)
    w_vt = w_in[:, 2 * a_width:3 * a_width].T.astype(BF16)
    w_bt = jnp.concatenate([w_in[:, o_qb:o_qb + b_width], w_in[:, o_qb + 2 * b_width:o_ib],
                            w_in[:, o_ib + 2 * B_HEADS:]], axis=1).T.astype(BF16)
    w_gate = jnp.pad(w_in[:, o_ib:o_ib + 2 * B_HEADS], ((0, 0), (0, LANES - 2 * B_HEADS))).astype(BF16)
    gate_bias = jnp.pad(jnp.concatenate([b_i, b_f]).astype(F32), (0, LANES - 2 * B_HEADS)).reshape(1, LANES)
    head_id = jnp.arange(a_width) // A_HEAD_DIM
    grp = (head_id[:, None] == head_id[None, :]).astype(BF16)
    gq_col = jnp.tile(g_q.astype(F32), A_HEADS).reshape(a_width, 1)
    gk = jnp.tile(g_k.astype(F32), A_HEADS).reshape(1, a_width)
    const = lambda i: (0, 0)
    row = lambda i: (i, 0)
    col = lambda i: (0, i)
    outs = pl.pallas_call(
        functools.partial(_inproj_kernel, a_width=a_width, b_width=b_width),
        grid=(t // tm,),
        in_specs=[
            pl.BlockSpec((tm, d), row),
            pl.BlockSpec((1, d), const),
            pl.BlockSpec((d, n_main), const),
            pl.BlockSpec((a_width, d), const),
            pl.BlockSpec((a_width, d), const),
            pl.BlockSpec((3 * b_width, d), const),
            pl.BlockSpec((d, LANES), const),
            pl.BlockSpec((a_width, 1), const),
            pl.BlockSpec((1, a_width), const),
            pl.BlockSpec((a_width, a_width), const),
            pl.BlockSpec((1, LANES), const),
        ],
        out_specs=[
            pl.BlockSpec((a_width, tm), col),
            pl.BlockSpec((tm, a_width), row),
            pl.BlockSpec((a_width, tm), col),
            pl.BlockSpec((tm, b_width), row),
            pl.BlockSpec((b_width, tm), col),
            pl.BlockSpec((b_width, tm), col),
            pl.BlockSpec((b_width, tm), col),
            pl.BlockSpec((tm, 2 * B_HEADS), row),
        ],
        out_shape=[
            jax.ShapeDtypeStruct((a_width, t), BF16),
            jax.ShapeDtypeStruct((t, a_width), BF16),
            jax.ShapeDtypeStruct((a_width, t), BF16),
            jax.ShapeDtypeStruct((t, b_width), BF16),
            jax.ShapeDtypeStruct((b_width, t), BF16),
            jax.ShapeDtypeStruct((b_width, t), BF16),
            jax.ShapeDtypeStruct((b_width, t), BF16),
            jax.ShapeDtypeStruct((t, 2 * B_HEADS), F32),
        ],
        compiler_params=_compiler_params(("parallel",)),
        name="inproj",
    )(x, gain.reshape(1, d), w_main, w_qt, w_vt, w_bt, w_gate, gq_col, gk, grp, gate_bias)
    return outs


def _moba_kernel(slope_ref, qt_ref, k_ref, vt_ref, o_ref, kaug_ref, vaug_ref, kmean_ref, qaug_ref, acc_ref,
                 sa_ref, sb_ref, *, nb, grp, heads, tiles):
    hg = pl.program_id(0)
    batch = pl.program_id(1)
    c0 = pl.program_id(2) * tiles
    blk = MOBA_BLOCK
    dh = A_HEAD_DIM
    nbp = pl.cdiv(nb, SUBLANES_BF16) * SUBLANES_BF16
    streams = [(ti, h) for ti in range(tiles) for h in range(heads)]

    @pl.when(c0 == 0)
    def _build_key_value_side():
        ones_rows = jnp.where(lax.broadcasted_iota(jnp.int32, (V_AUG_ROWS - dh, blk), 0) == 0,
                              1.0, 0.0).astype(BF16)
        for h in range(heads):
            for j in range(nb):
                g, off = divmod(j, grp)
                vaug_ref[h, g, 0:dh, off * blk:(off + 1) * blk] = vt_ref[dh * h:dh * (h + 1),
                                                                         j * blk:(j + 1) * blk]
                vaug_ref[h, g, dh:V_AUG_ROWS, off * blk:(off + 1) * blk] = ones_rows
        kmean_ref[...] = jnp.zeros_like(kmean_ref)
        lane = lax.broadcasted_iota(jnp.int32, (blk, LANES), 1)
        row = lax.broadcasted_iota(jnp.int32, (blk, LANES), 0)

        def key_block(j, carry, *, with_extra_terms):
            start = pl.multiple_of(j * blk, blk)
            pos = (row + j * blk).astype(F32)
            for pr in range(heads // 2):
                kb = k_ref[pl.ds(start, blk), pr * LANES:(pr + 1) * LANES]
                kmean_ref[pr, pl.ds(j, 1), :] = jnp.mean(kb.astype(F32), axis=0, keepdims=True)
                for hh in range(2):
                    h = 2 * pr + hh
                    is_data = (lane >= dh * hh) & (lane < dh * (hh + 1))
                    if with_extra_terms:
                        slope = slope_ref[pl.ds(heads * hg + h, 1), :][:, 0:LANES] * LOG2E
                        p1, p2, p3 = _split_bf16(slope * pos, 3)
                        rel = lane - dh * (1 - hh)
                        aug = jnp.where(rel == j, 1.0, 0.0)
                        aug = jnp.where(rel == nb, p1.astype(F32), aug)
                        aug = jnp.where(rel == nb + 1, p2.astype(F32), aug)
                        aug = jnp.where(rel == nb + 2, p3.astype(F32), aug)
                        aug = jnp.where((rel >= nb + 3) & (rel < nb + 6), 1.0, aug).astype(BF16)
                    else:
                        aug = kaug_ref[h, pl.ds(start, blk), :]
                    kaug_ref[h, pl.ds(start, blk), :] = jnp.where(is_data, kb, aug)
            return carry

        @pl.when(batch == 0)
        def _():
            lax.fori_loop(0, nb, functools.partial(key_block, with_extra_terms=True), 0)

        @pl.when(batch != 0)
        def _():
            lax.fori_loop(0, nb, functools.partial(key_block, with_extra_terms=False), 0)

    blk_ix = lax.broadcasted_iota(jnp.int32, (nbp, blk), 0)
    blk_f = blk_ix.astype(F32)
    aug_row = lax.broadcasted_iota(jnp.int32, (dh, blk), 0)
    qry_lane = lax.broadcasted_iota(jnp.int32, (dh, blk), 1)
    lane_k = lax.broadcasted_iota(jnp.int32, (nbp, LANES), 1)
    own_grp = c0 // grp
    key_ix = lax.broadcasted_iota(jnp.int32, (grp * blk, blk), 0) + own_grp * (grp * blk)
    qry_ix = lax.broadcasted_iota(jnp.int32, (grp * blk, blk), 1)
    gates = []
    for ti, h in streams:
        pr, hh = divmod(h, 2)
        qt_pair = qt_ref[pr * LANES:(pr + 1) * LANES, ti * blk:(ti + 1) * blk]
        is_data_k = (lane_k >= dh * hh) & (lane_k < dh * (hh + 1))
        km_hi, km_lo = _split_bf16(jnp.where(is_data_k, kmean_ref[pr, 0:nbp, :], 0.0), 2)
        gates.append(_dot(km_hi, qt_pair) + _dot(km_lo, qt_pair))
    for si, (ti, h) in enumerate(streams):
        pr, hh = divmod(h, 2)
        c = c0 + ti
        valid = blk_ix < c
        tq = (qry_lane + c * blk).astype(F32)
        gate = jnp.where(valid, gates[si], -jnp.inf)
        chosen = blk_ix == c
        for _ in range(MOBA_TOPK):
            best = jnp.max(gate, axis=0, keepdims=True)
            first = jnp.min(jnp.where(gate == best, blk_f, float(nbp)), axis=0, keepdims=True)
            pick = blk_f == first
            chosen = chosen | (pick & valid)
            gate = jnp.where(pick, -jnp.inf, gate)
        bias = jnp.where(chosen, 0.0, MASK_VALUE)
        if nbp < dh:
            bias = jnp.concatenate([bias, jnp.zeros((dh - nbp, blk), F32)], axis=0)
        slope = slope_ref[pl.ds(heads * hg + h, 1), :] * LOG2E
        t1, t2, t3 = _split_bf16(-slope * tq, 3)
        aug = jnp.where(aug_row < nb, bias, 0.0)
        aug = jnp.where((aug_row >= nb) & (aug_row < nb + 3), 1.0, aug)
        aug = jnp.where(aug_row == nb + 3, t1.astype(F32), aug)
        aug = jnp.where(aug_row == nb + 4, t2.astype(F32), aug)
        aug = jnp.where(aug_row == nb + 5, t3.astype(F32), aug).astype(BF16)
        data = qt_ref[dh * h:dh * (h + 1), ti * blk:(ti + 1) * blk]
        qaug_ref[si] = jnp.concatenate([data, aug] if hh == 0 else [aug, data], axis=0)

    def scores_into(buf_ref, g):
        start = pl.multiple_of(g * (grp * blk), grp * blk)
        col_max = []
        for si, (ti, h) in enumerate(streams):
            s = _dot(kaug_ref[h, pl.ds(start, grp * blk), :], qaug_ref[si])
            buf_ref[si] = s
            col_max.append(jnp.max(s, axis=0, keepdims=True))
        return tuple(col_max)

    def consume(buf_ref, g, ms, col_max, own=False):
        new_ms = []
        for si, (ti, h) in enumerate(streams):
            s = buf_ref[si]
            if own:
                s = jnp.where(key_ix <= qry_ix + (c0 + ti) * blk, s, MASK_VALUE)
                group_max = jnp.max(s, axis=0, keepdims=True)
            else:
                group_max = col_max[si]
            m_new = jnp.maximum(ms[si], group_max)
            alpha = jnp.exp2(ms[si] - m_new)
            p = jnp.exp2(s - m_new)
            acc_ref[si] = alpha * acc_ref[si] + _dot(vaug_ref[h, g], p.astype(BF16))
            new_ms.append(m_new)
        return tuple(new_ms)

    def write_output():
        for ti in range(tiles):
            outs = []
            for h in range(heads):
                acc = acc_ref[ti * heads + h]
                outs.append(acc[0:dh, :] / acc[dh:dh + 1, :])
            o_ref[ti * blk:(ti + 1) * blk, :] = jnp.concatenate(outs, axis=0).T.astype(o_ref.dtype)

    acc_ref[...] = jnp.zeros_like(acc_ref)
    masked_score = jnp.full((1, blk), MASK_VALUE, F32).astype(BF16).astype(F32)
    max_a0 = scores_into(sa_ref, 0)

    def two_groups(i, carry):
        ms, max_a = carry
        max_b = scores_into(sb_ref, 2 * i + 1)
        ms = consume(sa_ref, 2 * i, ms, max_a)
        max_a = scores_into(sa_ref, 2 * i + 2)
        return consume(sb_ref, 2 * i + 1, ms, max_b), max_a

    ms, max_a = lax.fori_loop(0, own_grp // 2, two_groups, ((masked_score,) * len(streams), max_a0))

    @pl.when(own_grp % 2 == 0)
    def _own_group_in_a():
        consume(sa_ref, own_grp, ms, None, own=True)
        write_output()

    @pl.when(own_grp % 2 == 1)
    def _own_group_in_b():
        scores_into(sb_ref, own_grp)
        consume(sb_ref, own_grp, consume(sa_ref, own_grp - 1, ms, max_a), None, own=True)
        write_output()


def _moba(qat, ka, vat, bsz, seq):
    t, a_width = ka.shape
    blk = MOBA_BLOCK
    assert seq % blk == 0 and seq // blk >= MOBA_TOPK
    nb = seq // blk
    assert nb + 6 <= A_HEAD_DIM
    grp = MOBA_KEY_GROUP if nb % MOBA_KEY_GROUP == 0 else 1
    heads = MOBA_HEADS_PER_STEP
    width = heads * A_HEAD_DIM
    assert heads % 2 == 0 and a_width % width == 0
    tiles = grp
    n_streams = tiles * heads
    steps = nb // tiles
    slopes = jnp.exp2(-8.0 * jnp.arange(1, A_HEADS + 1, dtype=F32) / A_HEADS)
    slope_tbl = jnp.broadcast_to(slopes[:, None], (A_HEADS, blk))
    return pl.pallas_call(
        functools.partial(_moba_kernel, nb=nb, grp=grp, heads=heads, tiles=tiles),
        grid=(a_width // width, bsz, steps),
        in_specs=[
            pl.BlockSpec((A_HEADS, blk), lambda hg, b, c: (0, 0)),
            pl.BlockSpec((width, tiles * blk), lambda hg, b, c: (hg, b * steps + c)),
            pl.BlockSpec((seq, width), lambda hg, b, c: (b, hg), pipeline_mode=pl.Buffered(1)),
            pl.BlockSpec((width, seq), lambda hg, b, c: (hg, b), pipeline_mode=pl.Buffered(1)),
        ],
        out_specs=pl.BlockSpec((tiles * blk, width), lambda hg, b, c: (b * steps + c, hg)),
        out_shape=jax.ShapeDtypeStruct((t, a_width), BF16),
        scratch_shapes=[
            pltpu.VMEM((heads, seq, LANES), BF16),
            pltpu.VMEM((heads, nb // grp, V_AUG_ROWS, grp * blk), BF16),
            pltpu.VMEM((heads // 2, LANES, LANES), F32),
            pltpu.VMEM((n_streams, LANES, blk), BF16),
            pltpu.VMEM((n_streams, V_AUG_ROWS, blk), F32),
            pltpu.VMEM((n_streams, grp * blk, blk), F32),
            pltpu.VMEM((n_streams, grp * blk, blk), F32),
        ],
        compiler_params=_compiler_params(("parallel", "arbitrary", "arbitrary")),
        name="moba",
    )(slope_tbl, qat, ka, vat)


def _mlstm_kernel(qt_ref, k_ref, vt_ref, obt_ref, grow_ref, cwq_ref, cbq_ref, cwk_ref, cbk_ref,
                  o_ref, qtail_ref, kbuf_ref, ct_ref, m_ref):
    ci = pl.program_id(1)
    L, width = k_ref.shape
    d = B_HEAD_DIM
    halo = SUBLANES_F32

    @pl.when(ci == 0)
    def _():
        qtail_ref[...] = jnp.zeros_like(qtail_ref)
        kbuf_ref[0:halo, :] = jnp.zeros((halo, width), F32)
        ct_ref[...] = jnp.zeros_like(ct_ref)
        m_ref[...] = jnp.zeros_like(m_ref)

    xq = qt_ref[...].astype(F32)
    lane = lax.broadcasted_iota(jnp.int32, (width, LANES), 1)
    src = lax.broadcasted_iota(jnp.int32, (L, L), 0)
    dst = lax.broadcasted_iota(jnp.int32, (L, L), 1)
    tail = qtail_ref[...]
    yq = cbq_ref[...] + xq * cwq_ref[CONV_WIDTH - 1]
    for s in range(1, CONV_WIDTH):
        shifted = _dot(qt_ref[...], (src + s == dst).astype(BF16))
        first = jnp.where(lane < s, pltpu.roll(tail, s, axis=1), shifted[:, :LANES])
        shifted = jnp.concatenate([first, shifted[:, LANES:]], axis=1)
        yq = yq + shifted * cwq_ref[CONV_WIDTH - 1 - s]
    qtail_ref[...] = xq[:, L - LANES:]
    q_all = (yq * jax.nn.sigmoid(yq) * (d ** -0.5)).astype(BF16)

    kbuf_ref[halo:halo + L, :] = k_ref[...].astype(F32)
    yk = cbk_ref[...]
    for j in range(CONV_WIDTH):
        off = halo - (CONV_WIDTH - 1) + j
        yk = yk + kbuf_ref[off:off + L, :] * cwk_ref[j:j + 1, :]
    kbuf_ref[0:halo, :] = kbuf_ref[L:L + halo, :]
    k_all = (yk * jax.nn.sigmoid(yk)).astype(BF16)

    def log_sigmoid(z):
        return jnp.minimum(z, 0.0) - jnp.log1p(jnp.exp(-jnp.abs(z)))

    causal = src <= dst
    tri_up = causal.astype(BF16)
    gates_row = grow_ref[0]
    gates_col = gates_row.T
    lf_hi, lf_lo = _split_bf16(log_sigmoid(gates_row), 2)
    b_rows = _dot(lf_hi, tri_up) + _dot(lf_lo, tri_up)
    b_cols = b_rows.T
    pad_rows = ct_ref.shape[1] - d
    ones_rows = jnp.where(lax.broadcasted_iota(jnp.int32, (pad_rows, L), 0) == 0, 1.0, 0.0).astype(BF16)
    for hd in range(B_HEADS):
        rows = slice(hd * d, (hd + 1) * d)
        qt = q_all[rows, :]
        k = k_all[:, rows]
        vt_aug = jnp.concatenate([vt_ref[rows, :], ones_rows], axis=0)
        i_col = gates_col[:, hd:hd + 1]
        i_row = gates_row[hd:hd + 1, :]
        b_col = b_cols[:, hd + B_HEADS:hd + B_HEADS + 1]
        b_row = b_rows[hd + B_HEADS:hd + B_HEADS + 1, :]

        m_prev = m_ref[hd]
        log_inter = b_row + m_prev
        dmat = jnp.where(causal, b_row + (i_col - b_col), -jnp.inf)
        m_t = jnp.maximum(log_inter, jnp.max(dmat, axis=0, keepdims=True))
        w_inter = jnp.exp(log_inter - m_t)
        sc = _dot(k, qt) * jnp.exp(dmat - m_t)
        ct_prev = ct_ref[hd]
        inter = w_inter * _dot(ct_prev.astype(BF16), qt)
        num = inter[:d, :] + _dot(vt_ref[rows, :], sc.astype(BF16))
        den = inter[d:d + 1, :] + jnp.sum(sc, axis=0, keepdims=True)
        hidden = num / jnp.maximum(jnp.abs(den), jnp.exp(-m_t))
        out_t = jax.nn.sigmoid(obt_ref[rows, :].astype(F32)) * hidden
        o_ref[:, rows] = out_t.T.astype(o_ref.dtype)

        b_last = b_row[:, L - 1:L]
        log_old = b_last + m_prev
        log_new = b_last - b_row + i_row
        m_new = jnp.maximum(log_old, jnp.max(log_new, axis=1, keepdims=True))
        a_old = jnp.exp(log_old - m_new)
        a_new = jnp.exp(log_new - m_new)
        ct_ref[hd] = a_old * ct_prev + _dot((vt_aug.astype(F32) * a_new).astype(BF16), k)
        m_ref[hd] = m_new


def _mlstm(qbt, kb, vbt, obt, gates, conv_w, conv_b, bsz, seq):
    b_width, t = qbt.shape
    d = B_HEAD_DIM
    L = min(MLSTM_KERNEL_CHUNK, seq)
    assert seq % L == 0 and L % LANES == 0 and d == LANES and b_width == B_HEADS * d
    nc = seq // L
    gates_row = gates.reshape(bsz, seq, 2 * B_HEADS).transpose(0, 2, 1)
    cwq = jnp.broadcast_to(conv_w[:, :b_width, None].astype(F32), (CONV_WIDTH, b_width, L))
    cbq = jnp.broadcast_to(conv_b[:b_width, None].astype(F32), (b_width, L))
    cwk = conv_w[:, b_width:].astype(F32)
    cbk = conv_b[b_width:].astype(F32).reshape(1, b_width)
    tposed = lambda b, c: (0, b * nc + c)
    natural = lambda b, c: (b * nc + c, 0)
    const2 = lambda b, c: (0, 0)
    return pl.pallas_call(
        _mlstm_kernel,
        grid=(bsz, nc),
        in_specs=[
            pl.BlockSpec((b_width, L), tposed),
            pl.BlockSpec((L, b_width), natural),
            pl.BlockSpec((b_width, L), tposed),
            pl.BlockSpec((b_width, L), tposed),
            pl.BlockSpec((1, 2 * B_HEADS, L), lambda b, c: (b, 0, c)),
            pl.BlockSpec((CONV_WIDTH, b_width, L), lambda b, c: (0, 0, 0)),
            pl.BlockSpec((b_width, L), const2),
            pl.BlockSpec((CONV_WIDTH, b_width), const2),
            pl.BlockSpec((1, b_width), const2),
        ],
        out_specs=pl.BlockSpec((L, b_width), natural),
        out_shape=jax.ShapeDtypeStruct((t, b_width), BF16),
        scratch_shapes=[
            pltpu.VMEM((b_width, LANES), F32),
            pltpu.VMEM((L + SUBLANES_F32, b_width), F32),
            pltpu.VMEM((B_HEADS, d + SUBLANES_BF16, d), F32),
            pltpu.VMEM((B_HEADS, 1, 1), F32),
        ],
        compiler_params=_compiler_params(("parallel", "arbitrary")),
        name="mlstm",
    )(qbt, kb, vbt, obt, gates_row, cwq, cbq, cwk, cbk)


def _pool_kernel(x_ref, halo_ref, g_ref, w_ref, scale_ref, o_ref, sums_ref, *, tiles_per_seq):
    i = pl.program_id(0)
    tm, d = x_ref.shape
    n_win = len(POOL_WINDOWS)
    grp = d // n_win
    base = POOL_PAD + POOL_HALO
    rows = POOL_HALO + tm
    x = x_ref[...]
    seq_tile = i % tiles_per_seq
    sums_ref[:, 0:POOL_PAD, :] = jnp.zeros((n_win, POOL_PAD, d), F32)
    halo_h = _rms_normalize(halo_ref[...], g_ref[...])
    sums_ref[0, POOL_PAD:base, :] = jnp.where(seq_tile == 0, 0.0, halo_h)
    h = _rms_normalize(x, g_ref[...])
    sums_ref[0, base:base + tm, :] = h
    t1 = (lax.broadcasted_iota(jnp.int32, (tm, 1), 0) + seq_tile * tm + 1).astype(F32)
    for k, win in enumerate(POOL_WINDOWS):
        half = win // 2
        c0 = k * grp
        if k + 1 < n_win:
            both = (sums_ref[k, POOL_PAD:POOL_PAD + rows, c0:]
                    + sums_ref[k, POOL_PAD - half:POOL_PAD - half + rows, c0:])
            sums_ref[k + 1, POOL_PAD:POOL_PAD + rows, c0:] = both
            total = both[POOL_HALO:, 0:grp]
        else:
            total = (sums_ref[k, base:base + tm, c0:c0 + grp]
                     + sums_ref[k, base - half:base - half + tm, c0:c0 + grp])
        pooled = total / jnp.minimum(t1, float(win)) - h[:, c0:c0 + grp]
        y = _dot(pooled.astype(BF16), w_ref[k])
        o_ref[:, c0:c0 + grp] = x[:, c0:c0 + grp] + y * scale_ref[:, c0:c0 + grp]


def _pool(x, gain, w_grp, scale, seq):
    t, d = x.shape
    tm = min(POOL_TOKEN_TILE, seq)
    assert seq % tm == 0 and tm % POOL_HALO == 0
    assert POOL_WINDOWS == tuple(2 ** (k + 1) for k in range(len(POOL_WINDOWS)))
    assert POOL_WINDOWS[-1] <= POOL_HALO and POOL_WINDOWS[-1] // 2 <= POOL_PAD
    n_grp, grp, _ = w_grp.shape
    halo_blocks = tm // POOL_HALO
    return pl.pallas_call(
        functools.partial(_pool_kernel, tiles_per_seq=seq // tm),
        grid=(t // tm,),
        in_specs=[
            pl.BlockSpec((tm, d), lambda i: (i, 0)),
            pl.BlockSpec((POOL_HALO, d), lambda i: (jnp.maximum(i * halo_blocks - 1, 0), 0)),
            pl.BlockSpec((1, d), lambda i: (0, 0)),
            pl.BlockSpec((n_grp, grp, grp), lambda i: (0, 0, 0)),
            pl.BlockSpec((1, d), lambda i: (0, 0)),
        ],
        out_specs=pl.BlockSpec((tm, d), lambda i: (i, 0)),
        out_shape=jax.ShapeDtypeStruct((t, d), F32),
        scratch_shapes=[pltpu.VMEM((len(POOL_WINDOWS), POOL_PAD + POOL_HALO + tm, d), F32)],
        compiler_params=_compiler_params(("parallel",)),
        name="pool",
    )(x, x, gain.reshape(1, d), w_grp.astype(BF16), scale.astype(F32).reshape(1, d))


def _mixer_heads(x, gain, w_in, g_q, g_k, conv_w, conv_b, b_i, b_f, bsz, seq):
    qat, ka, vat, kb, qbt, vbt, obt, gates = _inproj(x, gain, w_in, g_q, g_k, b_i, b_f)
    ya = _moba(qat, ka, vat, bsz, seq)
    yb = _mlstm(qbt, kb, vbt, obt, gates, conv_w, conv_b, bsz, seq)
    return ya, yb


def kernel(x, norm_g, ffn_w_gate, ffn_w_up, ffn_w_down, ab_w_in, ab_w_out, ab_g_q, ab_g_k,
           ab_conv_w, ab_conv_b, ab_b_i, ab_b_f, pool_w, pool_scale):
    bsz, seq, d = x.shape
    depth = norm_g.shape[0]
    y = x.reshape(bsz * seq, d)
    w_gate, w_up, w_down = (w.astype(BF16) for w in (ffn_w_gate, ffn_w_up, ffn_w_down))
    for layer in range(depth):
        y = _ffn(y, norm_g[layer, 0], w_gate, w_up, w_down, (layer, 0))
        mixer_out = None
        if layer % 2 == 0:
            e = layer // 2
            ya, yb = _mixer_heads(y, norm_g[layer, 1], ab_w_in[e], ab_g_q[e], ab_g_k[e],
                                  ab_conv_w[e], ab_conv_b[e], ab_b_i[e], ab_b_f[e], bsz, seq)
            mixer_out = (ya, yb, ab_w_out[e])
        else:
            o = layer // 2
            y = _pool(y, norm_g[layer, 1], pool_w[o], pool_scale[o], seq)
        y = _ffn(y, norm_g[layer, 2], w_gate, w_up, w_down, (layer, 1), mixer_out=mixer_out)
    return y.reshape(bsz, seq, d)
```

```python
import functools

import jax
import jax.numpy as jnp
from jax import lax
from jax.experimental import pallas as pl
from jax.experimental.pallas import tpu as pltpu

F32 = jnp.float32
BF16 = jnp.bfloat16

LANES = 128
SUBLANES_F32 = 8
SUBLANES_BF16 = 16
MXU_WIDTH = 256
VMEM_BYTES = 64 * 1024 * 1024
VMEM_LIMIT_BYTES = VMEM_BYTES * 7 // 8

RMS_EPS = 1e-6
A_HEADS = 8
A_HEAD_DIM = 64
MOBA_BLOCK = 256
MOBA_TOPK = 3
B_HEADS = 4
B_HEAD_DIM = 128
CONV_WIDTH = 4
POOL_WINDOWS = (2, 4, 8, 16)

MOBA_KEY_GROUP = 2
MOBA_HEADS_PER_STEP = 4
MLSTM_KERNEL_CHUNK = 256
POOL_HALO = 16
POOL_PAD = SUBLANES_F32
MASK_VALUE = -1e30
LOG2E = 1.4426950408889634
V_AUG_ROWS = A_HEAD_DIM + SUBLANES_BF16
FFN_TOKEN_TILE = 1024
FFN_SUB_TILE = 512
FFN_HIDDEN_TILE = MXU_WIDTH
PROJ_TOKEN_TILE = 1024
POOL_TOKEN_TILE = 1024


def _compiler_params(semantics):
    return pltpu.CompilerParams(dimension_semantics=semantics,
                                vmem_limit_bytes=VMEM_LIMIT_BYTES)


def _rms_normalize(x, gain):
    ms = jnp.mean(x * x, axis=-1, keepdims=True)
    return x * lax.rsqrt(ms + RMS_EPS) * gain


def _dot(a, b):
    return jnp.dot(a, b, preferred_element_type=F32)


def _dot_nt(a, b):
    return lax.dot_general(a, b, (((1,), (1,)), ((), ())), preferred_element_type=F32)


def _split_bf16(x, parts):
    out = []
    rem = x
    for _ in range(parts):
        p = rem.astype(BF16)
        out.append(p)
        rem = rem - p.astype(F32)
    return out


def _ffn_kernel(x_ref, g_ref, wg_ref, wu_ref, wd_ref, *rest, hidden_tile, sub_tile):
    o_ref = rest[-1]
    d_ff = wg_ref.shape[1]
    for r0 in range(0, x_ref.shape[0], sub_tile):
        x = x_ref[r0:r0 + sub_tile, :]
        if len(rest) > 1:
            ya_ref, yb_ref, wa_ref, wb_ref = rest[:-1]
            x = (x + _dot(ya_ref[r0:r0 + sub_tile, :], wa_ref[...])
                 + _dot(yb_ref[r0:r0 + sub_tile, :], wb_ref[...]))
        h = _rms_normalize(x, g_ref[...]).astype(BF16)
        acc = None
        for c0 in range(0, d_ff, hidden_tile):
            gate = _dot(h, wg_ref[:, c0:c0 + hidden_tile])
            up = _dot(h, wu_ref[:, c0:c0 + hidden_tile])
            act = (gate * jax.nn.sigmoid(gate) * up).astype(BF16)
            part = _dot(act, wd_ref[c0:c0 + hidden_tile, :])
            acc = part if acc is None else acc + part
        o_ref[r0:r0 + sub_tile, :] = x + 0.5 * acc


def _resident(shape):
    return pl.BlockSpec(shape, lambda *_: (0,) * len(shape), pipeline_mode=pl.Buffered(1))


def _ffn(x, gain, w_gate, w_up, w_down, which, mixer_out=None):
    t, d = x.shape
    d_ff = w_gate.shape[-1]
    tm, tf = FFN_TOKEN_TILE, FFN_HIDDEN_TILE
    assert t % tm == 0 and d_ff % tf == 0 and tm % FFN_SUB_TILE == 0
    row = lambda i: (i, 0)
    picked = lambda r, c: pl.BlockSpec((None, None, r, c), lambda i: (*which, 0, 0),
                                       pipeline_mode=pl.Buffered(1))
    in_specs = [pl.BlockSpec((tm, d), row), _resident((1, d)), picked(d, d_ff), picked(d, d_ff),
                picked(d_ff, d)]
    args = [x, gain.reshape(1, d), w_gate, w_up, w_down]
    if mixer_out is not None:
        ya, yb, w_out = mixer_out
        wa, wb = ya.shape[1], yb.shape[1]
        in_specs += [pl.BlockSpec((tm, wa), row), pl.BlockSpec((tm, wb), row),
                     _resident((wa, d)), _resident((wb, d))]
        args += [ya, yb, w_out[:wa].astype(BF16), w_out[wa:].astype(BF16)]
    return pl.pallas_call(
        functools.partial(_ffn_kernel, hidden_tile=tf, sub_tile=FFN_SUB_TILE),
        grid=(t // tm,),
        in_specs=in_specs,
        out_specs=pl.BlockSpec((tm, d), row),
        out_shape=jax.ShapeDtypeStruct((t, d), F32),
        compiler_params=_compiler_params(("parallel",)),
        name="ffn",
    )(*args)


def _inproj_kernel(x_ref, g_ref, w_ref, wqt_ref, wrt_ref, wgate_ref, gqcol_ref, gk_ref, grp_ref,
                   gbias_ref, qat_ref, ka_ref, vat_ref, kb_ref, qbt_ref, vbt_ref, obt_ref, gates_ref,
                   *, a_width, b_width):
    h = _rms_normalize(x_ref[...], g_ref[...]).astype(BF16)

    qt = _dot_nt(wqt_ref[...], h)
    for hd in range(a_width // A_HEAD_DIM):
        rows = slice(hd * A_HEAD_DIM, (hd + 1) * A_HEAD_DIM)
        y = qt[rows, :]
        ms = jnp.mean(y * y, axis=0, keepdims=True)
        gain = gqcol_ref[rows, :] * (A_HEAD_DIM ** -0.5 * LOG2E)
        qat_ref[rows, :] = (y * lax.rsqrt(ms + RMS_EPS) * gain).astype(BF16)

    y = _dot(h, w_ref[:, 0:a_width])
    sq_hi, sq_lo = _split_bf16(y * y, 2)
    ssq = _dot(sq_hi, grp_ref[...]) + _dot(sq_lo, grp_ref[...])
    ka_ref[...] = (y * lax.rsqrt(ssq * (1.0 / A_HEAD_DIM) + RMS_EPS) * gk_ref[...]).astype(BF16)

    kb_ref[...] = _dot(h, w_ref[:, a_width:a_width + b_width]).astype(BF16)
    rest_t = _dot_nt(wrt_ref[...], h)
    vat_ref[...] = rest_t[0:a_width, :].astype(BF16)
    for piece, out_ref in enumerate((qbt_ref, vbt_ref, obt_ref)):
        r0 = a_width + piece * b_width
        out_ref[...] = rest_t[r0:r0 + b_width, :].astype(BF16)
    gates = _dot(h, wgate_ref[...]) + gbias_ref[...]
    gates_ref[...] = gates[:, :2 * B_HEADS]


def _inproj(x, gain, w_in, g_q, g_k, b_i, b_f):
    t, d = x.shape
    a_width = A_HEADS * A_HEAD_DIM
    b_width = B_HEADS * B_HEAD_DIM
    n_main = a_width + b_width
    o_qb = 3 * a_width
    o_ib = 3 * a_width + 3 * b_width
    tm = PROJ_TOKEN_TILE
    assert t % tm == 0
    w_main = jnp.concatenate([w_in[:, a_width:2 * a_width], w_in[:, o_qb + b_width:o_qb + 2 * b_width]],
                             axis=1).astype(BF16)
    w_qt = w_in[:, :a_width].T.astype(BF16)
    w_rt = jnp.concatenate([w_in[:, 2 * a_width:3 * a_width], w_in[:, o_qb:o_qb + b_width],
                            w_in[:, o_qb + 2 * b_width:o_ib], w_in[:, o_ib + 2 * B_HEADS:]],
                           axis=1).T.astype(BF16)
    w_gate = jnp.pad(w_in[:, o_ib:o_ib + 2 * B_HEADS], ((0, 0), (0, LANES - 2 * B_HEADS))).astype(BF16)
    gate_bias = jnp.pad(jnp.concatenate([b_i, b_f]).astype(F32), (0, LANES - 2 * B_HEADS)).reshape(1, LANES)
    head_id = jnp.arange(a_width) // A_HEAD_DIM
    grp = (head_id[:, None] == head_id[None, :]).astype(BF16)
    gq_col = jnp.tile(g_q.astype(F32), A_HEADS).reshape(a_width, 1)
    gk = jnp.tile(g_k.astype(F32), A_HEADS).reshape(1, a_width)
    const = lambda i: (0, 0)
    row = lambda i: (i, 0)
    col = lambda i: (0, i)
    outs = pl.pallas_call(
        functools.partial(_inproj_kernel, a_width=a_width, b_width=b_width),
        grid=(t // tm,),
        in_specs=[
            pl.BlockSpec((tm, d), row),
            pl.BlockSpec((1, d), const),
            pl.BlockSpec((d, n_main), const),
            pl.BlockSpec((a_width, d), const),
            pl.BlockSpec((a_width + 3 * b_width, d), const),
            pl.BlockSpec((d, LANES), const),
            pl.BlockSpec((a_width, 1), const),
            pl.BlockSpec((1, a_width), const),
            pl.BlockSpec((a_width, a_width), const),
            pl.BlockSpec((1, LANES), const),
        ],
        out_specs=[
            pl.BlockSpec((a_width, tm), col),
            pl.BlockSpec((tm, a_width), row),
            pl.BlockSpec((a_width, tm), col),
            pl.BlockSpec((tm, b_width), row),
            pl.BlockSpec((b_width, tm), col),
            pl.BlockSpec((b_width, tm), col),
            pl.BlockSpec((b_width, tm), col),
            pl.BlockSpec((tm, 2 * B_HEADS), row),
        ],
        out_shape=[
            jax.ShapeDtypeStruct((a_width, t), BF16),
            jax.ShapeDtypeStruct((t, a_width), BF16),
            jax.ShapeDtypeStruct((a_width, t), BF16),
            jax.ShapeDtypeStruct((t, b_width), BF16),
            jax.ShapeDtypeStruct((b_width, t), BF16),
            jax.ShapeDtypeStruct((b_width, t), BF16),
            jax.ShapeDtypeStruct((b_width, t), BF16),
            jax.ShapeDtypeStruct((t, 2 * B_HEADS), F32),
        ],
        compiler_params=_compiler_params(("parallel",)),
        name="inproj",
    )(x, gain.reshape(1, d), w_main, w_qt, w_rt, w_gate, gq_col, gk, grp, gate_bias)
    return outs


def _moba_kernel(slope_ref, qt_ref, k_ref, vt_ref, o_ref, kaug_ref, vaug_ref, kmean_ref, qaug_ref, acc_ref,
                 sa_ref, sb_ref, *, nb, grp, heads, tiles):
    hg = pl.program_id(0)
    batch = pl.program_id(1)
    c0 = pl.program_id(2) * tiles
    blk = MOBA_BLOCK
    dh = A_HEAD_DIM
    nbp = pl.cdiv(nb, SUBLANES_BF16) * SUBLANES_BF16
    streams = [(ti, h) for ti in range(tiles) for h in range(heads)]

    @pl.when(c0 == 0)
    def _build_key_value_side():
        ones_rows = jnp.where(lax.broadcasted_iota(jnp.int32, (V_AUG_ROWS - dh, blk), 0) == 0,
                              1.0, 0.0).astype(BF16)
        for h in range(heads):
            for j in range(nb):
                g, off = divmod(j, grp)
                vaug_ref[h, g, 0:dh, off * blk:(off + 1) * blk] = vt_ref[dh * h:dh * (h + 1),
                                                                         j * blk:(j + 1) * blk]
                vaug_ref[h, g, dh:V_AUG_ROWS, off * blk:(off + 1) * blk] = ones_rows
        kmean_ref[...] = jnp.zeros_like(kmean_ref)
        lane = lax.broadcasted_iota(jnp.int32, (blk, LANES), 1)
        row = lax.broadcasted_iota(jnp.int32, (blk, LANES), 0)

        def key_block(j, carry, *, with_extra_terms):
            start = pl.multiple_of(j * blk, blk)
            pos = (row + j * blk).astype(F32)
            for pr in range(heads // 2):
                kb = k_ref[pl.ds(start, blk), pr * LANES:(pr + 1) * LANES]
                kmean_ref[pr, pl.ds(j, 1), :] = jnp.mean(kb.astype(F32), axis=0, keepdims=True)
                for hh in range(2):
                    h = 2 * pr + hh
                    is_data = (lane >= dh * hh) & (lane < dh * (hh + 1))
                    if with_extra_terms:
                        slope = slope_ref[pl.ds(heads * hg + h, 1), :][:, 0:LANES] * LOG2E
                        p1, p2, p3 = _split_bf16(slope * pos, 3)
                        rel = lane - dh * (1 - hh)
                        aug = jnp.where(rel == j, 1.0, 0.0)
                        aug = jnp.where(rel == nb, p1.astype(F32), aug)
                        aug = jnp.where(rel == nb + 1, p2.astype(F32), aug)
                        aug = jnp.where(rel == nb + 2, p3.astype(F32), aug)
                        aug = jnp.where((rel >= nb + 3) & (rel < nb + 6), 1.0, aug).astype(BF16)
                    else:
                        aug = kaug_ref[h, pl.ds(start, blk), :]
                    kaug_ref[h, pl.ds(start, blk), :] = jnp.where(is_data, kb, aug)
            return carry

        @pl.when(batch == 0)
        def _():
            lax.fori_loop(0, nb, functools.partial(key_block, with_extra_terms=True), 0)

        @pl.when(batch != 0)
        def _():
            lax.fori_loop(0, nb, functools.partial(key_block, with_extra_terms=False), 0)

    blk_ix = lax.broadcasted_iota(jnp.int32, (nbp, blk), 0)
    blk_f = blk_ix.astype(F32)
    aug_row = lax.broadcasted_iota(jnp.int32, (dh, blk), 0)
    qry_lane = lax.broadcasted_iota(jnp.int32, (dh, blk), 1)
    lane_k = lax.broadcasted_iota(jnp.int32, (nbp, LANES), 1)
    own_grp = c0 // grp
    key_ix = lax.broadcasted_iota(jnp.int32, (grp * blk, blk), 0) + own_grp * (grp * blk)
    qry_ix = lax.broadcasted_iota(jnp.int32, (grp * blk, blk), 1)
    gates = []
    for ti, h in streams:
        pr, hh = divmod(h, 2)
        qt_pair = qt_ref[pr * LANES:(pr + 1) * LANES, ti * blk:(ti + 1) * blk]
        is_data_k = (lane_k >= dh * hh) & (lane_k < dh * (hh + 1))
        km_hi, km_lo = _split_bf16(jnp.where(is_data_k, kmean_ref[pr, 0:nbp, :], 0.0), 2)
        gates.append(_dot(km_hi, qt_pair) + _dot(km_lo, qt_pair))
    for si, (ti, h) in enumerate(streams):
        pr, hh = divmod(h, 2)
        c = c0 + ti
        valid = blk_ix < c
        tq = (qry_lane + c * blk).astype(F32)
        gate = jnp.where(valid, gates[si], -jnp.inf)
        chosen = blk_ix == c
        for _ in range(MOBA_TOPK):
            best = jnp.max(gate, axis=0, keepdims=True)
            first = jnp.min(jnp.where(gate == best, blk_f, float(nbp)), axis=0, keepdims=True)
            pick = blk_f == first
            chosen = chosen | (pick & valid)
            gate = jnp.where(pick, -jnp.inf, gate)
        bias = jnp.where(chosen, 0.0, MASK_VALUE)
        if nbp < dh:
            bias = jnp.concatenate([bias, jnp.zeros((dh - nbp, blk), F32)], axis=0)
        slope = slope_ref[pl.ds(heads * hg + h, 1), :] * LOG2E
        t1, t2, t3 = _split_bf16(-slope * tq, 3)
        aug = jnp.where(aug_row < nb, bias, 0.0)
        aug = jnp.where((aug_row >= nb) & (aug_row < nb + 3), 1.0, aug)
        aug = jnp.where(aug_row == nb + 3, t1.astype(F32), aug)
        aug = jnp.where(aug_row == nb + 4, t2.astype(F32), aug)
        aug = jnp.where(aug_row == nb + 5, t3.astype(F32), aug).astype(BF16)
        data = qt_ref[dh * h:dh * (h + 1), ti * blk:(ti + 1) * blk]
        qaug_ref[si] = jnp.concatenate([data, aug] if hh == 0 else [aug, data], axis=0)

    def scores_into(buf_ref, g):
        start = pl.multiple_of(g * (grp * blk), grp * blk)
        col_max = []
        for si, (ti, h) in enumerate(streams):
            s = _dot(kaug_ref[h, pl.ds(start, grp * blk), :], qaug_ref[si])
            buf_ref[si] = s
            col_max.append(jnp.max(s, axis=0, keepdims=True))
        return tuple(col_max)

    def consume(buf_ref, g, ms, col_max, own=False):
        new_ms = []
        for si, (ti, h) in enumerate(streams):
            s = buf_ref[si]
            if own:
                s = jnp.where(key_ix <= qry_ix + (c0 + ti) * blk, s, MASK_VALUE)
                group_max = jnp.max(s, axis=0, keepdims=True)
            else:
                group_max = col_max[si]
            m_new = jnp.maximum(ms[si], group_max)
            alpha = jnp.exp2(ms[si] - m_new)
            p = jnp.exp2(s - m_new)
            acc_ref[si] = alpha * acc_ref[si] + _dot(vaug_ref[h, g], p.astype(BF16))
            new_ms.append(m_new)
        return tuple(new_ms)

    def write_output():
        for ti in range(tiles):
            outs = []
            for h in range(heads):
                acc = acc_ref[ti * heads + h]
                outs.append(acc[0:dh, :] / acc[dh:dh + 1, :])
            o_ref[ti * blk:(ti + 1) * blk, :] = jnp.concatenate(outs, axis=0).T.astype(o_ref.dtype)

    acc_ref[...] = jnp.zeros_like(acc_ref)
    masked_score = jnp.full((1, blk), MASK_VALUE, F32).astype(BF16).astype(F32)
    max_a0 = scores_into(sa_ref, 0)

    def two_groups(i, carry):
        ms, max_a = carry
        max_b = scores_into(sb_ref, 2 * i + 1)
        ms = consume(sa_ref, 2 * i, ms, max_a)
        max_a = scores_into(sa_ref, 2 * i + 2)
        return consume(sb_ref, 2 * i + 1, ms, max_b), max_a

    ms, max_a = lax.fori_loop(0, own_grp // 2, two_groups, ((masked_score,) * len(streams), max_a0))

    @pl.when(own_grp % 2 == 0)
    def _own_group_in_a():
        consume(sa_ref, own_grp, ms, None, own=True)
        write_output()

    @pl.when(own_grp % 2 == 1)
    def _own_group_in_b():
        scores_into(sb_ref, own_grp)
        consume(sb_ref, own_grp, consume(sa_ref, own_grp - 1, ms, max_a), None, own=True)
        write_output()


def _moba(qat, ka, vat, bsz, seq):
    t, a_width = ka.shape
    blk = MOBA_BLOCK
    assert seq % blk == 0 and seq // blk >= MOBA_TOPK
    nb = seq // blk
    assert nb + 6 <= A_HEAD_DIM
    grp = MOBA_KEY_GROUP if nb % MOBA_KEY_GROUP == 0 else 1
    heads = MOBA_HEADS_PER_STEP
    width = heads * A_HEAD_DIM
    assert heads % 2 == 0 and a_width % width == 0
    tiles = grp
    n_streams = tiles * heads
    steps = nb // tiles
    slopes = jnp.exp2(-8.0 * jnp.arange(1, A_HEADS + 1, dtype=F32) / A_HEADS)
    slope_tbl = jnp.broadcast_to(slopes[:, None], (A_HEADS, blk))
    return pl.pallas_call(
        functools.partial(_moba_kernel, nb=nb, grp=grp, heads=heads, tiles=tiles),
        grid=(a_width // width, bsz, steps),
        in_specs=[
            pl.BlockSpec((A_HEADS, blk), lambda hg, b, c: (0, 0)),
            pl.BlockSpec((width, tiles * blk), lambda hg, b, c: (hg, b * steps + c)),
            pl.BlockSpec((seq, width), lambda hg, b, c: (b, hg), pipeline_mode=pl.Buffered(1)),
            pl.BlockSpec((width, seq), lambda hg, b, c: (hg, b), pipeline_mode=pl.Buffered(1)),
        ],
        out_specs=pl.BlockSpec((tiles * blk, width), lambda hg, b, c: (b * steps + c, hg)),
        out_shape=jax.ShapeDtypeStruct((t, a_width), BF16),
        scratch_shapes=[
            pltpu.VMEM((heads, seq, LANES), BF16),
            pltpu.VMEM((heads, nb // grp, V_AUG_ROWS, grp * blk), BF16),
            pltpu.VMEM((heads // 2, LANES, LANES), F32),
            pltpu.VMEM((n_streams, LANES, blk), BF16),
            pltpu.VMEM((n_streams, V_AUG_ROWS, blk), F32),
            pltpu.VMEM((n_streams, grp * blk, blk), F32),
            pltpu.VMEM((n_streams, grp * blk, blk), F32),
        ],
        compiler_params=_compiler_params(("parallel", "arbitrary", "arbitrary")),
        name="moba",
    )(slope_tbl, qat, ka, vat)


def _mlstm_kernel(qt_ref, k_ref, vt_ref, obt_ref, grow_ref, cwq_ref, cbq_ref, cwk_ref, cbk_ref,
                  o_ref, qtail_ref, kbuf_ref, ct_ref, m_ref):
    ci = pl.program_id(1)
    L, width = k_ref.shape
    d = B_HEAD_DIM
    halo = SUBLANES_F32

    @pl.when(ci == 0)
    def _():
        qtail_ref[...] = jnp.zeros_like(qtail_ref)
        kbuf_ref[0:halo, :] = jnp.zeros((halo, width), F32)
        ct_ref[...] = jnp.zeros_like(ct_ref)
        m_ref[...] = jnp.zeros_like(m_ref)

    xq = qt_ref[...].astype(F32)
    lane = lax.broadcasted_iota(jnp.int32, (width, LANES), 1)
    tail = qtail_ref[...]
    yq = cbq_ref[...] + xq * cwq_ref[CONV_WIDTH - 1]
    for s in range(1, CONV_WIDTH):
        shifted = pltpu.roll(xq, s, axis=1)
        first = jnp.where(lane < s, pltpu.roll(tail, s, axis=1), shifted[:, :LANES])
        shifted = jnp.concatenate([first, shifted[:, LANES:]], axis=1)
        yq = yq + shifted * cwq_ref[CONV_WIDTH - 1 - s]
    qtail_ref[...] = xq[:, L - LANES:]
    q_all = (yq * jax.nn.sigmoid(yq) * (d ** -0.5)).astype(BF16)

    kbuf_ref[halo:halo + L, :] = k_ref[...].astype(F32)
    yk = cbk_ref[...]
    for j in range(CONV_WIDTH):
        off = halo - (CONV_WIDTH - 1) + j
        yk = yk + kbuf_ref[off:off + L, :] * cwk_ref[j:j + 1, :]
    kbuf_ref[0:halo, :] = kbuf_ref[L:L + halo, :]
    k_all = (yk * jax.nn.sigmoid(yk)).astype(BF16)

    def log_sigmoid(z):
        return jnp.minimum(z, 0.0) - jnp.log1p(jnp.exp(-jnp.abs(z)))

    src = lax.broadcasted_iota(jnp.int32, (L, L), 0)
    dst = lax.broadcasted_iota(jnp.int32, (L, L), 1)
    causal = src <= dst
    tri_up = causal.astype(BF16)
    gates_row = grow_ref[0]
    gates_col = gates_row.T
    lf_hi, lf_lo = _split_bf16(log_sigmoid(gates_row), 2)
    b_rows = _dot(lf_hi, tri_up) + _dot(lf_lo, tri_up)
    b_cols = b_rows.T
    pad_rows = ct_ref.shape[1] - d
    ones_rows = jnp.where(lax.broadcasted_iota(jnp.int32, (pad_rows, L), 0) == 0, 1.0, 0.0).astype(BF16)
    for hd in range(B_HEADS):
        rows = slice(hd * d, (hd + 1) * d)
        qt = q_all[rows, :]
        k = k_all[:, rows]
        vt_aug = jnp.concatenate([vt_ref[rows, :], ones_rows], axis=0)
        i_col = gates_col[:, hd:hd + 1]
        i_row = gates_row[hd:hd + 1, :]
        b_col = b_cols[:, hd + B_HEADS:hd + B_HEADS + 1]
        b_row = b_rows[hd + B_HEADS:hd + B_HEADS + 1, :]

        m_prev = m_ref[hd]
        log_inter = b_row + m_prev
        dmat = jnp.where(causal, b_row + (i_col - b_col), -jnp.inf)
        m_t = jnp.maximum(log_inter, jnp.max(dmat, axis=0, keepdims=True))
        w_inter = jnp.exp(log_inter - m_t)
        sc = _dot(k, qt) * jnp.exp(dmat - m_t)
        ct_prev = ct_ref[hd]
        inter = w_inter * _dot(ct_prev.astype(BF16), qt)
        num = inter[:d, :] + _dot(vt_ref[rows, :], sc.astype(BF16))
        den = inter[d:d + 1, :] + jnp.sum(sc, axis=0, keepdims=True)
        hidden = num / jnp.maximum(jnp.abs(den), jnp.exp(-m_t))
        out_t = jax.nn.sigmoid(obt_ref[rows, :].astype(F32)) * hidden
        o_ref[:, rows] = out_t.T.astype(o_ref.dtype)

        b_last = b_row[:, L - 1:L]
        log_old = b_last + m_prev
        log_new = b_last - b_row + i_row
        m_new = jnp.maximum(log_old, jnp.max(log_new, axis=1, keepdims=True))
        a_old = jnp.exp(log_old - m_new)
        a_new = jnp.exp(log_new - m_new)
        ct_ref[hd] = a_old * ct_prev + _dot((vt_aug.astype(F32) * a_new).astype(BF16), k)
        m_ref[hd] = m_new


def _mlstm(qbt, kb, vbt, obt, gates, conv_w, conv_b, bsz, seq):
    b_width, t = qbt.shape
    d = B_HEAD_DIM
    L = min(MLSTM_KERNEL_CHUNK, seq)
    assert seq % L == 0 and L % LANES == 0 and d == LANES and b_width == B_HEADS * d
    nc = seq // L
    gates_row = gates.reshape(bsz, seq, 2 * B_HEADS).transpose(0, 2, 1)
    cwq = jnp.broadcast_to(conv_w[:, :b_width, None].astype(F32), (CONV_WIDTH, b_width, L))
    cbq = jnp.broadcast_to(conv_b[:b_width, None].astype(F32), (b_width, L))
    cwk = conv_w[:, b_width:].astype(F32)
    cbk = conv_b[b_width:].astype(F32).reshape(1, b_width)
    tposed = lambda b, c: (0, b * nc + c)
    natural = lambda b, c: (b * nc + c, 0)
    const2 = lambda b, c: (0, 0)
    return pl.pallas_call(
        _mlstm_kernel,
        grid=(bsz, nc),
        in_specs=[
            pl.BlockSpec((b_width, L), tposed),
            pl.BlockSpec((L, b_width), natural),
            pl.BlockSpec((b_width, L), tposed),
            pl.BlockSpec((b_width, L), tposed),
            pl.BlockSpec((1, 2 * B_HEADS, L), lambda b, c: (b, 0, c)),
            pl.BlockSpec((CONV_WIDTH, b_width, L), lambda b, c: (0, 0, 0)),
            pl.BlockSpec((b_width, L), const2),
            pl.BlockSpec((CONV_WIDTH, b_width), const2),
            pl.BlockSpec((1, b_width), const2),
        ],
        out_specs=pl.BlockSpec((L, b_width), natural),
        out_shape=jax.ShapeDtypeStruct((t, b_width), BF16),
        scratch_shapes=[
            pltpu.VMEM((b_width, LANES), F32),
            pltpu.VMEM((L + SUBLANES_F32, b_width), F32),
            pltpu.VMEM((B_HEADS, d + SUBLANES_BF16, d), F32),
            pltpu.VMEM((B_HEADS, 1, 1), F32),
        ],
        compiler_params=_compiler_params(("parallel", "arbitrary")),
        name="mlstm",
    )(qbt, kb, vbt, obt, gates_row, cwq, cbq, cwk, cbk)


def _pool_kernel(x_ref, halo_ref, g_ref, w_ref, scale_ref, o_ref, sums_ref, *, tiles_per_seq):
    i = pl.program_id(0)
    tm, d = x_ref.shape
    n_win = len(POOL_WINDOWS)
    grp = d // n_win
    base = POOL_PAD + POOL_HALO
    rows = POOL_HALO + tm
    x = x_ref[...]
    seq_tile = i % tiles_per_seq
    sums_ref[:, 0:POOL_PAD, :] = jnp.zeros((n_win, POOL_PAD, d), F32)
    halo_h = _rms_normalize(halo_ref[...], g_ref[...])
    sums_ref[0, POOL_PAD:base, :] = jnp.where(seq_tile == 0, 0.0, halo_h)
    h = _rms_normalize(x, g_ref[...])
    sums_ref[0, base:base + tm, :] = h
    t1 = (lax.broadcasted_iota(jnp.int32, (tm, 1), 0) + seq_tile * tm + 1).astype(F32)
    for k, win in enumerate(POOL_WINDOWS):
        half = win // 2
        c0 = k * grp
        if k + 1 < n_win:
            both = (sums_ref[k, POOL_PAD:POOL_PAD + rows, c0:]
                    + sums_ref[k, POOL_PAD - half:POOL_PAD - half + rows, c0:])
            sums_ref[k + 1, POOL_PAD:POOL_PAD + rows, c0:] = both
            total = both[POOL_HALO:, 0:grp]
        else:
            total = (sums_ref[k, base:base + tm, c0:c0 + grp]
                     + sums_ref[k, base - half:base - half + tm, c0:c0 + grp])
        pooled = total / jnp.minimum(t1, float(win)) - h[:, c0:c0 + grp]
        y = _dot(pooled.astype(BF16), w_ref[k])
        o_ref[:, c0:c0 + grp] = x[:, c0:c0 + grp] + y * scale_ref[:, c0:c0 + grp]


def _pool(x, gain, w_grp, scale, seq):
    t, d = x.shape
    tm = min(POOL_TOKEN_TILE, seq)
    assert seq % tm == 0 and tm % POOL_HALO == 0
    assert POOL_WINDOWS == tuple(2 ** (k + 1) for k in range(len(POOL_WINDOWS)))
    assert POOL_WINDOWS[-1] <= POOL_HALO and POOL_WINDOWS[-1] // 2 <= POOL_PAD
    n_grp, grp, _ = w_grp.shape
    halo_blocks = tm // POOL_HALO
    return pl.pallas_call(
        functools.partial(_pool_kernel, tiles_per_seq=seq // tm),
        grid=(t // tm,),
        in_specs=[
            pl.BlockSpec((tm, d), lambda i: (i, 0)),
            pl.BlockSpec((POOL_HALO, d), lambda i: (jnp.maximum(i * halo_blocks - 1, 0), 0)),
            pl.BlockSpec((1, d), lambda i: (0, 0)),
            pl.BlockSpec((n_grp, grp, grp), lambda i: (0, 0, 0)),
            pl.BlockSpec((1, d), lambda i: (0, 0)),
        ],
        out_specs=pl.BlockSpec((tm, d), lambda i: (i, 0)),
        out_shape=jax.ShapeDtypeStruct((t, d), F32),
        scratch_shapes=[pltpu.VMEM((len(POOL_WINDOWS), POOL_PAD + POOL_HALO + tm, d), F32)],
        compiler_params=_compiler_params(("parallel",)),
        name="pool",
    )(x, x, gain.reshape(1, d), w_grp.astype(BF16), scale.astype(F32).reshape(1, d))


def _mixer_heads(x, gain, w_in, g_q, g_k, conv_w, conv_b, b_i, b_f, bsz, seq):
    qat, ka, vat, kb, qbt, vbt, obt, gates = _inproj(x, gain, w_in, g_q, g_k, b_i, b_f)
    ya = _moba(qat, ka, vat, bsz, seq)
    yb = _mlstm(qbt, kb, vbt, obt, gates, conv_w, conv_b, bsz, seq)
    return ya, yb


def kernel(x, norm_g, ffn_w_gate, ffn_w_up, ffn_w_down, ab_w_in, ab_w_out, ab_g_q, ab_g_k,
           ab_conv_w, ab_conv_b, ab_b_i, ab_b_f, pool_w, pool_scale):
    bsz, seq, d = x.shape
    depth = norm_g.shape[0]
    y = x.reshape(bsz * seq, d)
    w_gate, w_up, w_down = (w.astype(BF16) for w in (ffn_w_gate, ffn_w_up, ffn_w_down))
    for layer in range(depth):
        y = _ffn(y, norm_g[layer, 0], w_gate, w_up, w_down, (layer, 0))
        mixer_out = None
        if layer % 2 == 0:
            e = layer // 2
            ya, yb = _mixer_heads(y, norm_g[layer, 1], ab_w_in[e], ab_g_q[e], ab_g_k[e],
                                  ab_conv_w[e], ab_conv_b[e], ab_b_i[e], ab_b_f[e], bsz, seq)
            mixer_out = (ya, yb, ab_w_out[e])
        else:
            o = layer // 2
            y = _pool(y, norm_g[layer, 1], pool_w[o], pool_scale[o], seq)
        y = _ffn(y, norm_g[layer, 2], w_gate, w_up, w_down, (layer, 1), mixer_out=mixer_out)
    return y.reshape(bsz, seq, d)
```

```python
import functools

import jax
import jax.numpy as jnp
from jax import lax
from jax.experimental import pallas as pl
from jax.experimental.pallas import tpu as pltpu

F32 = jnp.float32
BF16 = jnp.bfloat16

LANES = 128
SUBLANES_F32 = 8
SUBLANES_BF16 = 16
MXU_WIDTH = 256
VMEM_BYTES = 64 * 1024 * 1024
VMEM_LIMIT_BYTES = VMEM_BYTES * 7 // 8

RMS_EPS = 1e-6
A_HEADS = 8
A_HEAD_DIM = 64
MOBA_BLOCK = 256
MOBA_TOPK = 3
B_HEADS = 4
B_HEAD_DIM = 128
CONV_WIDTH = 4
POOL_WINDOWS = (2, 4, 8, 16)

MOBA_KEY_GROUP = 2
MOBA_HEADS_PER_STEP = 4
MLSTM_KERNEL_CHUNK = 256
POOL_HALO = 16
POOL_PAD = SUBLANES_F32
MASK_VALUE = -1e30
LOG2E = 1.4426950408889634
V_AUG_ROWS = A_HEAD_DIM + SUBLANES_BF16
FFN_TOKEN_TILE = 1024
FFN_SUB_TILE = 512
FFN_HIDDEN_TILE = MXU_WIDTH
PROJ_TOKEN_TILE = 1024
POOL_TOKEN_TILE = 1024


def _compiler_params(semantics):
    return pltpu.CompilerParams(dimension_semantics=semantics,
                                vmem_limit_bytes=VMEM_LIMIT_BYTES)


def _rms_normalize(x, gain):
    ms = jnp.mean(x * x, axis=-1, keepdims=True)
    return x * lax.rsqrt(ms + RMS_EPS) * gain


def _dot(a, b):
    return jnp.dot(a, b, preferred_element_type=F32)


def _dot_nt(a, b):
    return lax.dot_general(a, b, (((1,), (1,)), ((), ())), preferred_element_type=F32)


def _split_bf16(x, parts):
    out = []
    rem = x
    for _ in range(parts):
        p = rem.astype(BF16)
        out.append(p)
        rem = rem - p.astype(F32)
    return out


def _ffn_kernel(x_ref, g_ref, wg_ref, wu_ref, wd_ref, *rest, hidden_tile, sub_tile):
    o_ref = rest[-1]
    d_ff = wg_ref.shape[1]
    for r0 in range(0, x_ref.shape[0], sub_tile):
        x = x_ref[r0:r0 + sub_tile, :]
        if len(rest) > 1:
            ya_ref, yb_ref, wa_ref, wb_ref = rest[:-1]
            x = (x + _dot(ya_ref[r0:r0 + sub_tile, :], wa_ref[...])
                 + _dot(yb_ref[r0:r0 + sub_tile, :], wb_ref[...]))
        h = _rms_normalize(x, g_ref[...]).astype(BF16)
        acc = None
        for c0 in range(0, d_ff, hidden_tile):
            gate = _dot(h, wg_ref[:, c0:c0 + hidden_tile])
            up = _dot(h, wu_ref[:, c0:c0 + hidden_tile])
            act = (gate * jax.nn.sigmoid(gate) * up).astype(BF16)
            part = _dot(act, wd_ref[c0:c0 + hidden_tile, :])
            acc = part if acc is None else acc + part
        o_ref[r0:r0 + sub_tile, :] = x + 0.5 * acc


def _resident(shape):
    return pl.BlockSpec(shape, lambda *_: (0,) * len(shape), pipeline_mode=pl.Buffered(1))


def _ffn(x, gain, w_gate, w_up, w_down, which, mixer_out=None):
    t, d = x.shape
    d_ff = w_gate.shape[-1]
    tm, tf = FFN_TOKEN_TILE, FFN_HIDDEN_TILE
    assert t % tm == 0 and d_ff % tf == 0 and tm % FFN_SUB_TILE == 0
    row = lambda i: (i, 0)
    picked = lambda r, c: pl.BlockSpec((None, None, r, c), lambda i: (*which, 0, 0),
                                       pipeline_mode=pl.Buffered(1))
    in_specs = [pl.BlockSpec((tm, d), row), _resident((1, d)), picked(d, d_ff), picked(d, d_ff),
                picked(d_ff, d)]
    args = [x, gain.reshape(1, d), w_gate, w_up, w_down]
    if mixer_out is not None:
        ya, yb, w_out = mixer_out
        wa, wb = ya.shape[1], yb.shape[1]
        in_specs += [pl.BlockSpec((tm, wa), row), pl.BlockSpec((tm, wb), row),
                     _resident((wa, d)), _resident((wb, d))]
        args += [ya, yb, w_out[:wa].astype(BF16), w_out[wa:].astype(BF16)]
    return pl.pallas_call(
        functools.partial(_ffn_kernel, hidden_tile=tf, sub_tile=FFN_SUB_TILE),
        grid=(t // tm,),
        in_specs=in_specs,
        out_specs=pl.BlockSpec((tm, d), row),
        out_shape=jax.ShapeDtypeStruct((t, d), F32),
        compiler_params=_compiler_params(("parallel",)),
        name="ffn",
    )(*args)


def _inproj_kernel(x_ref, g_ref, w_ref, wqt_ref, wrt_ref, wgate_ref, gqcol_ref, gk_ref, grp_ref,
                   gbias_ref, qat_ref, ka_ref, vat_ref, kb_ref, qbt_ref, vbt_ref, obt_ref, gates_ref,
                   *, a_width, b_width):
    h = _rms_normalize(x_ref[...], g_ref[...]).astype(BF16)

    qt = _dot_nt(wqt_ref[...], h)
    for hd in range(a_width // A_HEAD_DIM):
        rows = slice(hd * A_HEAD_DIM, (hd + 1) * A_HEAD_DIM)
        y = qt[rows, :]
        ms = jnp.mean(y * y, axis=0, keepdims=True)
        gain = gqcol_ref[rows, :] * (A_HEAD_DIM ** -0.5 * LOG2E)
        qat_ref[rows, :] = (y * lax.rsqrt(ms + RMS_EPS) * gain).astype(BF16)

    y = _dot(h, w_ref[:, 0:a_width])
    sq_hi, sq_lo = _split_bf16(y * y, 2)
    ssq = _dot(sq_hi, grp_ref[...]) + _dot(sq_lo, grp_ref[...])
    ka_ref[...] = (y * lax.rsqrt(ssq * (1.0 / A_HEAD_DIM) + RMS_EPS) * gk_ref[...]).astype(BF16)

    kb_ref[...] = _dot(h, w_ref[:, a_width:a_width + b_width]).astype(BF16)
    rest_t = _dot_nt(wrt_ref[...], h)
    vat_ref[...] = rest_t[0:a_width, :].astype(BF16)
    for piece, out_ref in enumerate((qbt_ref, vbt_ref, obt_ref)):
        r0 = a_width + piece * b_width
        out_ref[...] = rest_t[r0:r0 + b_width, :].astype(BF16)
    gates = _dot(h, wgate_ref[...]) + gbias_ref[...]
    gates_ref[...] = gates[:, :2 * B_HEADS]


def _inproj(x, gain, w_in, g_q, g_k, b_i, b_f):
    t, d = x.shape
    a_width = A_HEADS * A_HEAD_DIM
    b_width = B_HEADS * B_HEAD_DIM
    n_main = a_width + b_width
    o_qb = 3 * a_width
    o_ib = 3 * a_width + 3 * b_width
    tm = PROJ_TOKEN_TILE
    assert t % tm == 0
    w_main = jnp.concatenate([w_in[:, a_width:2 * a_width], w_in[:, o_qb + b_width:o_qb + 2 * b_width]],
                             axis=1).astype(BF16)
    w_qt = w_in[:, :a_width].T.astype(BF16)
    w_rt = jnp.concatenate([w_in[:, 2 * a_width:3 * a_width], w_in[:, o_qb:o_qb + b_width],
                            w_in[:, o_qb + 2 * b_width:o_ib], w_in[:, o_ib + 2 * B_HEADS:]],
                           axis=1).T.astype(BF16)
    w_gate = jnp.pad(w_in[:, o_ib:o_ib + 2 * B_HEADS], ((0, 0), (0, LANES - 2 * B_HEADS))).astype(BF16)
    gate_bias = jnp.pad(jnp.concatenate([b_i, b_f]).astype(F32), (0, LANES - 2 * B_HEADS)).reshape(1, LANES)
    head_id = jnp.arange(a_width) // A_HEAD_DIM
    grp = (head_id[:, None] == head_id[None, :]).astype(BF16)
    gq_col = jnp.tile(g_q.astype(F32), A_HEADS).reshape(a_width, 1)
    gk = jnp.tile(g_k.astype(F32), A_HEADS).reshape(1, a_width)
    const = lambda i: (0, 0)
    row = lambda i: (i, 0)
    col = lambda i: (0, i)
    outs = pl.pallas_call(
        functools.partial(_inproj_kernel, a_width=a_width, b_width=b_width),
        grid=(t // tm,),
        in_specs=[
            pl.BlockSpec((tm, d), row),
            pl.BlockSpec((1, d), const),
            pl.BlockSpec((d, n_main), const),
            pl.BlockSpec((a_width, d), const),
            pl.BlockSpec((a_width + 3 * b_width, d), const),
            pl.BlockSpec((d, LANES), const),
            pl.BlockSpec((a_width, 1), const),
            pl.BlockSpec((1, a_width), const),
            pl.BlockSpec((a_width, a_width), const),
            pl.BlockSpec((1, LANES), const),
        ],
        out_specs=[
            pl.BlockSpec((a_width, tm), col),
            pl.BlockSpec((tm, a_width), row),
            pl.BlockSpec((a_width, tm), col),
            pl.BlockSpec((tm, b_width), row),
            pl.BlockSpec((b_width, tm), col),
            pl.BlockSpec((b_width, tm), col),
            pl.BlockSpec((b_width, tm), col),
            pl.BlockSpec((tm, 2 * B_HEADS), row),
        ],
        out_shape=[
            jax.ShapeDtypeStruct((a_width, t), BF16),
            jax.ShapeDtypeStruct((t, a_width), BF16),
            jax.ShapeDtypeStruct((a_width, t), BF16),
            jax.ShapeDtypeStruct((t, b_width), BF16),
            jax.ShapeDtypeStruct((b_width, t), BF16),
            jax.ShapeDtypeStruct((b_width, t), BF16),
            jax.ShapeDtypeStruct((b_width, t), BF16),
            jax.ShapeDtypeStruct((t, 2 * B_HEADS), F32),
        ],
        compiler_params=_compiler_params(("parallel",)),
        name="inproj",
    )(x, gain.reshape(1, d), w_main, w_qt, w_rt, w_gate, gq_col, gk, grp, gate_bias)
    return outs


def _moba_kernel(slope_ref, qt_ref, k_ref, vt_ref, o_ref, kaug_ref, vaug_ref, kmean_ref, qaug_ref, acc_ref,
                 sa_ref, sb_ref, *, nb, grp, heads, tiles):
    hg = pl.program_id(0)
    batch = pl.program_id(1)
    c0 = pl.program_id(2) * tiles
    blk = MOBA_BLOCK
    dh = A_HEAD_DIM
    nbp = pl.cdiv(nb, SUBLANES_BF16) * SUBLANES_BF16
    streams = [(ti, h) for ti in range(tiles) for h in range(heads)]

    @pl.when(c0 == 0)
    def _build_key_value_side():
        ones_rows = jnp.where(lax.broadcasted_iota(jnp.int32, (V_AUG_ROWS - dh, blk), 0) == 0,
                              1.0, 0.0).astype(BF16)
        for h in range(heads):
            for j in range(nb):
                g, off = divmod(j, grp)
                vaug_ref[h, g, 0:dh, off * blk:(off + 1) * blk] = vt_ref[dh * h:dh * (h + 1),
                                                                         j * blk:(j + 1) * blk]
                vaug_ref[h, g, dh:V_AUG_ROWS, off * blk:(off + 1) * blk] = ones_rows
        kmean_ref[...] = jnp.zeros_like(kmean_ref)
        lane = lax.broadcasted_iota(jnp.int32, (blk, LANES), 1)
        row = lax.broadcasted_iota(jnp.int32, (blk, LANES), 0)

        def key_block(j, carry, *, with_extra_terms):
            start = pl.multiple_of(j * blk, blk)
            pos = (row + j * blk).astype(F32)
            for pr in range(heads // 2):
                kb = k_ref[pl.ds(start, blk), pr * LANES:(pr + 1) * LANES]
                kmean_ref[pr, pl.ds(j, 1), :] = jnp.mean(kb.astype(F32), axis=0, keepdims=True)
                for hh in range(2):
                    h = 2 * pr + hh
                    is_data = (lane >= dh * hh) & (lane < dh * (hh + 1))
                    if with_extra_terms:
                        slope = slope_ref[pl.ds(heads * hg + h, 1), :][:, 0:LANES] * LOG2E
                        p1, p2, p3 = _split_bf16(slope * pos, 3)
                        rel = lane - dh * (1 - hh)
                        aug = jnp.where(rel == j, 1.0, 0.0)
                        aug = jnp.where(rel == nb, p1.astype(F32), aug)
                        aug = jnp.where(rel == nb + 1, p2.astype(F32), aug)
                        aug = jnp.where(rel == nb + 2, p3.astype(F32), aug)
                        aug = jnp.where((rel >= nb + 3) & (rel < nb + 6), 1.0, aug).astype(BF16)
                    else:
                        aug = kaug_ref[h, pl.ds(start, blk), :]
                    kaug_ref[h, pl.ds(start, blk), :] = jnp.where(is_data, kb, aug)
            return carry

        @pl.when(batch == 0)
        def _():
            lax.fori_loop(0, nb, functools.partial(key_block, with_extra_terms=True), 0)

        @pl.when(batch != 0)
        def _():
            lax.fori_loop(0, nb, functools.partial(key_block, with_extra_terms=False), 0)

    blk_ix = lax.broadcasted_iota(jnp.int32, (nbp, blk), 0)
    blk_f = blk_ix.astype(F32)
    aug_row = lax.broadcasted_iota(jnp.int32, (dh, blk), 0)
    qry_lane = lax.broadcasted_iota(jnp.int32, (dh, blk), 1)
    lane_k = lax.broadcasted_iota(jnp.int32, (nbp, LANES), 1)
    own_grp = c0 // grp
    key_ix = lax.broadcasted_iota(jnp.int32, (grp * blk, blk), 0) + own_grp * (grp * blk)
    qry_ix = lax.broadcasted_iota(jnp.int32, (grp * blk, blk), 1)
    gates = []
    for ti, h in streams:
        pr, hh = divmod(h, 2)
        qt_pair = qt_ref[pr * LANES:(pr + 1) * LANES, ti * blk:(ti + 1) * blk]
        is_data_k = (lane_k >= dh * hh) & (lane_k < dh * (hh + 1))
        km_hi, km_lo = _split_bf16(jnp.where(is_data_k, kmean_ref[pr, 0:nbp, :], 0.0), 2)
        gates.append(_dot(km_hi, qt_pair) + _dot(km_lo, qt_pair))
    for si, (ti, h) in enumerate(streams):
        pr, hh = divmod(h, 2)
        c = c0 + ti
        valid = blk_ix < c
        tq = (qry_lane + c * blk).astype(F32)
        gate = jnp.where(valid, gates[si], -jnp.inf)
        chosen = blk_ix == c
        for _ in range(MOBA_TOPK):
            best = jnp.max(gate, axis=0, keepdims=True)
            first = jnp.min(jnp.where(gate == best, blk_f, float(nbp)), axis=0, keepdims=True)
            pick = blk_f == first
            chosen = chosen | (pick & valid)
            gate = jnp.where(pick, -jnp.inf, gate)
        bias = jnp.where(chosen, 0.0, MASK_VALUE)
        if nbp < dh:
            bias = jnp.concatenate([bias, jnp.zeros((dh - nbp, blk), F32)], axis=0)
        slope = slope_ref[pl.ds(heads * hg + h, 1), :] * LOG2E
        t1, t2, t3 = _split_bf16(-slope * tq, 3)
        aug = jnp.where(aug_row < nb, bias, 0.0)
        aug = jnp.where((aug_row >= nb) & (aug_row < nb + 3), 1.0, aug)
        aug = jnp.where(aug_row == nb + 3, t1.astype(F32), aug)
        aug = jnp.where(aug_row == nb + 4, t2.astype(F32), aug)
        aug = jnp.where(aug_row == nb + 5, t3.astype(F32), aug).astype(BF16)
        data = qt_ref[dh * h:dh * (h + 1), ti * blk:(ti + 1) * blk]
        qaug_ref[si] = jnp.concatenate([data, aug] if hh == 0 else [aug, data], axis=0)

    def scores_into(buf_ref, g):
        start = pl.multiple_of(g * (grp * blk), grp * blk)
        col_max = []
        for si, (ti, h) in enumerate(streams):
            s = _dot(kaug_ref[h, pl.ds(start, grp * blk), :], qaug_ref[si])
            buf_ref[si] = s
            col_max.append(jnp.max(s, axis=0, keepdims=True))
        return tuple(col_max)

    def consume(buf_ref, g, ms, col_max, own=False):
        new_ms = []
        for si, (ti, h) in enumerate(streams):
            s = buf_ref[si]
            if own:
                s = jnp.where(key_ix <= qry_ix + (c0 + ti) * blk, s, MASK_VALUE)
                group_max = jnp.max(s, axis=0, keepdims=True)
            else:
                group_max = col_max[si]
            m_new = jnp.maximum(ms[si], group_max)
            alpha = jnp.exp2(ms[si] - m_new)
            p = jnp.exp2(s - m_new)
            acc_ref[si] = alpha * acc_ref[si] + _dot(vaug_ref[h, g], p.astype(BF16))
            new_ms.append(m_new)
        return tuple(new_ms)

    def write_output():
        for ti in range(tiles):
            outs = []
            for h in range(heads):
                acc = acc_ref[ti * heads + h]
                outs.append(acc[0:dh, :] / acc[dh:dh + 1, :])
            o_ref[ti * blk:(ti + 1) * blk, :] = jnp.concatenate(outs, axis=0).T.astype(o_ref.dtype)

    acc_ref[...] = jnp.zeros_like(acc_ref)
    masked_score = jnp.full((1, blk), MASK_VALUE, F32).astype(BF16).astype(F32)
    max_a0 = scores_into(sa_ref, 0)

    def two_groups(i, carry):
        ms, max_a = carry
        max_b = scores_into(sb_ref, 2 * i + 1)
        ms = consume(sa_ref, 2 * i, ms, max_a)
        max_a = scores_into(sa_ref, 2 * i + 2)
        return consume(sb_ref, 2 * i + 1, ms, max_b), max_a

    ms, max_a = lax.fori_loop(0, own_grp // 2, two_groups, ((masked_score,) * len(streams), max_a0))

    @pl.when(own_grp % 2 == 0)
    def _own_group_in_a():
        consume(sa_ref, own_grp, ms, None, own=True)
        write_output()

    @pl.when(own_grp % 2 == 1)
    def _own_group_in_b():
        scores_into(sb_ref, own_grp)
        consume(sb_ref, own_grp, consume(sa_ref, own_grp - 1, ms, max_a), None, own=True)
        write_output()


def _moba(qat, ka, vat, bsz, seq):
    t, a_width = ka.shape
    blk = MOBA_BLOCK
    assert seq % blk == 0 and seq // blk >= MOBA_TOPK
    nb = seq // blk
    assert nb + 6 <= A_HEAD_DIM
    grp = MOBA_KEY_GROUP if nb % MOBA_KEY_GROUP == 0 else 1
    heads = MOBA_HEADS_PER_STEP
    width = heads * A_HEAD_DIM
    assert heads % 2 == 0 and a_width % width == 0
    tiles = grp
    n_streams = tiles * heads
    steps = nb // tiles
    slopes = jnp.exp2(-8.0 * jnp.arange(1, A_HEADS + 1, dtype=F32) / A_HEADS)
    slope_tbl = jnp.broadcast_to(slopes[:, None], (A_HEADS, blk))
    return pl.pallas_call(
        functools.partial(_moba_kernel, nb=nb, grp=grp, heads=heads, tiles=tiles),
        grid=(a_width // width, bsz, steps),
        in_specs=[
            pl.BlockSpec((A_HEADS, blk), lambda hg, b, c: (0, 0)),
            pl.BlockSpec((width, tiles * blk), lambda hg, b, c: (hg, b * steps + c)),
            pl.BlockSpec((seq, width), lambda hg, b, c: (b, hg)),
            pl.BlockSpec((width, seq), lambda hg, b, c: (hg, b)),
        ],
        out_specs=pl.BlockSpec((tiles * blk, width), lambda hg, b, c: (b * steps + c, hg)),
        out_shape=jax.ShapeDtypeStruct((t, a_width), BF16),
        scratch_shapes=[
            pltpu.VMEM((heads, seq, LANES), BF16),
            pltpu.VMEM((heads, nb // grp, V_AUG_ROWS, grp * blk), BF16),
            pltpu.VMEM((heads // 2, LANES, LANES), F32),
            pltpu.VMEM((n_streams, LANES, blk), BF16),
            pltpu.VMEM((n_streams, V_AUG_ROWS, blk), F32),
            pltpu.VMEM((n_streams, grp * blk, blk), F32),
            pltpu.VMEM((n_streams, grp * blk, blk), F32),
        ],
        compiler_params=_compiler_params(("parallel", "arbitrary", "arbitrary")),
        name="moba",
    )(slope_tbl, qat, ka, vat)


def _mlstm_kernel(qt_ref, k_ref, vt_ref, obt_ref, grow_ref, cwq_ref, cbq_ref, cwk_ref, cbk_ref,
                  o_ref, qtail_ref, kbuf_ref, ct_ref, m_ref):
    ci = pl.program_id(1)
    L, width = k_ref.shape
    d = B_HEAD_DIM
    halo = SUBLANES_F32

    @pl.when(ci == 0)
    def _():
        qtail_ref[...] = jnp.zeros_like(qtail_ref)
        kbuf_ref[0:halo, :] = jnp.zeros((halo, width), F32)
        ct_ref[...] = jnp.zeros_like(ct_ref)
        m_ref[...] = jnp.zeros_like(m_ref)

    xq = qt_ref[...].astype(F32)
    lane = lax.broadcasted_iota(jnp.int32, (width, LANES), 1)
    tail = qtail_ref[...]
    yq = cbq_ref[...] + xq * cwq_ref[CONV_WIDTH - 1]
    for s in range(1, CONV_WIDTH):
        shifted = pltpu.roll(xq, s, axis=1)
        first = jnp.where(lane < s, pltpu.roll(tail, s, axis=1), shifted[:, :LANES])
        shifted = jnp.concatenate([first, shifted[:, LANES:]], axis=1)
        yq = yq + shifted * cwq_ref[CONV_WIDTH - 1 - s]
    qtail_ref[...] = xq[:, L - LANES:]
    q_all = (yq * jax.nn.sigmoid(yq) * (d ** -0.5)).astype(BF16)

    kbuf_ref[halo:halo + L, :] = k_ref[...].astype(F32)
    yk = cbk_ref[...]
    for j in range(CONV_WIDTH):
        off = halo - (CONV_WIDTH - 1) + j
        yk = yk + kbuf_ref[off:off + L, :] * cwk_ref[j:j + 1, :]
    kbuf_ref[0:halo, :] = kbuf_ref[L:L + halo, :]
    k_all = (yk * jax.nn.sigmoid(yk)).astype(BF16)

    def log_sigmoid(z):
        return jnp.minimum(z, 0.0) - jnp.log1p(jnp.exp(-jnp.abs(z)))

    src = lax.broadcasted_iota(jnp.int32, (L, L), 0)
    dst = lax.broadcasted_iota(jnp.int32, (L, L), 1)
    causal = src <= dst
    tri_up = causal.astype(BF16)
    gates_row = grow_ref[0]
    gates_col = gates_row.T
    lf_hi, lf_lo = _split_bf16(log_sigmoid(gates_row), 2)
    b_rows = _dot(lf_hi, tri_up) + _dot(lf_lo, tri_up)
    b_cols = b_rows.T
    pad_rows = ct_ref.shape[1] - d
    ones_rows = jnp.where(lax.broadcasted_iota(jnp.int32, (pad_rows, L), 0) == 0, 1.0, 0.0).astype(BF16)
    for hd in range(B_HEADS):
        rows = slice(hd * d, (hd + 1) * d)
        qt = q_all[rows, :]
        k = k_all[:, rows]
        vt_aug = jnp.concatenate([vt_ref[rows, :], ones_rows], axis=0)
        i_col = gates_col[:, hd:hd + 1]
        i_row = gates_row[hd:hd + 1, :]
        b_col = b_cols[:, hd + B_HEADS:hd + B_HEADS + 1]
        b_row = b_rows[hd + B_HEADS:hd + B_HEADS + 1, :]

        m_prev = m_ref[hd]
        log_inter = b_row + m_prev
        dmat = jnp.where(causal, b_row + (i_col - b_col), -jnp.inf)
        m_t = jnp.maximum(log_inter, jnp.max(dmat, axis=0, keepdims=True))
        w_inter = jnp.exp(log_inter - m_t)
        sc = _dot(k, qt) * jnp.exp(dmat - m_t)
        ct_prev = ct_ref[hd]
        inter = w_inter * _dot(ct_prev.astype(BF16), qt)
        num = inter[:d, :] + _dot(vt_ref[rows, :], sc.astype(BF16))
        den = inter[d:d + 1, :] + jnp.sum(sc, axis=0, keepdims=True)
        hidden = num / jnp.maximum(jnp.abs(den), jnp.exp(-m_t))
        out_t = jax.nn.sigmoid(obt_ref[rows, :].astype(F32)) * hidden
        o_ref[:, rows] = out_t.T.astype(o_ref.dtype)

        b_last = b_row[:, L - 1:L]
        log_old = b_last + m_prev
        log_new = b_last - b_row + i_row
        m_new = jnp.maximum(log_old, jnp.max(log_new, axis=1, keepdims=True))
        a_old = jnp.exp(log_old - m_new)
        a_new = jnp.exp(log_new - m_new)
        ct_ref[hd] = a_old * ct_prev + _dot((vt_aug.astype(F32) * a_new).astype(BF16), k)
        m_ref[hd] = m_new


def _mlstm(qbt, kb, vbt, obt, gates, conv_w, conv_b, bsz, seq):
    b_width, t = qbt.shape
    d = B_HEAD_DIM
    L = min(MLSTM_KERNEL_CHUNK, seq)
    assert seq % L == 0 and L % LANES == 0 and d == LANES and b_width == B_HEADS * d
    nc = seq // L
    gates_row = gates.reshape(bsz, seq, 2 * B_HEADS).transpose(0, 2, 1)
    cwq = jnp.broadcast_to(conv_w[:, :b_width, None].astype(F32), (CONV_WIDTH, b_width, L))
    cbq = jnp.broadcast_to(conv_b[:b_width, None].astype(F32), (b_width, L))
    cwk = conv_w[:, b_width:].astype(F32)
    cbk = conv_b[b_width:].astype(F32).reshape(1, b_width)
    tposed = lambda b, c: (0, b * nc + c)
    natural = lambda b, c: (b * nc + c, 0)
    const2 = lambda b, c: (0, 0)
    return pl.pallas_call(
        _mlstm_kernel,
        grid=(bsz, nc),
        in_specs=[
            pl.BlockSpec((b_width, L), tposed),
            pl.BlockSpec((L, b_width), natural),
            pl.BlockSpec((b_width, L), tposed),
            pl.BlockSpec((b_width, L), tposed),
            pl.BlockSpec((1, 2 * B_HEADS, L), lambda b, c: (b, 0, c)),
            pl.BlockSpec((CONV_WIDTH, b_width, L), lambda b, c: (0, 0, 0)),
            pl.BlockSpec((b_width, L), const2),
            pl.BlockSpec((CONV_WIDTH, b_width), const2),
            pl.BlockSpec((1, b_width), const2),
        ],
        out_specs=pl.BlockSpec((L, b_width), natural),
        out_shape=jax.ShapeDtypeStruct((t, b_width), BF16),
        scratch_shapes=[
            pltpu.VMEM((b_width, LANES), F32),
            pltpu.VMEM((L + SUBLANES_F32, b_width), F32),
            pltpu.VMEM((B_HEADS, d + SUBLANES_BF16, d), F32),
            pltpu.VMEM((B_HEADS, 1, 1), F32),
        ],
        compiler_params=_compiler_params(("parallel", "arbitrary")),
        name="mlstm",
    )(qbt, kb, vbt, obt, gates_row, cwq, cbq, cwk, cbk)


def _pool_kernel(x_ref, halo_ref, g_ref, w_ref, scale_ref, o_ref, sums_ref, *, tiles_per_seq):
    i = pl.program_id(0)
    tm, d = x_ref.shape
    n_win = len(POOL_WINDOWS)
    grp = d // n_win
    base = POOL_PAD + POOL_HALO
    rows = POOL_HALO + tm
    x = x_ref[...]
    seq_tile = i % tiles_per_seq
    sums_ref[:, 0:POOL_PAD, :] = jnp.zeros((n_win, POOL_PAD, d), F32)
    halo_h = _rms_normalize(halo_ref[...], g_ref[...])
    sums_ref[0, POOL_PAD:base, :] = jnp.where(seq_tile == 0, 0.0, halo_h)
    h = _rms_normalize(x, g_ref[...])
    sums_ref[0, base:base + tm, :] = h
    t1 = (lax.broadcasted_iota(jnp.int32, (tm, 1), 0) + seq_tile * tm + 1).astype(F32)
    for k, win in enumerate(POOL_WINDOWS):
        half = win // 2
        c0 = k * grp
        if k + 1 < n_win:
            both = (sums_ref[k, POOL_PAD:POOL_PAD + rows, c0:]
                    + sums_ref[k, POOL_PAD - half:POOL_PAD - half + rows, c0:])
            sums_ref[k + 1, POOL_PAD:POOL_PAD + rows, c0:] = both
            total = both[POOL_HALO:, 0:grp]
        else:
            total = (sums_ref[k, base:base + tm, c0:c0 + grp]
                     + sums_ref[k, base - half:base - half + tm, c0:c0 + grp])
        pooled = total / jnp.minimum(t1, float(win)) - h[:, c0:c0 + grp]
        y = _dot(pooled.astype(BF16), w_ref[k])
        o_ref[:, c0:c0 + grp] = x[:, c0:c0 + grp] + y * scale_ref[:, c0:c0 + grp]


def _pool(x, gain, w_grp, scale, seq):
    t, d = x.shape
    tm = min(POOL_TOKEN_TILE, seq)
    assert seq % tm == 0 and tm % POOL_HALO == 0
    assert POOL_WINDOWS == tuple(2 ** (k + 1) for k in range(len(POOL_WINDOWS)))
    assert POOL_WINDOWS[-1] <= POOL_HALO and POOL_WINDOWS[-1] // 2 <= POOL_PAD
    n_grp, grp, _ = w_grp.shape
    halo_blocks = tm // POOL_HALO
    return pl.pallas_call(
        functools.partial(_pool_kernel, tiles_per_seq=seq // tm),
        grid=(t // tm,),
        in_specs=[
            pl.BlockSpec((tm, d), lambda i: (i, 0)),
            pl.BlockSpec((POOL_HALO, d), lambda i: (jnp.maximum(i * halo_blocks - 1, 0), 0)),
            pl.BlockSpec((1, d), lambda i: (0, 0)),
            pl.BlockSpec((n_grp, grp, grp), lambda i: (0, 0, 0)),
            pl.BlockSpec((1, d), lambda i: (0, 0)),
        ],
        out_specs=pl.BlockSpec((tm, d), lambda i: (i, 0)),
        out_shape=jax.ShapeDtypeStruct((t, d), F32),
        scratch_shapes=[pltpu.VMEM((len(POOL_WINDOWS), POOL_PAD + POOL_HALO + tm, d), F32)],
        compiler_params=_compiler_params(("parallel",)),
        name="pool",
    )(x, x, gain.reshape(1, d), w_grp.astype(BF16), scale.astype(F32).reshape(1, d))


def _mixer_heads(x, gain, w_in, g_q, g_k, conv_w, conv_b, b_i, b_f, bsz, seq):
    qat, ka, vat, kb, qbt, vbt, obt, gates = _inproj(x, gain, w_in, g_q, g_k, b_i, b_f)
    ya = _moba(qat, ka, vat, bsz, seq)
    yb = _mlstm(qbt, kb, vbt, obt, gates, conv_w, conv_b, bsz, seq)
    return ya, yb


def kernel(x, norm_g, ffn_w_gate, ffn_w_up, ffn_w_down, ab_w_in, ab_w_out, ab_g_q, ab_g_k,
           ab_conv_w, ab_conv_b, ab_b_i, ab_b_f, pool_w, pool_scale):
    bsz, seq, d = x.shape
    depth = norm_g.shape[0]
    y = x.reshape(bsz * seq, d)
    w_gate, w_up, w_down = (w.astype(BF16) for w in (ffn_w_gate, ffn_w_up, ffn_w_down))
    for layer in range(depth):
        y = _ffn(y, norm_g[layer, 0], w_gate, w_up, w_down, (layer, 0))
        mixer_out = None
        if layer % 2 == 0:
            e = layer // 2
            ya, yb = _mixer_heads(y, norm_g[layer, 1], ab_w_in[e], ab_g_q[e], ab_g_k[e],
                                  ab_conv_w[e], ab_conv_b[e], ab_b_i[e], ab_b_f[e], bsz, seq)
            mixer_out = (ya, yb, ab_w_out[e])
        else:
            o = layer // 2
            y = _pool(y, norm_g[layer, 1], pool_w[o], pool_scale[o], seq)
        y = _ffn(y, norm_g[layer, 2], w_gate, w_up, w_down, (layer, 1), mixer_out=mixer_out)
    return y.reshape(bsz, seq, d)
```

```python
import functools

import jax
import jax.numpy as jnp
from jax import lax
from jax.experimental import pallas as pl
from jax.experimental.pallas import tpu as pltpu

F32 = jnp.float32
BF16 = jnp.bfloat16

LANES = 128
SUBLANES_F32 = 8
SUBLANES_BF16 = 16
MXU_WIDTH = 256
VMEM_BYTES = 64 * 1024 * 1024
VMEM_LIMIT_BYTES = VMEM_BYTES * 7 // 8

RMS_EPS = 1e-6
A_HEADS = 8
A_HEAD_DIM = 64
MOBA_BLOCK = 256
MOBA_TOPK = 3
B_HEADS = 4
B_HEAD_DIM = 128
CONV_WIDTH = 4
POOL_WINDOWS = (2, 4, 8, 16)

MOBA_KEY_GROUP = 2
MOBA_HEADS_PER_STEP = 4
MLSTM_KERNEL_CHUNK = 256
POOL_HALO = 16
POOL_PAD = SUBLANES_F32
MASK_VALUE = -1e30
LOG2E = 1.4426950408889634
V_AUG_ROWS = A_HEAD_DIM + SUBLANES_BF16
FFN_TOKEN_TILE = 1024
FFN_SUB_TILE = 512
FFN_HIDDEN_TILE = MXU_WIDTH
PROJ_TOKEN_TILE = 1024
POOL_TOKEN_TILE = 1024


def _compiler_params(semantics):
    return pltpu.CompilerParams(dimension_semantics=semantics,
                                vmem_limit_bytes=VMEM_LIMIT_BYTES)


def _rms_normalize(x, gain):
    ms = jnp.mean(x * x, axis=-1, keepdims=True)
    return x * lax.rsqrt(ms + RMS_EPS) * gain


def _dot(a, b):
    return jnp.dot(a, b, preferred_element_type=F32)


def _dot_nt(a, b):
    return lax.dot_general(a, b, (((1,), (1,)), ((), ())), preferred_element_type=F32)


def _split_bf16(x, parts):
    out = []
    rem = x
    for _ in range(parts):
        p = rem.astype(BF16)
        out.append(p)
        rem = rem - p.astype(F32)
    return out


def _ffn_kernel(x_ref, g_ref, wg_ref, wu_ref, wd_ref, *rest, hidden_tile, sub_tile):
    o_ref = rest[-1]
    d_ff = wg_ref.shape[1]
    for r0 in range(0, x_ref.shape[0], sub_tile):
        x = x_ref[r0:r0 + sub_tile, :]
        if len(rest) > 1:
            ya_ref, yb_ref, wa_ref, wb_ref = rest[:-1]
            x = (x + _dot(ya_ref[r0:r0 + sub_tile, :], wa_ref[...])
                 + _dot(yb_ref[r0:r0 + sub_tile, :], wb_ref[...]))
        h = (x * g_ref[...]).astype(BF16)
        inv_rms = lax.rsqrt(jnp.mean(x * x, axis=-1, keepdims=True) + RMS_EPS)
        acc = None
        for c0 in range(0, d_ff, hidden_tile):
            gate = _dot(h, wg_ref[:, c0:c0 + hidden_tile]) * inv_rms
            up = _dot(h, wu_ref[:, c0:c0 + hidden_tile])
            act = (gate * jax.nn.sigmoid(gate) * up).astype(BF16)
            part = _dot(act, wd_ref[c0:c0 + hidden_tile, :])
            acc = part if acc is None else acc + part
        o_ref[r0:r0 + sub_tile, :] = x + (0.5 * inv_rms) * acc


def _resident(shape):
    return pl.BlockSpec(shape, lambda *_: (0,) * len(shape), pipeline_mode=pl.Buffered(1))


def _ffn(x, gain, w_gate, w_up, w_down, which, mixer_out=None):
    t, d = x.shape
    d_ff = w_gate.shape[-1]
    tm, tf = FFN_TOKEN_TILE, FFN_HIDDEN_TILE
    assert t % tm == 0 and d_ff % tf == 0 and tm % FFN_SUB_TILE == 0
    row = lambda i: (i, 0)
    picked = lambda r, c: pl.BlockSpec((None, None, r, c), lambda i: (*which, 0, 0),
                                       pipeline_mode=pl.Buffered(1))
    in_specs = [pl.BlockSpec((tm, d), row), _resident((1, d)), picked(d, d_ff), picked(d, d_ff),
                picked(d_ff, d)]
    args = [x, gain.reshape(1, d), w_gate, w_up, w_down]
    if mixer_out is not None:
        ya, yb, w_out = mixer_out
        wa, wb = ya.shape[1], yb.shape[1]
        in_specs += [pl.BlockSpec((tm, wa), row), pl.BlockSpec((tm, wb), row),
                     _resident((wa, d)), _resident((wb, d))]
        args += [ya, yb, w_out[:wa].astype(BF16), w_out[wa:].astype(BF16)]
    return pl.pallas_call(
        functools.partial(_ffn_kernel, hidden_tile=tf, sub_tile=FFN_SUB_TILE),
        grid=(t // tm,),
        in_specs=in_specs,
        out_specs=pl.BlockSpec((tm, d), row),
        out_shape=jax.ShapeDtypeStruct((t, d), F32),
        compiler_params=_compiler_params(("parallel",)),
        name="ffn",
    )(*args)


def _inproj_kernel(x_ref, g_ref, w_ref, wqt_ref, wrt_ref, wgate_ref, gqcol_ref, gk_ref, grp_ref,
                   gbias_ref, qat_ref, ka_ref, vat_ref, kb_ref, qbt_ref, vbt_ref, obt_ref, gates_ref,
                   *, a_width, b_width):
    h = _rms_normalize(x_ref[...], g_ref[...]).astype(BF16)

    qt = _dot_nt(wqt_ref[...], h)
    for hd in range(a_width // A_HEAD_DIM):
        rows = slice(hd * A_HEAD_DIM, (hd + 1) * A_HEAD_DIM)
        y = qt[rows, :]
        ms = jnp.mean(y * y, axis=0, keepdims=True)
        gain = gqcol_ref[rows, :] * (A_HEAD_DIM ** -0.5 * LOG2E)
        qat_ref[rows, :] = (y * lax.rsqrt(ms + RMS_EPS) * gain).astype(BF16)

    y = _dot(h, w_ref[:, 0:a_width])
    sq_hi, sq_lo = _split_bf16(y * y, 2)
    ssq = _dot(sq_hi, grp_ref[...]) + _dot(sq_lo, grp_ref[...])
    ka_ref[...] = (y * lax.rsqrt(ssq * (1.0 / A_HEAD_DIM) + RMS_EPS) * gk_ref[...]).astype(BF16)

    kb_ref[...] = _dot(h, w_ref[:, a_width:a_width + b_width]).astype(BF16)
    rest_t = _dot_nt(wrt_ref[...], h)
    vat_ref[...] = rest_t[0:a_width, :].astype(BF16)
    for piece, out_ref in enumerate((qbt_ref, vbt_ref, obt_ref)):
        r0 = a_width + piece * b_width
        out_ref[...] = rest_t[r0:r0 + b_width, :].astype(BF16)
    gates = _dot(h, wgate_ref[...]) + gbias_ref[...]
    gates_ref[...] = gates[:, :2 * B_HEADS]


def _inproj(x, gain, w_in, g_q, g_k, b_i, b_f):
    t, d = x.shape
    a_width = A_HEADS * A_HEAD_DIM
    b_width = B_HEADS * B_HEAD_DIM
    n_main = a_width + b_width
    o_qb = 3 * a_width
    o_ib = 3 * a_width + 3 * b_width
    tm = PROJ_TOKEN_TILE
    assert t % tm == 0
    w_main = jnp.concatenate([w_in[:, a_width:2 * a_width], w_in[:, o_qb + b_width:o_qb + 2 * b_width]],
                             axis=1).astype(BF16)
    w_qt = w_in[:, :a_width].T.astype(BF16)
    w_rt = jnp.concatenate([w_in[:, 2 * a_width:3 * a_width], w_in[:, o_qb:o_qb + b_width],
                            w_in[:, o_qb + 2 * b_width:o_ib], w_in[:, o_ib + 2 * B_HEADS:]],
                           axis=1).T.astype(BF16)
    w_gate = jnp.pad(w_in[:, o_ib:o_ib + 2 * B_HEADS], ((0, 0), (0, LANES - 2 * B_HEADS))).astype(BF16)
    gate_bias = jnp.pad(jnp.concatenate([b_i, b_f]).astype(F32), (0, LANES - 2 * B_HEADS)).reshape(1, LANES)
    head_id = jnp.arange(a_width) // A_HEAD_DIM
    grp = (head_id[:, None] == head_id[None, :]).astype(BF16)
    gq_col = jnp.tile(g_q.astype(F32), A_HEADS).reshape(a_width, 1)
    gk = jnp.tile(g_k.astype(F32), A_HEADS).reshape(1, a_width)
    const = lambda i: (0, 0)
    row = lambda i: (i, 0)
    col = lambda i: (0, i)
    outs = pl.pallas_call(
        functools.partial(_inproj_kernel, a_width=a_width, b_width=b_width),
        grid=(t // tm,),
        in_specs=[
            pl.BlockSpec((tm, d), row),
            pl.BlockSpec((1, d), const),
            pl.BlockSpec((d, n_main), const),
            pl.BlockSpec((a_width, d), const),
            pl.BlockSpec((a_width + 3 * b_width, d), const),
            pl.BlockSpec((d, LANES), const),
            pl.BlockSpec((a_width, 1), const),
            pl.BlockSpec((1, a_width), const),
            pl.BlockSpec((a_width, a_width), const),
            pl.BlockSpec((1, LANES), const),
        ],
        out_specs=[
            pl.BlockSpec((a_width, tm), col),
            pl.BlockSpec((tm, a_width), row),
            pl.BlockSpec((a_width, tm), col),
            pl.BlockSpec((tm, b_width), row),
            pl.BlockSpec((b_width, tm), col),
            pl.BlockSpec((b_width, tm), col),
            pl.BlockSpec((b_width, tm), col),
            pl.BlockSpec((tm, 2 * B_HEADS), row),
        ],
        out_shape=[
            jax.ShapeDtypeStruct((a_width, t), BF16),
            jax.ShapeDtypeStruct((t, a_width), BF16),
            jax.ShapeDtypeStruct((a_width, t), BF16),
            jax.ShapeDtypeStruct((t, b_width), BF16),
            jax.ShapeDtypeStruct((b_width, t), BF16),
            jax.ShapeDtypeStruct((b_width, t), BF16),
            jax.ShapeDtypeStruct((b_width, t), BF16),
            jax.ShapeDtypeStruct((t, 2 * B_HEADS), F32),
        ],
        compiler_params=_compiler_params(("parallel",)),
        name="inproj",
    )(x, gain.reshape(1, d), w_main, w_qt, w_rt, w_gate, gq_col, gk, grp, gate_bias)
    return outs


def _moba_kernel(slope_ref, qt_ref, k_ref, vt_ref, o_ref, kaug_ref, vaug_ref, kmean_ref, qaug_ref, acc_ref,
                 sa_ref, sb_ref, *, nb, grp, heads, tiles):
    hg = pl.program_id(0)
    batch = pl.program_id(1)
    c0 = pl.program_id(2) * tiles
    blk = MOBA_BLOCK
    dh = A_HEAD_DIM
    nbp = pl.cdiv(nb, SUBLANES_BF16) * SUBLANES_BF16
    streams = [(ti, h) for ti in range(tiles) for h in range(heads)]

    @pl.when(c0 == 0)
    def _build_key_value_side():
        ones_rows = jnp.where(lax.broadcasted_iota(jnp.int32, (V_AUG_ROWS - dh, blk), 0) == 0,
                              1.0, 0.0).astype(BF16)
        for h in range(heads):
            for j in range(nb):
                g, off = divmod(j, grp)
                vaug_ref[h, g, 0:dh, off * blk:(off + 1) * blk] = vt_ref[dh * h:dh * (h + 1),
                                                                         j * blk:(j + 1) * blk]
                vaug_ref[h, g, dh:V_AUG_ROWS, off * blk:(off + 1) * blk] = ones_rows
        kmean_ref[...] = jnp.zeros_like(kmean_ref)
        lane = lax.broadcasted_iota(jnp.int32, (blk, LANES), 1)
        row = lax.broadcasted_iota(jnp.int32, (blk, LANES), 0)

        def key_block(j, carry, *, with_extra_terms):
            start = pl.multiple_of(j * blk, blk)
            pos = (row + j * blk).astype(F32)
            for pr in range(heads // 2):
                kb = k_ref[pl.ds(start, blk), pr * LANES:(pr + 1) * LANES]
                kmean_ref[pr, pl.ds(j, 1), :] = jnp.mean(kb.astype(F32), axis=0, keepdims=True)
                for hh in range(2):
                    h = 2 * pr + hh
                    is_data = (lane >= dh * hh) & (lane < dh * (hh + 1))
                    if with_extra_terms:
                        slope = slope_ref[pl.ds(heads * hg + h, 1), :][:, 0:LANES] * LOG2E
                        p1, p2, p3 = _split_bf16(slope * pos, 3)
                        rel = lane - dh * (1 - hh)
                        aug = jnp.where(rel == j, 1.0, 0.0)
                        aug = jnp.where(rel == nb, p1.astype(F32), aug)
                        aug = jnp.where(rel == nb + 1, p2.astype(F32), aug)
                        aug = jnp.where(rel == nb + 2, p3.astype(F32), aug)
                        aug = jnp.where((rel >= nb + 3) & (rel < nb + 6), 1.0, aug).astype(BF16)
                    else:
                        aug = kaug_ref[h, pl.ds(start, blk), :]
                    kaug_ref[h, pl.ds(start, blk), :] = jnp.where(is_data, kb, aug)
            return carry

        @pl.when(batch == 0)
        def _():
            lax.fori_loop(0, nb, functools.partial(key_block, with_extra_terms=True), 0)

        @pl.when(batch != 0)
        def _():
            lax.fori_loop(0, nb, functools.partial(key_block, with_extra_terms=False), 0)

    blk_ix = lax.broadcasted_iota(jnp.int32, (nbp, blk), 0)
    blk_f = blk_ix.astype(F32)
    aug_row = lax.broadcasted_iota(jnp.int32, (dh, blk), 0)
    qry_lane = lax.broadcasted_iota(jnp.int32, (dh, blk), 1)
    lane_k = lax.broadcasted_iota(jnp.int32, (nbp, LANES), 1)
    own_grp = c0 // grp
    key_ix = lax.broadcasted_iota(jnp.int32, (grp * blk, blk), 0) + own_grp * (grp * blk)
    qry_ix = lax.broadcasted_iota(jnp.int32, (grp * blk, blk), 1)
    gates = []
    for ti, h in streams:
        pr, hh = divmod(h, 2)
        qt_pair = qt_ref[pr * LANES:(pr + 1) * LANES, ti * blk:(ti + 1) * blk]
        is_data_k = (lane_k >= dh * hh) & (lane_k < dh * (hh + 1))
        km_hi, km_lo = _split_bf16(jnp.where(is_data_k, kmean_ref[pr, 0:nbp, :], 0.0), 2)
        gates.append(_dot(km_hi, qt_pair) + _dot(km_lo, qt_pair))
    for si, (ti, h) in enumerate(streams):
        pr, hh = divmod(h, 2)
        c = c0 + ti
        valid = blk_ix < c
        tq = (qry_lane + c * blk).astype(F32)
        gate = jnp.where(valid, gates[si], -jnp.inf)
        chosen = blk_ix == c
        for _ in range(MOBA_TOPK):
            best = jnp.max(gate, axis=0, keepdims=True)
            first = jnp.min(jnp.where(gate == best, blk_f, float(nbp)), axis=0, keepdims=True)
            pick = blk_f == first
            chosen = chosen | (pick & valid)
            gate = jnp.where(pick, -jnp.inf, gate)
        bias = jnp.where(chosen, 0.0, MASK_VALUE)
        if nbp < dh:
            bias = jnp.concatenate([bias, jnp.zeros((dh - nbp, blk), F32)], axis=0)
        slope = slope_ref[pl.ds(heads * hg + h, 1), :] * LOG2E
        t1, t2, t3 = _split_bf16(-slope * tq, 3)
        aug = jnp.where(aug_row < nb, bias, 0.0)
        aug = jnp.where((aug_row >= nb) & (aug_row < nb + 3), 1.0, aug)
        aug = jnp.where(aug_row == nb + 3, t1.astype(F32), aug)
        aug = jnp.where(aug_row == nb + 4, t2.astype(F32), aug)
        aug = jnp.where(aug_row == nb + 5, t3.astype(F32), aug).astype(BF16)
        data = qt_ref[dh * h:dh * (h + 1), ti * blk:(ti + 1) * blk]
        qaug_ref[si] = jnp.concatenate([data, aug] if hh == 0 else [aug, data], axis=0)

    def scores_into(buf_ref, g):
        start = pl.multiple_of(g * (grp * blk), grp * blk)
        col_max = []
        for si, (ti, h) in enumerate(streams):
            s = _dot(kaug_ref[h, pl.ds(start, grp * blk), :], qaug_ref[si])
            buf_ref[si] = s
            col_max.append(jnp.max(s, axis=0, keepdims=True))
        return tuple(col_max)

    def consume(buf_ref, g, ms, col_max, own=False):
        new_ms = []
        for si, (ti, h) in enumerate(streams):
            s = buf_ref[si]
            if own:
                s = jnp.where(key_ix <= qry_ix + (c0 + ti) * blk, s, MASK_VALUE)
                group_max = jnp.max(s, axis=0, keepdims=True)
            else:
                group_max = col_max[si]
            m_new = jnp.maximum(ms[si], group_max)
            alpha = jnp.exp2(ms[si] - m_new)
            p = jnp.exp2(s - m_new)
            acc_ref[si] = alpha * acc_ref[si] + _dot(vaug_ref[h, g], p.astype(BF16))
            new_ms.append(m_new)
        return tuple(new_ms)

    def write_output():
        for ti in range(tiles):
            outs = []
            for h in range(heads):
                acc = acc_ref[ti * heads + h]
                outs.append(acc[0:dh, :] / acc[dh:dh + 1, :])
            o_ref[ti * blk:(ti + 1) * blk, :] = jnp.concatenate(outs, axis=0).T.astype(o_ref.dtype)

    acc_ref[...] = jnp.zeros_like(acc_ref)
    masked_score = jnp.full((1, blk), MASK_VALUE, F32).astype(BF16).astype(F32)
    max_a0 = scores_into(sa_ref, 0)

    def two_groups(i, carry):
        ms, max_a = carry
        max_b = scores_into(sb_ref, 2 * i + 1)
        ms = consume(sa_ref, 2 * i, ms, max_a)
        max_a = scores_into(sa_ref, 2 * i + 2)
        return consume(sb_ref, 2 * i + 1, ms, max_b), max_a

    ms, max_a = lax.fori_loop(0, own_grp // 2, two_groups, ((masked_score,) * len(streams), max_a0))

    @pl.when(own_grp % 2 == 0)
    def _own_group_in_a():
        consume(sa_ref, own_grp, ms, None, own=True)
        write_output()

    @pl.when(own_grp % 2 == 1)
    def _own_group_in_b():
        scores_into(sb_ref, own_grp)
        consume(sb_ref, own_grp, consume(sa_ref, own_grp - 1, ms, max_a), None, own=True)
        write_output()


def _moba(qat, ka, vat, bsz, seq):
    t, a_width = ka.shape
    blk = MOBA_BLOCK
    assert seq % blk == 0 and seq // blk >= MOBA_TOPK
    nb = seq // blk
    assert nb + 6 <= A_HEAD_DIM
    grp = MOBA_KEY_GROUP if nb % MOBA_KEY_GROUP == 0 else 1
    heads = MOBA_HEADS_PER_STEP
    width = heads * A_HEAD_DIM
    assert heads % 2 == 0 and a_width % width == 0
    tiles = grp
    n_streams = tiles * heads
    steps = nb // tiles
    slopes = jnp.exp2(-8.0 * jnp.arange(1, A_HEADS + 1, dtype=F32) / A_HEADS)
    slope_tbl = jnp.broadcast_to(slopes[:, None], (A_HEADS, blk))
    return pl.pallas_call(
        functools.partial(_moba_kernel, nb=nb, grp=grp, heads=heads, tiles=tiles),
        grid=(a_width // width, bsz, steps),
        in_specs=[
            pl.BlockSpec((A_HEADS, blk), lambda hg, b, c: (0, 0)),
            pl.BlockSpec((width, tiles * blk), lambda hg, b, c: (hg, b * steps + c)),
            pl.BlockSpec((seq, width), lambda hg, b, c: (b, hg)),
            pl.BlockSpec((width, seq), lambda hg, b, c: (hg, b)),
        ],
        out_specs=pl.BlockSpec((tiles * blk, width), lambda hg, b, c: (b * steps + c, hg)),
        out_shape=jax.ShapeDtypeStruct((t, a_width), BF16),
        scratch_shapes=[
            pltpu.VMEM((heads, seq, LANES), BF16),
            pltpu.VMEM((heads, nb // grp, V_AUG_ROWS, grp * blk), BF16),
            pltpu.VMEM((heads // 2, LANES, LANES), F32),
            pltpu.VMEM((n_streams, LANES, blk), BF16),
            pltpu.VMEM((n_streams, V_AUG_ROWS, blk), F32),
            pltpu.VMEM((n_streams, grp * blk, blk), F32),
            pltpu.VMEM((n_streams, grp * blk, blk), F32),
        ],
        compiler_params=_compiler_params(("parallel", "arbitrary", "arbitrary")),
        name="moba",
    )(slope_tbl, qat, ka, vat)


def _mlstm_kernel(qt_ref, k_ref, vt_ref, obt_ref, grow_ref, cwq_ref, cbq_ref, cwk_ref, cbk_ref,
                  o_ref, qtail_ref, kbuf_ref, ct_ref, m_ref):
    ci = pl.program_id(1)
    L, width = k_ref.shape
    d = B_HEAD_DIM
    halo = SUBLANES_F32

    @pl.when(ci == 0)
    def _():
        qtail_ref[...] = jnp.zeros_like(qtail_ref)
        kbuf_ref[0:halo, :] = jnp.zeros((halo, width), F32)
        ct_ref[...] = jnp.zeros_like(ct_ref)
        m_ref[...] = jnp.zeros_like(m_ref)

    xq = qt_ref[...].astype(F32)
    lane = lax.broadcasted_iota(jnp.int32, (width, LANES), 1)
    tail = qtail_ref[...]
    yq = cbq_ref[...] + xq * cwq_ref[CONV_WIDTH - 1]
    for s in range(1, CONV_WIDTH):
        shifted = pltpu.roll(xq, s, axis=1)
        first = jnp.where(lane < s, pltpu.roll(tail, s, axis=1), shifted[:, :LANES])
        shifted = jnp.concatenate([first, shifted[:, LANES:]], axis=1)
        yq = yq + shifted * cwq_ref[CONV_WIDTH - 1 - s]
    qtail_ref[...] = xq[:, L - LANES:]
    q_all = (yq * jax.nn.sigmoid(yq) * (d ** -0.5)).astype(BF16)

    kbuf_ref[halo:halo + L, :] = k_ref[...].astype(F32)
    yk = cbk_ref[...]
    for j in range(CONV_WIDTH):
        off = halo - (CONV_WIDTH - 1) + j
        yk = yk + kbuf_ref[off:off + L, :] * cwk_ref[j:j + 1, :]
    kbuf_ref[0:halo, :] = kbuf_ref[L:L + halo, :]
    k_all = (yk * jax.nn.sigmoid(yk)).astype(BF16)

    def log_sigmoid(z):
        return jnp.minimum(z, 0.0) - jnp.log1p(jnp.exp(-jnp.abs(z)))

    src = lax.broadcasted_iota(jnp.int32, (L, L), 0)
    dst = lax.broadcasted_iota(jnp.int32, (L, L), 1)
    causal = src <= dst
    tri_up = causal.astype(BF16)
    gates_row = grow_ref[0]
    gates_col = gates_row.T
    lf_hi, lf_lo = _split_bf16(log_sigmoid(gates_row), 2)
    b_rows = _dot(lf_hi, tri_up) + _dot(lf_lo, tri_up)
    b_cols = b_rows.T
    pad_rows = ct_ref.shape[1] - d
    ones_rows = jnp.where(lax.broadcasted_iota(jnp.int32, (pad_rows, L), 0) == 0, 1.0, 0.0).astype(BF16)
    for hd in range(B_HEADS):
        rows = slice(hd * d, (hd + 1) * d)
        qt = q_all[rows, :]
        k = k_all[:, rows]
        vt_aug = jnp.concatenate([vt_ref[rows, :], ones_rows], axis=0)
        i_col = gates_col[:, hd:hd + 1]
        i_row = gates_row[hd:hd + 1, :]
        b_col = b_cols[:, hd + B_HEADS:hd + B_HEADS + 1]
        b_row = b_rows[hd + B_HEADS:hd + B_HEADS + 1, :]

        m_prev = m_ref[hd]
        log_inter = b_row + m_prev
        dmat = jnp.where(causal, b_row + (i_col - b_col), -jnp.inf)
        m_t = jnp.maximum(log_inter, jnp.max(dmat, axis=0, keepdims=True))
        w_inter = jnp.exp(log_inter - m_t)
        sc = _dot(k, qt) * jnp.exp(dmat - m_t)
        ct_prev = ct_ref[hd]
        inter = w_inter * _dot(ct_prev.astype(BF16), qt)
        num = inter[:d, :] + _dot(vt_ref[rows, :], sc.astype(BF16))
        den = inter[d:d + 1, :] + jnp.sum(sc, axis=0, keepdims=True)
        hidden = num / jnp.maximum(jnp.abs(den), jnp.exp(-m_t))
        out_t = jax.nn.sigmoid(obt_ref[rows, :].astype(F32)) * hidden
        o_ref[:, rows] = out_t.T.astype(o_ref.dtype)

        b_last = b_row[:, L - 1:L]
        log_old = b_last + m_prev
        log_new = b_last - b_row + i_row
        m_new = jnp.maximum(log_old, jnp.max(log_new, axis=1, keepdims=True))
        a_old = jnp.exp(log_old - m_new)
        a_new = jnp.exp(log_new - m_new)
        ct_ref[hd] = a_old * ct_prev + _dot((vt_aug.astype(F32) * a_new).astype(BF16), k)
        m_ref[hd] = m_new


def _mlstm(qbt, kb, vbt, obt, gates, conv_w, conv_b, bsz, seq):
    b_width, t = qbt.shape
    d = B_HEAD_DIM
    L = min(MLSTM_KERNEL_CHUNK, seq)
    assert seq % L == 0 and L % LANES == 0 and d == LANES and b_width == B_HEADS * d
    nc = seq // L
    gates_row = gates.reshape(bsz, seq, 2 * B_HEADS).transpose(0, 2, 1)
    cwq = jnp.broadcast_to(conv_w[:, :b_width, None].astype(F32), (CONV_WIDTH, b_width, L))
    cbq = jnp.broadcast_to(conv_b[:b_width, None].astype(F32), (b_width, L))
    cwk = conv_w[:, b_width:].astype(F32)
    cbk = conv_b[b_width:].astype(F32).reshape(1, b_width)
    tposed = lambda b, c: (0, b * nc + c)
    natural = lambda b, c: (b * nc + c, 0)
    const2 = lambda b, c: (0, 0)
    return pl.pallas_call(
        _mlstm_kernel,
        grid=(bsz, nc),
        in_specs=[
            pl.BlockSpec((b_width, L), tposed),
            pl.BlockSpec((L, b_width), natural),
            pl.BlockSpec((b_width, L), tposed),
            pl.BlockSpec((b_width, L), tposed),
            pl.BlockSpec((1, 2 * B_HEADS, L), lambda b, c: (b, 0, c)),
            pl.BlockSpec((CONV_WIDTH, b_width, L), lambda b, c: (0, 0, 0)),
            pl.BlockSpec((b_width, L), const2),
            pl.BlockSpec((CONV_WIDTH, b_width), const2),
            pl.BlockSpec((1, b_width), const2),
        ],
        out_specs=pl.BlockSpec((L, b_width), natural),
        out_shape=jax.ShapeDtypeStruct((t, b_width), BF16),
        scratch_shapes=[
            pltpu.VMEM((b_width, LANES), F32),
            pltpu.VMEM((L + SUBLANES_F32, b_width), F32),
            pltpu.VMEM((B_HEADS, d + SUBLANES_BF16, d), F32),
            pltpu.VMEM((B_HEADS, 1, 1), F32),
        ],
        compiler_params=_compiler_params(("parallel", "arbitrary")),
        name="mlstm",
    )(qbt, kb, vbt, obt, gates_row, cwq, cbq, cwk, cbk)


def _pool_kernel(x_ref, halo_ref, g_ref, w_ref, scale_ref, o_ref, sums_ref, *, tiles_per_seq):
    i = pl.program_id(0)
    tm, d = x_ref.shape
    n_win = len(POOL_WINDOWS)
    grp = d // n_win
    base = POOL_PAD + POOL_HALO
    rows = POOL_HALO + tm
    x = x_ref[...]
    seq_tile = i % tiles_per_seq
    sums_ref[:, 0:POOL_PAD, :] = jnp.zeros((n_win, POOL_PAD, d), F32)
    halo_h = _rms_normalize(halo_ref[...], g_ref[...])
    sums_ref[0, POOL_PAD:base, :] = jnp.where(seq_tile == 0, 0.0, halo_h)
    h = _rms_normalize(x, g_ref[...])
    sums_ref[0, base:base + tm, :] = h
    t1 = (lax.broadcasted_iota(jnp.int32, (tm, 1), 0) + seq_tile * tm + 1).astype(F32)
    for k, win in enumerate(POOL_WINDOWS):
        half = win // 2
        c0 = k * grp
        if k + 1 < n_win:
            both = (sums_ref[k, POOL_PAD:POOL_PAD + rows, c0:]
                    + sums_ref[k, POOL_PAD - half:POOL_PAD - half + rows, c0:])
            sums_ref[k + 1, POOL_PAD:POOL_PAD + rows, c0:] = both
            total = both[POOL_HALO:, 0:grp]
        else:
            total = (sums_ref[k, base:base + tm, c0:c0 + grp]
                     + sums_ref[k, base - half:base - half + tm, c0:c0 + grp])
        pooled = total / jnp.minimum(t1, float(win)) - h[:, c0:c0 + grp]
        y = _dot(pooled.astype(BF16), w_ref[k])
        o_ref[:, c0:c0 + grp] = x[:, c0:c0 + grp] + y * scale_ref[:, c0:c0 + grp]


def _pool(x, gain, w_grp, scale, seq):
    t, d = x.shape
    tm = min(POOL_TOKEN_TILE, seq)
    assert seq % tm == 0 and tm % POOL_HALO == 0
    assert POOL_WINDOWS == tuple(2 ** (k + 1) for k in range(len(POOL_WINDOWS)))
    assert POOL_WINDOWS[-1] <= POOL_HALO and POOL_WINDOWS[-1] // 2 <= POOL_PAD
    n_grp, grp, _ = w_grp.shape
    halo_blocks = tm // POOL_HALO
    return pl.pallas_call(
        functools.partial(_pool_kernel, tiles_per_seq=seq // tm),
        grid=(t // tm,),
        in_specs=[
            pl.BlockSpec((tm, d), lambda i: (i, 0)),
            pl.BlockSpec((POOL_HALO, d), lambda i: (jnp.maximum(i * halo_blocks - 1, 0), 0)),
            pl.BlockSpec((1, d), lambda i: (0, 0)),
            pl.BlockSpec((n_grp, grp, grp), lambda i: (0, 0, 0)),
            pl.BlockSpec((1, d), lambda i: (0, 0)),
        ],
        out_specs=pl.BlockSpec((tm, d), lambda i: (i, 0)),
        out_shape=jax.ShapeDtypeStruct((t, d), F32),
        scratch_shapes=[pltpu.VMEM((len(POOL_WINDOWS), POOL_PAD + POOL_HALO + tm, d), F32)],
        compiler_params=_compiler_params(("parallel",)),
        name="pool",
    )(x, x, gain.reshape(1, d), w_grp.astype(BF16), scale.astype(F32).reshape(1, d))


def _mixer_heads(x, gain, w_in, g_q, g_k, conv_w, conv_b, b_i, b_f, bsz, seq):
    qat, ka, vat, kb, qbt, vbt, obt, gates = _inproj(x, gain, w_in, g_q, g_k, b_i, b_f)
    ya = _moba(qat, ka, vat, bsz, seq)
    yb = _mlstm(qbt, kb, vbt, obt, gates, conv_w, conv_b, bsz, seq)
    return ya, yb


def kernel(x, norm_g, ffn_w_gate, ffn_w_up, ffn_w_down, ab_w_in, ab_w_out, ab_g_q, ab_g_k,
           ab_conv_w, ab_conv_b, ab_b_i, ab_b_f, pool_w, pool_scale):
    bsz, seq, d = x.shape
    depth = norm_g.shape[0]
    y = x.reshape(bsz * seq, d)
    w_gate, w_up, w_down = (w.astype(BF16) for w in (ffn_w_gate, ffn_w_up, ffn_w_down))
    for layer in range(depth):
        y = _ffn(y, norm_g[layer, 0], w_gate, w_up, w_down, (layer, 0))
        mixer_out = None
        if layer % 2 == 0:
            e = layer // 2
            ya, yb = _mixer_heads(y, norm_g[layer, 1], ab_w_in[e], ab_g_q[e], ab_g_k[e],
                                  ab_conv_w[e], ab_conv_b[e], ab_b_i[e], ab_b_f[e], bsz, seq)
            mixer_out = (ya, yb, ab_w_out[e])
        else:
            o = layer // 2
            y = _pool(y, norm_g[layer, 1], pool_w[o], pool_scale[o], seq)
        y = _ffn(y, norm_g[layer, 2], w_gate, w_up, w_down, (layer, 1), mixer_out=mixer_out)
    return y.reshape(bsz, seq, d)
```

```python
import functools

import jax
import jax.numpy as jnp
from jax import lax
from jax.experimental import pallas as pl
from jax.experimental.pallas import tpu as pltpu

F32 = jnp.float32
BF16 = jnp.bfloat16

LANES = 128
SUBLANES_F32 = 8
SUBLANES_BF16 = 16
MXU_WIDTH = 256
VMEM_BYTES = 64 * 1024 * 1024
VMEM_LIMIT_BYTES = VMEM_BYTES * 7 // 8

RMS_EPS = 1e-6
A_HEADS = 8
A_HEAD_DIM = 64
MOBA_BLOCK = 256
MOBA_TOPK = 3
B_HEADS = 4
B_HEAD_DIM = 128
CONV_WIDTH = 4
POOL_WINDOWS = (2, 4, 8, 16)

MOBA_KEY_GROUP = 2
MOBA_HEADS_PER_STEP = 4
MLSTM_KERNEL_CHUNK = 256
POOL_HALO = 16
POOL_PAD = SUBLANES_F32
MASK_VALUE = -1e30
LOG2E = 1.4426950408889634
V_AUG_ROWS = A_HEAD_DIM + SUBLANES_BF16
FFN_TOKEN_TILE = 1024
FFN_SUB_TILE = 512
FFN_HIDDEN_TILE = MXU_WIDTH
PROJ_TOKEN_TILE = 1024
POOL_TOKEN_TILE = 1024


def _compiler_params(semantics):
    return pltpu.CompilerParams(dimension_semantics=semantics,
                                vmem_limit_bytes=VMEM_LIMIT_BYTES)


def _rms_normalize(x, gain):
    ms = jnp.mean(x * x, axis=-1, keepdims=True)
    return x * lax.rsqrt(ms + RMS_EPS) * gain


def _dot(a, b):
    return jnp.dot(a, b, preferred_element_type=F32)


def _dot_nt(a, b):
    return lax.dot_general(a, b, (((1,), (1,)), ((), ())), preferred_element_type=F32)


def _split_bf16(x, parts):
    out = []
    rem = x
    for _ in range(parts):
        p = rem.astype(BF16)
        out.append(p)
        rem = rem - p.astype(F32)
    return out


def _ffn_kernel(x_ref, g_ref, wg_hbm, wu_hbm, wd_hbm, *rest, hidden_tile, sub_tile, which):
    o_ref, wg_ref, wu_ref, wd_ref, sem = rest[-5:]
    d_ff = wg_ref.shape[1]
    chunks = list(range(0, d_ff, hidden_tile))

    def weight_copies(ci):
        c0 = chunks[ci]
        cols = pl.ds(c0, hidden_tile)
        return (pltpu.make_async_copy(wg_hbm.at[which[0], which[1], :, cols], wg_ref.at[:, cols], sem.at[0, ci]),
                pltpu.make_async_copy(wu_hbm.at[which[0], which[1], :, cols], wu_ref.at[:, cols], sem.at[1, ci]),
                pltpu.make_async_copy(wd_hbm.at[which[0], which[1], cols, :], wd_ref.at[cols, :], sem.at[2, ci]))

    def body(first_step):
        if first_step:
            for ci in range(len(chunks)):
                for copy in weight_copies(ci):
                    copy.start()
        for r0 in range(0, x_ref.shape[0], sub_tile):
            x = x_ref[r0:r0 + sub_tile, :]
            if len(rest) > 5:
                ya_ref, yb_ref, wa_ref, wb_ref = rest[:4]
                x = (x + _dot(ya_ref[r0:r0 + sub_tile, :], wa_ref[...])
                     + _dot(yb_ref[r0:r0 + sub_tile, :], wb_ref[...]))
            h = (x * g_ref[...]).astype(BF16)
            inv_rms = lax.rsqrt(jnp.mean(x * x, axis=-1, keepdims=True) + RMS_EPS)
            acc = None
            for ci, c0 in enumerate(chunks):
                if first_step and r0 == 0:
                    for copy in weight_copies(ci):
                        copy.wait()
                gate = _dot(h, wg_ref[:, c0:c0 + hidden_tile]) * inv_rms
                up = _dot(h, wu_ref[:, c0:c0 + hidden_tile])
                act = (gate * jax.nn.sigmoid(gate) * up).astype(BF16)
                part = _dot(act, wd_ref[c0:c0 + hidden_tile, :])
                acc = part if acc is None else acc + part
            o_ref[r0:r0 + sub_tile, :] = x + (0.5 * inv_rms) * acc

    @pl.when(pl.program_id(0) == 0)
    def _():
        body(True)

    @pl.when(pl.program_id(0) != 0)
    def _():
        body(False)


def _resident(shape):
    return pl.BlockSpec(shape, lambda *_: (0,) * len(shape), pipeline_mode=pl.Buffered(1))


def _ffn(x, gain, w_gate, w_up, w_down, which, mixer_out=None):
    t, d = x.shape
    d_ff = w_gate.shape[-1]
    tm, tf = FFN_TOKEN_TILE, FFN_HIDDEN_TILE
    assert t % tm == 0 and d_ff % tf == 0 and tm % FFN_SUB_TILE == 0
    row = lambda i: (i, 0)
    in_hbm = pl.BlockSpec(memory_space=pl.ANY)
    in_specs = [pl.BlockSpec((tm, d), row), _resident((1, d)), in_hbm, in_hbm, in_hbm]
    args = [x, gain.reshape(1, d), w_gate, w_up, w_down]
    if mixer_out is not None:
        ya, yb, w_out = mixer_out
        wa, wb = ya.shape[1], yb.shape[1]
        in_specs += [pl.BlockSpec((tm, wa), row), pl.BlockSpec((tm, wb), row),
                     _resident((wa, d)), _resident((wb, d))]
        args += [ya, yb, w_out[:wa].astype(BF16), w_out[wa:].astype(BF16)]
    return pl.pallas_call(
        functools.partial(_ffn_kernel, hidden_tile=tf, sub_tile=FFN_SUB_TILE, which=which),
        grid=(t // tm,),
        in_specs=in_specs,
        out_specs=pl.BlockSpec((tm, d), row),
        out_shape=jax.ShapeDtypeStruct((t, d), F32),
        scratch_shapes=[pltpu.VMEM((d, d_ff), BF16), pltpu.VMEM((d, d_ff), BF16), pltpu.VMEM((d_ff, d), BF16),
                        pltpu.SemaphoreType.DMA((3, d_ff // tf))],
        compiler_params=_compiler_params(("arbitrary",)),
        name="ffn",
    )(*args)


def _inproj_kernel(x_ref, g_ref, w_ref, wqt_ref, wrt_ref, wgate_ref, gqcol_ref, gk_ref, grp_ref,
                   gbias_ref, qat_ref, ka_ref, vat_ref, kb_ref, qbt_ref, vbt_ref, obt_ref, gates_ref,
                   *, a_width, b_width):
    h = _rms_normalize(x_ref[...], g_ref[...]).astype(BF16)

    qt = _dot_nt(wqt_ref[...], h)
    for hd in range(a_width // A_HEAD_DIM):
        rows = slice(hd * A_HEAD_DIM, (hd + 1) * A_HEAD_DIM)
        y = qt[rows, :]
        ms = jnp.mean(y * y, axis=0, keepdims=True)
        gain = gqcol_ref[rows, :] * (A_HEAD_DIM ** -0.5 * LOG2E)
        qat_ref[rows, :] = (y * lax.rsqrt(ms + RMS_EPS) * gain).astype(BF16)

    y = _dot(h, w_ref[:, 0:a_width])
    sq_hi, sq_lo = _split_bf16(y * y, 2)
    ssq = _dot(sq_hi, grp_ref[...]) + _dot(sq_lo, grp_ref[...])
    ka_ref[...] = (y * lax.rsqrt(ssq * (1.0 / A_HEAD_DIM) + RMS_EPS) * gk_ref[...]).astype(BF16)

    kb_ref[...] = _dot(h, w_ref[:, a_width:a_width + b_width]).astype(BF16)
    rest_t = _dot_nt(wrt_ref[...], h)
    vat_ref[...] = rest_t[0:a_width, :].astype(BF16)
    for piece, out_ref in enumerate((qbt_ref, vbt_ref, obt_ref)):
        r0 = a_width + piece * b_width
        out_ref[...] = rest_t[r0:r0 + b_width, :].astype(BF16)
    gates = _dot(h, wgate_ref[...]) + gbias_ref[...]
    gates_ref[...] = gates[:, :2 * B_HEADS]


def _inproj(x, gain, w_in, g_q, g_k, b_i, b_f):
    t, d = x.shape
    a_width = A_HEADS * A_HEAD_DIM
    b_width = B_HEADS * B_HEAD_DIM
    n_main = a_width + b_width
    o_qb = 3 * a_width
    o_ib = 3 * a_width + 3 * b_width
    tm = PROJ_TOKEN_TILE
    assert t % tm == 0
    w_main = jnp.concatenate([w_in[:, a_width:2 * a_width], w_in[:, o_qb + b_width:o_qb + 2 * b_width]],
                             axis=1).astype(BF16)
    w_qt = w_in[:, :a_width].T.astype(BF16)
    w_rt = jnp.concatenate([w_in[:, 2 * a_width:3 * a_width], w_in[:, o_qb:o_qb + b_width],
                            w_in[:, o_qb + 2 * b_width:o_ib], w_in[:, o_ib + 2 * B_HEADS:]],
                           axis=1).T.astype(BF16)
    w_gate = jnp.pad(w_in[:, o_ib:o_ib + 2 * B_HEADS], ((0, 0), (0, LANES - 2 * B_HEADS))).astype(BF16)
    gate_bias = jnp.pad(jnp.concatenate([b_i, b_f]).astype(F32), (0, LANES - 2 * B_HEADS)).reshape(1, LANES)
    head_id = jnp.arange(a_width) // A_HEAD_DIM
    grp = (head_id[:, None] == head_id[None, :]).astype(BF16)
    gq_col = jnp.tile(g_q.astype(F32), A_HEADS).reshape(a_width, 1)
    gk = jnp.tile(g_k.astype(F32), A_HEADS).reshape(1, a_width)
    const = lambda i: (0, 0)
    row = lambda i: (i, 0)
    col = lambda i: (0, i)
    outs = pl.pallas_call(
        functools.partial(_inproj_kernel, a_width=a_width, b_width=b_width),
        grid=(t // tm,),
        in_specs=[
            pl.BlockSpec((tm, d), row),
            pl.BlockSpec((1, d), const),
            pl.BlockSpec((d, n_main), const),
            pl.BlockSpec((a_width, d), const),
            pl.BlockSpec((a_width + 3 * b_width, d), const),
            pl.BlockSpec((d, LANES), const),
            pl.BlockSpec((a_width, 1), const),
            pl.BlockSpec((1, a_width), const),
            pl.BlockSpec((a_width, a_width), const),
            pl.BlockSpec((1, LANES), const),
        ],
        out_specs=[
            pl.BlockSpec((a_width, tm), col),
            pl.BlockSpec((tm, a_width), row),
            pl.BlockSpec((a_width, tm), col),
            pl.BlockSpec((tm, b_width), row),
            pl.BlockSpec((b_width, tm), col),
            pl.BlockSpec((b_width, tm), col),
            pl.BlockSpec((b_width, tm), col),
            pl.BlockSpec((tm, 2 * B_HEADS), row),
        ],
        out_shape=[
            jax.ShapeDtypeStruct((a_width, t), BF16),
            jax.ShapeDtypeStruct((t, a_width), BF16),
            jax.ShapeDtypeStruct((a_width, t), BF16),
            jax.ShapeDtypeStruct((t, b_width), BF16),
            jax.ShapeDtypeStruct((b_width, t), BF16),
            jax.ShapeDtypeStruct((b_width, t), BF16),
            jax.ShapeDtypeStruct((b_width, t), BF16),
            jax.ShapeDtypeStruct((t, 2 * B_HEADS), F32),
        ],
        compiler_params=_compiler_params(("parallel",)),
        name="inproj",
    )(x, gain.reshape(1, d), w_main, w_qt, w_rt, w_gate, gq_col, gk, grp, gate_bias)
    return outs


def _moba_kernel(slope_ref, qt_ref, k_ref, vt_ref, o_ref, kaug_ref, vaug_ref, kmean_ref, qaug_ref, acc_ref,
                 sa_ref, sb_ref, *, nb, grp, heads, tiles):
    hg = pl.program_id(0)
    batch = pl.program_id(1)
    c0 = pl.program_id(2) * tiles
    blk = MOBA_BLOCK
    dh = A_HEAD_DIM
    nbp = pl.cdiv(nb, SUBLANES_BF16) * SUBLANES_BF16
    streams = [(ti, h) for ti in range(tiles) for h in range(heads)]

    @pl.when(c0 == 0)
    def _build_key_value_side():
        ones_rows = jnp.where(lax.broadcasted_iota(jnp.int32, (V_AUG_ROWS - dh, blk), 0) == 0,
                              1.0, 0.0).astype(BF16)
        for h in range(heads):
            for j in range(nb):
                g, off = divmod(j, grp)
                vaug_ref[h, g, 0:dh, off * blk:(off + 1) * blk] = vt_ref[dh * h:dh * (h + 1),
                                                                         j * blk:(j + 1) * blk]
                vaug_ref[h, g, dh:V_AUG_ROWS, off * blk:(off + 1) * blk] = ones_rows
        kmean_ref[...] = jnp.zeros_like(kmean_ref)
        lane = lax.broadcasted_iota(jnp.int32, (blk, LANES), 1)
        row = lax.broadcasted_iota(jnp.int32, (blk, LANES), 0)

        def key_block(j, carry, *, with_extra_terms):
            start = pl.multiple_of(j * blk, blk)
            pos = (row + j * blk).astype(F32)
            for pr in range(heads // 2):
                kb = k_ref[pl.ds(start, blk), pr * LANES:(pr + 1) * LANES]
                kmean_ref[pr, pl.ds(j, 1), :] = jnp.mean(kb.astype(F32), axis=0, keepdims=True)
                for hh in range(2):
                    h = 2 * pr + hh
                    is_data = (lane >= dh * hh) & (lane < dh * (hh + 1))
                    if with_extra_terms:
                        slope = slope_ref[pl.ds(heads * hg + h, 1), :][:, 0:LANES] * LOG2E
                        p1, p2, p3 = _split_bf16(slope * pos, 3)
                        rel = lane - dh * (1 - hh)
                        aug = jnp.where(rel == j, 1.0, 0.0)
                        aug = jnp.where(rel == nb, p1.astype(F32), aug)
                        aug = jnp.where(rel == nb + 1, p2.astype(F32), aug)
                        aug = jnp.where(rel == nb + 2, p3.astype(F32), aug)
                        aug = jnp.where((rel >= nb + 3) & (rel < nb + 6), 1.0, aug).astype(BF16)
                    else:
                        aug = kaug_ref[h, pl.ds(start, blk), :]
                    kaug_ref[h, pl.ds(start, blk), :] = jnp.where(is_data, kb, aug)
            return carry

        @pl.when(batch == 0)
        def _():
            lax.fori_loop(0, nb, functools.partial(key_block, with_extra_terms=True), 0)

        @pl.when(batch != 0)
        def _():
            lax.fori_loop(0, nb, functools.partial(key_block, with_extra_terms=False), 0)

    blk_ix = lax.broadcasted_iota(jnp.int32, (nbp, blk), 0)
    blk_f = blk_ix.astype(F32)
    aug_row = lax.broadcasted_iota(jnp.int32, (dh, blk), 0)
    qry_lane = lax.broadcasted_iota(jnp.int32, (dh, blk), 1)
    lane_k = lax.broadcasted_iota(jnp.int32, (nbp, LANES), 1)
    own_grp = c0 // grp
    key_ix = lax.broadcasted_iota(jnp.int32, (grp * blk, blk), 0) + own_grp * (grp * blk)
    qry_ix = lax.broadcasted_iota(jnp.int32, (grp * blk, blk), 1)
    gates = []
    for ti, h in streams:
        pr, hh = divmod(h, 2)
        qt_pair = qt_ref[pr * LANES:(pr + 1) * LANES, ti * blk:(ti + 1) * blk]
        is_data_k = (lane_k >= dh * hh) & (lane_k < dh * (hh + 1))
        km_hi, km_lo = _split_bf16(jnp.where(is_data_k, kmean_ref[pr, 0:nbp, :], 0.0), 2)
        gates.append(_dot(km_hi, qt_pair) + _dot(km_lo, qt_pair))
    for si, (ti, h) in enumerate(streams):
        pr, hh = divmod(h, 2)
        c = c0 + ti
        valid = blk_ix < c
        tq = (qry_lane + c * blk).astype(F32)
        gate = jnp.where(valid, gates[si], -jnp.inf)
        chosen = blk_ix == c
        for _ in range(MOBA_TOPK):
            best = jnp.max(gate, axis=0, keepdims=True)
            first = jnp.min(jnp.where(gate == best, blk_f, float(nbp)), axis=0, keepdims=True)
            pick = blk_f == first
            chosen = chosen | (pick & valid)
            gate = jnp.where(pick, -jnp.inf, gate)
        bias = jnp.where(chosen, 0.0, MASK_VALUE)
        if nbp < dh:
            bias = jnp.concatenate([bias, jnp.zeros((dh - nbp, blk), F32)], axis=0)
        slope = slope_ref[pl.ds(heads * hg + h, 1), :] * LOG2E
        t1, t2, t3 = _split_bf16(-slope * tq, 3)
        aug = jnp.where(aug_row < nb, bias, 0.0)
        aug = jnp.where((aug_row >= nb) & (aug_row < nb + 3), 1.0, aug)
        aug = jnp.where(aug_row == nb + 3, t1.astype(F32), aug)
        aug = jnp.where(aug_row == nb + 4, t2.astype(F32), aug)
        aug = jnp.where(aug_row == nb + 5, t3.astype(F32), aug).astype(BF16)
        data = qt_ref[dh * h:dh * (h + 1), ti * blk:(ti + 1) * blk]
        qaug_ref[si] = jnp.concatenate([data, aug] if hh == 0 else [aug, data], axis=0)

    def scores_into(buf_ref, g):
        start = pl.multiple_of(g * (grp * blk), grp * blk)
        col_max = []
        for si, (ti, h) in enumerate(streams):
            s = _dot(kaug_ref[h, pl.ds(start, grp * blk), :], qaug_ref[si])
            buf_ref[si] = s
            col_max.append(jnp.max(s, axis=0, keepdims=True))
        return tuple(col_max)

    def consume(buf_ref, g, ms, col_max, own=False):
        new_ms = []
        for si, (ti, h) in enumerate(streams):
            s = buf_ref[si]
            if own:
                s = jnp.where(key_ix <= qry_ix + (c0 + ti) * blk, s, MASK_VALUE)
                group_max = jnp.max(s, axis=0, keepdims=True)
            else:
                group_max = col_max[si]
            m_new = jnp.maximum(ms[si], group_max)
            alpha = jnp.exp2(ms[si] - m_new)
            p = jnp.exp2(s - m_new)
            acc_ref[si] = alpha * acc_ref[si] + _dot(vaug_ref[h, g], p.astype(BF16))
            new_ms.append(m_new)
        return tuple(new_ms)

    def write_output():
        for ti in range(tiles):
            outs = []
            for h in range(heads):
                acc = acc_ref[ti * heads + h]
                outs.append(acc[0:dh, :] / acc[dh:dh + 1, :])
            o_ref[ti * blk:(ti + 1) * blk, :] = jnp.concatenate(outs, axis=0).T.astype(o_ref.dtype)

    acc_ref[...] = jnp.zeros_like(acc_ref)
    masked_score = jnp.full((1, blk), MASK_VALUE, F32).astype(BF16).astype(F32)
    max_a0 = scores_into(sa_ref, 0)

    def two_groups(i, carry):
        ms, max_a = carry
        max_b = scores_into(sb_ref, 2 * i + 1)
        ms = consume(sa_ref, 2 * i, ms, max_a)
        max_a = scores_into(sa_ref, 2 * i + 2)
        return consume(sb_ref, 2 * i + 1, ms, max_b), max_a

    ms, max_a = lax.fori_loop(0, own_grp // 2, two_groups, ((masked_score,) * len(streams), max_a0))

    @pl.when(own_grp % 2 == 0)
    def _own_group_in_a():
        consume(sa_ref, own_grp, ms, None, own=True)
        write_output()

    @pl.when(own_grp % 2 == 1)
    def _own_group_in_b():
        scores_into(sb_ref, own_grp)
        consume(sb_ref, own_grp, consume(sa_ref, own_grp - 1, ms, max_a), None, own=True)
        write_output()


def _moba(qat, ka, vat, bsz, seq):
    t, a_width = ka.shape
    blk = MOBA_BLOCK
    assert seq % blk == 0 and seq // blk >= MOBA_TOPK
    nb = seq // blk
    assert nb + 6 <= A_HEAD_DIM
    grp = MOBA_KEY_GROUP if nb % MOBA_KEY_GROUP == 0 else 1
    heads = MOBA_HEADS_PER_STEP
    width = heads * A_HEAD_DIM
    assert heads % 2 == 0 and a_width % width == 0
    tiles = grp
    n_streams = tiles * heads
    steps = nb // tiles
    slopes = jnp.exp2(-8.0 * jnp.arange(1, A_HEADS + 1, dtype=F32) / A_HEADS)
    slope_tbl = jnp.broadcast_to(slopes[:, None], (A_HEADS, blk))
    return pl.pallas_call(
        functools.partial(_moba_kernel, nb=nb, grp=grp, heads=heads, tiles=tiles),
        grid=(a_width // width, bsz, steps),
        in_specs=[
            pl.BlockSpec((A_HEADS, blk), lambda hg, b, c: (0, 0)),
            pl.BlockSpec((width, tiles * blk), lambda hg, b, c: (hg, b * steps + c)),
            pl.BlockSpec((seq, width), lambda hg, b, c: (b, hg)),
            pl.BlockSpec((width, seq), lambda hg, b, c: (hg, b)),
        ],
        out_specs=pl.BlockSpec((tiles * blk, width), lambda hg, b, c: (b * steps + c, hg)),
        out_shape=jax.ShapeDtypeStruct((t, a_width), BF16),
        scratch_shapes=[
            pltpu.VMEM((heads, seq, LANES), BF16),
            pltpu.VMEM((heads, nb // grp, V_AUG_ROWS, grp * blk), BF16),
            pltpu.VMEM((heads // 2, LANES, LANES), F32),
            pltpu.VMEM((n_streams, LANES, blk), BF16),
            pltpu.VMEM((n_streams, V_AUG_ROWS, blk), F32),
            pltpu.VMEM((n_streams, grp * blk, blk), F32),
            pltpu.VMEM((n_streams, grp * blk, blk), F32),
        ],
        compiler_params=_compiler_params(("parallel", "arbitrary", "arbitrary")),
        name="moba",
    )(slope_tbl, qat, ka, vat)


def _mlstm_kernel(qt_ref, k_ref, vt_ref, obt_ref, grow_ref, cwq_ref, cbq_ref, cwk_ref, cbk_ref,
                  o_ref, qtail_ref, kbuf_ref, ct_ref, m_ref):
    ci = pl.program_id(1)
    L, width = k_ref.shape
    d = B_HEAD_DIM
    halo = SUBLANES_F32

    @pl.when(ci == 0)
    def _():
        qtail_ref[...] = jnp.zeros_like(qtail_ref)
        kbuf_ref[0:halo, :] = jnp.zeros((halo, width), F32)
        ct_ref[...] = jnp.zeros_like(ct_ref)
        m_ref[...] = jnp.zeros_like(m_ref)

    xq = qt_ref[...].astype(F32)
    lane = lax.broadcasted_iota(jnp.int32, (width, LANES), 1)
    tail = qtail_ref[...]
    yq = cbq_ref[...] + xq * cwq_ref[CONV_WIDTH - 1]
    for s in range(1, CONV_WIDTH):
        shifted = pltpu.roll(xq, s, axis=1)
        first = jnp.where(lane < s, pltpu.roll(tail, s, axis=1), shifted[:, :LANES])
        shifted = jnp.concatenate([first, shifted[:, LANES:]], axis=1)
        yq = yq + shifted * cwq_ref[CONV_WIDTH - 1 - s]
    qtail_ref[...] = xq[:, L - LANES:]
    q_all = (yq * jax.nn.sigmoid(yq) * (d ** -0.5)).astype(BF16)

    kbuf_ref[halo:halo + L, :] = k_ref[...].astype(F32)
    yk = cbk_ref[...]
    for j in range(CONV_WIDTH):
        off = halo - (CONV_WIDTH - 1) + j
        yk = yk + kbuf_ref[off:off + L, :] * cwk_ref[j:j + 1, :]
    kbuf_ref[0:halo, :] = kbuf_ref[L:L + halo, :]
    k_all = (yk * jax.nn.sigmoid(yk)).astype(BF16)

    def log_sigmoid(z):
        return jnp.minimum(z, 0.0) - jnp.log1p(jnp.exp(-jnp.abs(z)))

    src = lax.broadcasted_iota(jnp.int32, (L, L), 0)
    dst = lax.broadcasted_iota(jnp.int32, (L, L), 1)
    causal = src <= dst
    tri_up = causal.astype(BF16)
    gates_row = grow_ref[0]
    gates_col = gates_row.T
    lf_hi, lf_lo = _split_bf16(log_sigmoid(gates_row), 2)
    b_rows = _dot(lf_hi, tri_up) + _dot(lf_lo, tri_up)
    b_cols = b_rows.T
    pad_rows = ct_ref.shape[1] - d
    ones_rows = jnp.where(lax.broadcasted_iota(jnp.int32, (pad_rows, L), 0) == 0, 1.0, 0.0).astype(BF16)
    for hd in range(B_HEADS):
        rows = slice(hd * d, (hd + 1) * d)
        qt = q_all[rows, :]
        k = k_all[:, rows]
        vt_aug = jnp.concatenate([vt_ref[rows, :], ones_rows], axis=0)
        i_col = gates_col[:, hd:hd + 1]
        i_row = gates_row[hd:hd + 1, :]
        b_col = b_cols[:, hd + B_HEADS:hd + B_HEADS + 1]
        b_row = b_rows[hd + B_HEADS:hd + B_HEADS + 1, :]

        m_prev = m_ref[hd]
        log_inter = b_row + m_prev
        dmat = jnp.where(causal, b_row + (i_col - b_col), -jnp.inf)
        m_t = jnp.maximum(log_inter, jnp.max(dmat, axis=0, keepdims=True))
        w_inter = jnp.exp(log_inter - m_t)
        sc = _dot(k, qt) * jnp.exp(dmat - m_t)
        ct_prev = ct_ref[hd]
        inter = w_inter * _dot(ct_prev.astype(BF16), qt)
        num = inter[:d, :] + _dot(vt_ref[rows, :], sc.astype(BF16))
        den = inter[d:d + 1, :] + jnp.sum(sc, axis=0, keepdims=True)
        hidden = num / jnp.maximum(jnp.abs(den), jnp.exp(-m_t))
        out_t = jax.nn.sigmoid(obt_ref[rows, :].astype(F32)) * hidden
        o_ref[:, rows] = out_t.T.astype(o_ref.dtype)

        b_last = b_row[:, L - 1:L]
        log_old = b_last + m_prev
        log_new = b_last - b_row + i_row
        m_new = jnp.maximum(log_old, jnp.max(log_new, axis=1, keepdims=True))
        a_old = jnp.exp(log_old - m_new)
        a_new = jnp.exp(log_new - m_new)
        ct_ref[hd] = a_old * ct_prev + _dot((vt_aug.astype(F32) * a_new).astype(BF16), k)
        m_ref[hd] = m_new


def _mlstm(qbt, kb, vbt, obt, gates, conv_w, conv_b, bsz, seq):
    b_width, t = qbt.shape
    d = B_HEAD_DIM
    L = min(MLSTM_KERNEL_CHUNK, seq)
    assert seq % L == 0 and L % LANES == 0 and d == LANES and b_width == B_HEADS * d
    nc = seq // L
    gates_row = gates.reshape(bsz, seq, 2 * B_HEADS).transpose(0, 2, 1)
    cwq = jnp.broadcast_to(conv_w[:, :b_width, None].astype(F32), (CONV_WIDTH, b_width, L))
    cbq = jnp.broadcast_to(conv_b[:b_width, None].astype(F32), (b_width, L))
    cwk = conv_w[:, b_width:].astype(F32)
    cbk = conv_b[b_width:].astype(F32).reshape(1, b_width)
    tposed = lambda b, c: (0, b * nc + c)
    natural = lambda b, c: (b * nc + c, 0)
    const2 = lambda b, c: (0, 0)
    return pl.pallas_call(
        _mlstm_kernel,
        grid=(bsz, nc),
        in_specs=[
            pl.BlockSpec((b_width, L), tposed),
            pl.BlockSpec((L, b_width), natural),
            pl.BlockSpec((b_width, L), tposed),
            pl.BlockSpec((b_width, L), tposed),
            pl.BlockSpec((1, 2 * B_HEADS, L), lambda b, c: (b, 0, c)),
            pl.BlockSpec((CONV_WIDTH, b_width, L), lambda b, c: (0, 0, 0)),
            pl.BlockSpec((b_width, L), const2),
            pl.BlockSpec((CONV_WIDTH, b_width), const2),
            pl.BlockSpec((1, b_width), const2),
        ],
        out_specs=pl.BlockSpec((L, b_width), natural),
        out_shape=jax.ShapeDtypeStruct((t, b_width), BF16),
        scratch_shapes=[
            pltpu.VMEM((b_width, LANES), F32),
            pltpu.VMEM((L + SUBLANES_F32, b_width), F32),
            pltpu.VMEM((B_HEADS, d + SUBLANES_BF16, d), F32),
            pltpu.VMEM((B_HEADS, 1, 1), F32),
        ],
        compiler_params=_compiler_params(("parallel", "arbitrary")),
        name="mlstm",
    )(qbt, kb, vbt, obt, gates_row, cwq, cbq, cwk, cbk)


def _pool_kernel(x_ref, halo_ref, g_ref, w_ref, scale_ref, o_ref, sums_ref, *, tiles_per_seq):
    i = pl.program_id(0)
    tm, d = x_ref.shape
    n_win = len(POOL_WINDOWS)
    grp = d // n_win
    base = POOL_PAD + POOL_HALO
    rows = POOL_HALO + tm
    x = x_ref[...]
    seq_tile = i % tiles_per_seq
    sums_ref[:, 0:POOL_PAD, :] = jnp.zeros((n_win, POOL_PAD, d), F32)
    halo_h = _rms_normalize(halo_ref[...], g_ref[...])
    sums_ref[0, POOL_PAD:base, :] = jnp.where(seq_tile == 0, 0.0, halo_h)
    h = _rms_normalize(x, g_ref[...])
    sums_ref[0, base:base + tm, :] = h
    t1 = (lax.broadcasted_iota(jnp.int32, (tm, 1), 0) + seq_tile * tm + 1).astype(F32)
    for k, win in enumerate(POOL_WINDOWS):
        half = win // 2
        c0 = k * grp
        if k + 1 < n_win:
            both = (sums_ref[k, POOL_PAD:POOL_PAD + rows, c0:]
                    + sums_ref[k, POOL_PAD - half:POOL_PAD - half + rows, c0:])
            sums_ref[k + 1, POOL_PAD:POOL_PAD + rows, c0:] = both
            total = both[POOL_HALO:, 0:grp]
        else:
            total = (sums_ref[k, base:base + tm, c0:c0 + grp]
                     + sums_ref[k, base - half:base - half + tm, c0:c0 + grp])
        pooled = total / jnp.minimum(t1, float(win)) - h[:, c0:c0 + grp]
        y = _dot(pooled.astype(BF16), w_ref[k])
        o_ref[:, c0:c0 + grp] = x[:, c0:c0 + grp] + y * scale_ref[:, c0:c0 + grp]


def _pool(x, gain, w_grp, scale, seq):
    t, d = x.shape
    tm = min(POOL_TOKEN_TILE, seq)
    assert seq % tm == 0 and tm % POOL_HALO == 0
    assert POOL_WINDOWS == tuple(2 ** (k + 1) for k in range(len(POOL_WINDOWS)))
    assert POOL_WINDOWS[-1] <= POOL_HALO and POOL_WINDOWS[-1] // 2 <= POOL_PAD
    n_grp, grp, _ = w_grp.shape
    halo_blocks = tm // POOL_HALO
    return pl.pallas_call(
        functools.partial(_pool_kernel, tiles_per_seq=seq // tm),
        grid=(t // tm,),
        in_specs=[
            pl.BlockSpec((tm, d), lambda i: (i, 0)),
            pl.BlockSpec((POOL_HALO, d), lambda i: (jnp.maximum(i * halo_blocks - 1, 0), 0)),
            pl.BlockSpec((1, d), lambda i: (0, 0)),
            pl.BlockSpec((n_grp, grp, grp), lambda i: (0, 0, 0)),
            pl.BlockSpec((1, d), lambda i: (0, 0)),
        ],
        out_specs=pl.BlockSpec((tm, d), lambda i: (i, 0)),
        out_shape=jax.ShapeDtypeStruct((t, d), F32),
        scratch_shapes=[pltpu.VMEM((len(POOL_WINDOWS), POOL_PAD + POOL_HALO + tm, d), F32)],
        compiler_params=_compiler_params(("parallel",)),
        name="pool",
    )(x, x, gain.reshape(1, d), w_grp.astype(BF16), scale.astype(F32).reshape(1, d))


def _mixer_heads(x, gain, w_in, g_q, g_k, conv_w, conv_b, b_i, b_f, bsz, seq):
    qat, ka, vat, kb, qbt, vbt, obt, gates = _inproj(x, gain, w_in, g_q, g_k, b_i, b_f)
    ya = _moba(qat, ka, vat, bsz, seq)
    yb = _mlstm(qbt, kb, vbt, obt, gates, conv_w, conv_b, bsz, seq)
    return ya, yb


def kernel(x, norm_g, ffn_w_gate, ffn_w_up, ffn_w_down, ab_w_in, ab_w_out, ab_g_q, ab_g_k,
           ab_conv_w, ab_conv_b, ab_b_i, ab_b_f, pool_w, pool_scale):
    bsz, seq, d = x.shape
    depth = norm_g.shape[0]
    y = x.reshape(bsz * seq, d)
    w_gate, w_up, w_down = (w.astype(BF16) for w in (ffn_w_gate, ffn_w_up, ffn_w_down))
    for layer in range(depth):
        y = _ffn(y, norm_g[layer, 0], w_gate, w_up, w_down, (layer, 0))
        mixer_out = None
        if layer % 2 == 0:
            e = layer // 2
            ya, yb = _mixer_heads(y, norm_g[layer, 1], ab_w_in[e], ab_g_q[e], ab_g_k[e],
                                  ab_conv_w[e], ab_conv_b[e], ab_b_i[e], ab_b_f[e], bsz, seq)
            mixer_out = (ya, yb, ab_w_out[e])
        else:
            o = layer // 2
            y = _pool(y, norm_g[layer, 1], pool_w[o], pool_scale[o], seq)
        y = _ffn(y, norm_g[layer, 2], w_gate, w_up, w_down, (layer, 1), mixer_out=mixer_out)
    return y.reshape(bsz, seq, d)
```

```python
import functools

import jax
import jax.numpy as jnp
from jax import lax
from jax.experimental import pallas as pl
from jax.experimental.pallas import tpu as pltpu

F32 = jnp.float32
BF16 = jnp.bfloat16

LANES = 128
SUBLANES_F32 = 8
SUBLANES_BF16 = 16
MXU_WIDTH = 256
VMEM_BYTES = 64 * 1024 * 1024
VMEM_LIMIT_BYTES = VMEM_BYTES * 7 // 8

RMS_EPS = 1e-6
A_HEADS = 8
A_HEAD_DIM = 64
MOBA_BLOCK = 256
MOBA_TOPK = 3
B_HEADS = 4
B_HEAD_DIM = 128
CONV_WIDTH = 4
POOL_WINDOWS = (2, 4, 8, 16)

MOBA_KEY_GROUP = 2
MOBA_HEADS_PER_STEP = 4
MLSTM_KERNEL_CHUNK = 256
POOL_HALO = 16
POOL_PAD = SUBLANES_F32
MASK_VALUE = -1e30
LOG2E = 1.4426950408889634
V_AUG_ROWS = A_HEAD_DIM + SUBLANES_BF16
FFN_TOKEN_TILE = 1024
FFN_SUB_TILE = 512
FFN_HIDDEN_TILE = MXU_WIDTH
PROJ_TOKEN_TILE = 1024
POOL_TOKEN_TILE = 1024


def _compiler_params(semantics):
    return pltpu.CompilerParams(dimension_semantics=semantics,
                                vmem_limit_bytes=VMEM_LIMIT_BYTES)


def _rms_normalize(x, gain):
    ms = jnp.mean(x * x, axis=-1, keepdims=True)
    return x * lax.rsqrt(ms + RMS_EPS) * gain


def _dot(a, b):
    return jnp.dot(a, b, preferred_element_type=F32)


def _dot_nt(a, b):
    return lax.dot_general(a, b, (((1,), (1,)), ((), ())), preferred_element_type=F32)


def _split_bf16(x, parts):
    out = []
    rem = x
    for _ in range(parts):
        p = rem.astype(BF16)
        out.append(p)
        rem = rem - p.astype(F32)
    return out


def _ffn_kernel(x_ref, g_ref, wg_ref, wu_ref, wd_ref, *rest, hidden_tile, sub_tile):
    o_ref = rest[-1]
    d_ff = wg_ref.shape[1]
    for r0 in range(0, x_ref.shape[0], sub_tile):
        x = x_ref[r0:r0 + sub_tile, :]
        if len(rest) > 1:
            ya_ref, yb_ref, wa_ref, wb_ref = rest[:-1]
            x = (x + _dot(ya_ref[r0:r0 + sub_tile, :], wa_ref[...])
                 + _dot(yb_ref[r0:r0 + sub_tile, :], wb_ref[...]))
        h = (x * g_ref[...]).astype(BF16)
        inv_rms = lax.rsqrt(jnp.mean(x * x, axis=-1, keepdims=True) + RMS_EPS)
        acc = None
        for c0 in range(0, d_ff, hidden_tile):
            gate = _dot(h, wg_ref[:, c0:c0 + hidden_tile]) * inv_rms
            up = _dot(h, wu_ref[:, c0:c0 + hidden_tile])
            act = (gate * jax.nn.sigmoid(gate) * up).astype(BF16)
            part = _dot(act, wd_ref[c0:c0 + hidden_tile, :])
            acc = part if acc is None else acc + part
        o_ref[r0:r0 + sub_tile, :] = x + (0.5 * inv_rms) * acc


def _resident(shape):
    return pl.BlockSpec(shape, lambda *_: (0,) * len(shape), pipeline_mode=pl.Buffered(1))


def _ffn(x, gain, w_gate, w_up, w_down, which, mixer_out=None):
    t, d = x.shape
    d_ff = w_gate.shape[-1]
    tm, tf = FFN_TOKEN_TILE, FFN_HIDDEN_TILE
    assert t % tm == 0 and d_ff % tf == 0 and tm % FFN_SUB_TILE == 0
    row = lambda i: (i, 0)
    picked = lambda r, c: pl.BlockSpec((None, None, r, c), lambda i: (*which, 0, 0),
                                       pipeline_mode=pl.Buffered(1))
    in_specs = [pl.BlockSpec((tm, d), row), _resident((1, d)), picked(d, d_ff), picked(d, d_ff),
                picked(d_ff, d)]
    args = [x, gain.reshape(1, d), w_gate, w_up, w_down]
    if mixer_out is not None:
        ya, yb, w_out = mixer_out
        wa, wb = ya.shape[1], yb.shape[1]
        in_specs += [pl.BlockSpec((tm, wa), row), pl.BlockSpec((tm, wb), row),
                     _resident((wa, d)), _resident((wb, d))]
        args += [ya, yb, w_out[:wa].astype(BF16), w_out[wa:].astype(BF16)]
    return pl.pallas_call(
        functools.partial(_ffn_kernel, hidden_tile=tf, sub_tile=FFN_SUB_TILE),
        grid=(t // tm,),
        in_specs=in_specs,
        out_specs=pl.BlockSpec((tm, d), row),
        out_shape=jax.ShapeDtypeStruct((t, d), F32),
        compiler_params=_compiler_params(("parallel",)),
        name="ffn",
    )(*args)


def _inproj_kernel(x_ref, g_ref, w_ref, wqt_ref, wrt_ref, gqcol_ref, gk_ref, grp_ref,
                   gbias_ref, qat_ref, ka_ref, vat_ref, kb_ref, qbt_ref, vbt_ref, obt_ref, gates_ref,
                   *, a_width, b_width):
    h = _rms_normalize(x_ref[...], g_ref[...]).astype(BF16)

    qt = _dot_nt(wqt_ref[...], h)
    for hd in range(a_width // A_HEAD_DIM):
        rows = slice(hd * A_HEAD_DIM, (hd + 1) * A_HEAD_DIM)
        y = qt[rows, :]
        ms = jnp.mean(y * y, axis=0, keepdims=True)
        gain = gqcol_ref[rows, :] * (A_HEAD_DIM ** -0.5 * LOG2E)
        qat_ref[rows, :] = (y * lax.rsqrt(ms + RMS_EPS) * gain).astype(BF16)

    y = _dot(h, w_ref[:, 0:a_width])
    sq_hi, sq_lo = _split_bf16(y * y, 2)
    ssq = _dot(sq_hi, grp_ref[...]) + _dot(sq_lo, grp_ref[...])
    ka_ref[...] = (y * lax.rsqrt(ssq * (1.0 / A_HEAD_DIM) + RMS_EPS) * gk_ref[...]).astype(BF16)

    kb_ref[...] = _dot(h, w_ref[:, a_width:a_width + b_width]).astype(BF16)
    rest_t = _dot_nt(wrt_ref[...], h)
    vat_ref[...] = rest_t[0:a_width, :].astype(BF16)
    for piece, out_ref in enumerate((qbt_ref, vbt_ref, obt_ref)):
        r0 = a_width + piece * b_width
        out_ref[...] = rest_t[r0:r0 + b_width, :].astype(BF16)
    r0 = a_width + 3 * b_width
    gates_ref[...] = rest_t[r0:r0 + 2 * B_HEADS, :] + gbias_ref[...]


def _inproj(x, gain, w_in, g_q, g_k, b_i, b_f):
    t, d = x.shape
    a_width = A_HEADS * A_HEAD_DIM
    b_width = B_HEADS * B_HEAD_DIM
    n_main = a_width + b_width
    o_qb = 3 * a_width
    o_ib = 3 * a_width + 3 * b_width
    tm = PROJ_TOKEN_TILE
    assert t % tm == 0
    w_main = jnp.concatenate([w_in[:, a_width:2 * a_width], w_in[:, o_qb + b_width:o_qb + 2 * b_width]],
                             axis=1).astype(BF16)
    w_qt = w_in[:, :a_width].T.astype(BF16)
    n_gate_rows = SUBLANES_BF16
    w_rt = jnp.concatenate([w_in[:, 2 * a_width:3 * a_width], w_in[:, o_qb:o_qb + b_width],
                            w_in[:, o_qb + 2 * b_width:o_ib], w_in[:, o_ib + 2 * B_HEADS:],
                            w_in[:, o_ib:o_ib + 2 * B_HEADS],
                            jnp.zeros((d, n_gate_rows - 2 * B_HEADS), w_in.dtype)], axis=1).T.astype(BF16)
    gate_bias = jnp.broadcast_to(jnp.concatenate([b_i, b_f]).astype(F32)[:, None], (2 * B_HEADS, tm))
    head_id = jnp.arange(a_width) // A_HEAD_DIM
    grp = (head_id[:, None] == head_id[None, :]).astype(BF16)
    gq_col = jnp.tile(g_q.astype(F32), A_HEADS).reshape(a_width, 1)
    gk = jnp.tile(g_k.astype(F32), A_HEADS).reshape(1, a_width)
    const = lambda i: (0, 0)
    row = lambda i: (i, 0)
    col = lambda i: (0, i)
    outs = pl.pallas_call(
        functools.partial(_inproj_kernel, a_width=a_width, b_width=b_width),
        grid=(t // tm,),
        in_specs=[
            pl.BlockSpec((tm, d), row),
            pl.BlockSpec((1, d), const),
            pl.BlockSpec((d, n_main), const),
            pl.BlockSpec((a_width, d), const),
            pl.BlockSpec((a_width + 3 * b_width + n_gate_rows, d), const),
            pl.BlockSpec((a_width, 1), const),
            pl.BlockSpec((1, a_width), const),
            pl.BlockSpec((a_width, a_width), const),
            pl.BlockSpec((2 * B_HEADS, tm), const),
        ],
        out_specs=[
            pl.BlockSpec((a_width, tm), col),
            pl.BlockSpec((tm, a_width), row),
            pl.BlockSpec((a_width, tm), col),
            pl.BlockSpec((tm, b_width), row),
            pl.BlockSpec((b_width, tm), col),
            pl.BlockSpec((b_width, tm), col),
            pl.BlockSpec((b_width, tm), col),
            pl.BlockSpec((2 * B_HEADS, tm), col),
        ],
        out_shape=[
            jax.ShapeDtypeStruct((a_width, t), BF16),
            jax.ShapeDtypeStruct((t, a_width), BF16),
            jax.ShapeDtypeStruct((a_width, t), BF16),
            jax.ShapeDtypeStruct((t, b_width), BF16),
            jax.ShapeDtypeStruct((b_width, t), BF16),
            jax.ShapeDtypeStruct((b_width, t), BF16),
            jax.ShapeDtypeStruct((b_width, t), BF16),
            jax.ShapeDtypeStruct((2 * B_HEADS, t), F32),
        ],
        compiler_params=_compiler_params(("parallel",)),
        name="inproj",
    )(x, gain.reshape(1, d), w_main, w_qt, w_rt, gq_col, gk, grp, gate_bias)
    return outs


def _moba_kernel(slope_ref, qt_ref, k_ref, vt_ref, o_ref, kaug_ref, vaug_ref, kmean_ref, qaug_ref, acc_ref,
                 sa_ref, sb_ref, *, nb, grp, heads, tiles):
    hg = pl.program_id(0)
    batch = pl.program_id(1)
    c0 = pl.program_id(2) * tiles
    blk = MOBA_BLOCK
    dh = A_HEAD_DIM
    nbp = pl.cdiv(nb, SUBLANES_BF16) * SUBLANES_BF16
    streams = [(ti, h) for ti in range(tiles) for h in range(heads)]

    @pl.when(c0 == 0)
    def _build_key_value_side():
        ones_rows = jnp.where(lax.broadcasted_iota(jnp.int32, (V_AUG_ROWS - dh, blk), 0) == 0,
                              1.0, 0.0).astype(BF16)
        for h in range(heads):
            for j in range(nb):
                g, off = divmod(j, grp)
                vaug_ref[h, g, 0:dh, off * blk:(off + 1) * blk] = vt_ref[dh * h:dh * (h + 1),
                                                                         j * blk:(j + 1) * blk]
                vaug_ref[h, g, dh:V_AUG_ROWS, off * blk:(off + 1) * blk] = ones_rows
        kmean_ref[...] = jnp.zeros_like(kmean_ref)
        lane = lax.broadcasted_iota(jnp.int32, (blk, LANES), 1)
        row = lax.broadcasted_iota(jnp.int32, (blk, LANES), 0)

        def key_block(j, carry, *, with_extra_terms):
            start = pl.multiple_of(j * blk, blk)
            pos = (row + j * blk).astype(F32)
            for pr in range(heads // 2):
                kb = k_ref[pl.ds(start, blk), pr * LANES:(pr + 1) * LANES]
                kmean_ref[pr, pl.ds(j, 1), :] = jnp.mean(kb.astype(F32), axis=0, keepdims=True)
                for hh in range(2):
                    h = 2 * pr + hh
                    is_data = (lane >= dh * hh) & (lane < dh * (hh + 1))
                    if with_extra_terms:
                        slope = slope_ref[pl.ds(heads * hg + h, 1), :][:, 0:LANES] * LOG2E
                        p1, p2, p3 = _split_bf16(slope * pos, 3)
                        rel = lane - dh * (1 - hh)
                        aug = jnp.where(rel == j, 1.0, 0.0)
                        aug = jnp.where(rel == nb, p1.astype(F32), aug)
                        aug = jnp.where(rel == nb + 1, p2.astype(F32), aug)
                        aug = jnp.where(rel == nb + 2, p3.astype(F32), aug)
                        aug = jnp.where((rel >= nb + 3) & (rel < nb + 6), 1.0, aug).astype(BF16)
                    else:
                        aug = kaug_ref[h, pl.ds(start, blk), :]
                    kaug_ref[h, pl.ds(start, blk), :] = jnp.where(is_data, kb, aug)
            return carry

        @pl.when(batch == 0)
        def _():
            lax.fori_loop(0, nb, functools.partial(key_block, with_extra_terms=True), 0)

        @pl.when(batch != 0)
        def _():
            lax.fori_loop(0, nb, functools.partial(key_block, with_extra_terms=False), 0)

    blk_ix = lax.broadcasted_iota(jnp.int32, (nbp, blk), 0)
    blk_f = blk_ix.astype(F32)
    aug_row = lax.broadcasted_iota(jnp.int32, (dh, blk), 0)
    qry_lane = lax.broadcasted_iota(jnp.int32, (dh, blk), 1)
    lane_k = lax.broadcasted_iota(jnp.int32, (nbp, LANES), 1)
    own_grp = c0 // grp
    key_ix = lax.broadcasted_iota(jnp.int32, (grp * blk, blk), 0) + own_grp * (grp * blk)
    qry_ix = lax.broadcasted_iota(jnp.int32, (grp * blk, blk), 1)
    gates = []
    for ti, h in streams:
        pr, hh = divmod(h, 2)
        qt_pair = qt_ref[pr * LANES:(pr + 1) * LANES, ti * blk:(ti + 1) * blk]
        is_data_k = (lane_k >= dh * hh) & (lane_k < dh * (hh + 1))
        km_hi, km_lo = _split_bf16(jnp.where(is_data_k, kmean_ref[pr, 0:nbp, :], 0.0), 2)
        gates.append(_dot(km_hi, qt_pair) + _dot(km_lo, qt_pair))
    for si, (ti, h) in enumerate(streams):
        pr, hh = divmod(h, 2)
        c = c0 + ti
        valid = blk_ix < c
        tq = (qry_lane + c * blk).astype(F32)
        gate = jnp.where(valid, gates[si], -jnp.inf)
        chosen = blk_ix == c
        for _ in range(MOBA_TOPK):
            best = jnp.max(gate, axis=0, keepdims=True)
            first = jnp.min(jnp.where(gate == best, blk_f, float(nbp)), axis=0, keepdims=True)
            pick = blk_f == first
            chosen = chosen | (pick & valid)
            gate = jnp.where(pick, -jnp.inf, gate)
        bias = jnp.where(chosen, 0.0, MASK_VALUE)
        if nbp < dh:
            bias = jnp.concatenate([bias, jnp.zeros((dh - nbp, blk), F32)], axis=0)
        slope = slope_ref[pl.ds(heads * hg + h, 1), :] * LOG2E
        t1, t2, t3 = _split_bf16(-slope * tq, 3)
        aug = jnp.where(aug_row < nb, bias, 0.0)
        aug = jnp.where((aug_row >= nb) & (aug_row < nb + 3), 1.0, aug)
        aug = jnp.where(aug_row == nb + 3, t1.astype(F32), aug)
        aug = jnp.where(aug_row == nb + 4, t2.astype(F32), aug)
        aug = jnp.where(aug_row == nb + 5, t3.astype(F32), aug).astype(BF16)
        data = qt_ref[dh * h:dh * (h + 1), ti * blk:(ti + 1) * blk]
        qaug_ref[si] = jnp.concatenate([data, aug] if hh == 0 else [aug, data], axis=0)

    def scores_into(buf_ref, g):
        start = pl.multiple_of(g * (grp * blk), grp * blk)
        col_max = []
        for si, (ti, h) in enumerate(streams):
            s = _dot(kaug_ref[h, pl.ds(start, grp * blk), :], qaug_ref[si])
            buf_ref[si] = s
            col_max.append(jnp.max(s, axis=0, keepdims=True))
        return tuple(col_max)

    def consume(buf_ref, g, ms, col_max, own=False):
        new_ms = []
        for si, (ti, h) in enumerate(streams):
            s = buf_ref[si]
            if own:
                s = jnp.where(key_ix <= qry_ix + (c0 + ti) * blk, s, MASK_VALUE)
                group_max = jnp.max(s, axis=0, keepdims=True)
            else:
                group_max = col_max[si]
            m_new = jnp.maximum(ms[si], group_max)
            alpha = jnp.exp2(ms[si] - m_new)
            p = jnp.exp2(s - m_new)
            acc_ref[si] = alpha * acc_ref[si] + _dot(vaug_ref[h, g], p.astype(BF16))
            new_ms.append(m_new)
        return tuple(new_ms)

    def write_output():
        for ti in range(tiles):
            outs = []
            for h in range(heads):
                acc = acc_ref[ti * heads + h]
                outs.append(acc[0:dh, :] / acc[dh:dh + 1, :])
            o_ref[ti * blk:(ti + 1) * blk, :] = jnp.concatenate(outs, axis=0).T.astype(o_ref.dtype)

    acc_ref[...] = jnp.zeros_like(acc_ref)
    masked_score = jnp.full((1, blk), MASK_VALUE, F32).astype(BF16).astype(F32)
    max_a0 = scores_into(sa_ref, 0)

    def two_groups(i, carry):
        ms, max_a = carry
        max_b = scores_into(sb_ref, 2 * i + 1)
        ms = consume(sa_ref, 2 * i, ms, max_a)
        max_a = scores_into(sa_ref, 2 * i + 2)
        return consume(sb_ref, 2 * i + 1, ms, max_b), max_a

    ms, max_a = lax.fori_loop(0, own_grp // 2, two_groups, ((masked_score,) * len(streams), max_a0))

    @pl.when(own_grp % 2 == 0)
    def _own_group_in_a():
        consume(sa_ref, own_grp, ms, None, own=True)
        write_output()

    @pl.when(own_grp % 2 == 1)
    def _own_group_in_b():
        scores_into(sb_ref, own_grp)
        consume(sb_ref, own_grp, consume(sa_ref, own_grp - 1, ms, max_a), None, own=True)
        write_output()


def _moba(qat, ka, vat, bsz, seq):
    t, a_width = ka.shape
    blk = MOBA_BLOCK
    assert seq % blk == 0 and seq // blk >= MOBA_TOPK
    nb = seq // blk
    assert nb + 6 <= A_HEAD_DIM
    grp = MOBA_KEY_GROUP if nb % MOBA_KEY_GROUP == 0 else 1
    heads = MOBA_HEADS_PER_STEP
    width = heads * A_HEAD_DIM
    assert heads % 2 == 0 and a_width % width == 0
    tiles = grp
    n_streams = tiles * heads
    steps = nb // tiles
    slopes = jnp.exp2(-8.0 * jnp.arange(1, A_HEADS + 1, dtype=F32) / A_HEADS)
    slope_tbl = jnp.broadcast_to(slopes[:, None], (A_HEADS, blk))
    return pl.pallas_call(
        functools.partial(_moba_kernel, nb=nb, grp=grp, heads=heads, tiles=tiles),
        grid=(a_width // width, bsz, steps),
        in_specs=[
            pl.BlockSpec((A_HEADS, blk), lambda hg, b, c: (0, 0)),
            pl.BlockSpec((width, tiles * blk), lambda hg, b, c: (hg, b * steps + c)),
            pl.BlockSpec((seq, width), lambda hg, b, c: (b, hg)),
            pl.BlockSpec((width, seq), lambda hg, b, c: (hg, b)),
        ],
        out_specs=pl.BlockSpec((tiles * blk, width), lambda hg, b, c: (b * steps + c, hg)),
        out_shape=jax.ShapeDtypeStruct((t, a_width), BF16),
        scratch_shapes=[
            pltpu.VMEM((heads, seq, LANES), BF16),
            pltpu.VMEM((heads, nb // grp, V_AUG_ROWS, grp * blk), BF16),
            pltpu.VMEM((heads // 2, LANES, LANES), F32),
            pltpu.VMEM((n_streams, LANES, blk), BF16),
            pltpu.VMEM((n_streams, V_AUG_ROWS, blk), F32),
            pltpu.VMEM((n_streams, grp * blk, blk), F32),
            pltpu.VMEM((n_streams, grp * blk, blk), F32),
        ],
        compiler_params=_compiler_params(("parallel", "arbitrary", "arbitrary")),
        name="moba",
    )(slope_tbl, qat, ka, vat)


def _mlstm_kernel(qt_ref, k_ref, vt_ref, obt_ref, grow_ref, cwq_ref, cbq_ref, cwk_ref, cbk_ref,
                  o_ref, qtail_ref, kbuf_ref, ct_ref, m_ref):
    ci = pl.program_id(1)
    L, width = k_ref.shape
    d = B_HEAD_DIM
    halo = SUBLANES_F32

    @pl.when(ci == 0)
    def _():
        qtail_ref[...] = jnp.zeros_like(qtail_ref)
        kbuf_ref[0:halo, :] = jnp.zeros((halo, width), F32)
        ct_ref[...] = jnp.zeros_like(ct_ref)
        m_ref[...] = jnp.zeros_like(m_ref)

    xq = qt_ref[...].astype(F32)
    lane = lax.broadcasted_iota(jnp.int32, (width, LANES), 1)
    tail = qtail_ref[...]
    yq = cbq_ref[...] + xq * cwq_ref[CONV_WIDTH - 1]
    for s in range(1, CONV_WIDTH):
        shifted = pltpu.roll(xq, s, axis=1)
        first = jnp.where(lane < s, pltpu.roll(tail, s, axis=1), shifted[:, :LANES])
        shifted = jnp.concatenate([first, shifted[:, LANES:]], axis=1)
        yq = yq + shifted * cwq_ref[CONV_WIDTH - 1 - s]
    qtail_ref[...] = xq[:, L - LANES:]
    q_all = (yq * jax.nn.sigmoid(yq) * (d ** -0.5)).astype(BF16)

    kbuf_ref[halo:halo + L, :] = k_ref[...].astype(F32)
    yk = cbk_ref[...]
    for j in range(CONV_WIDTH):
        off = halo - (CONV_WIDTH - 1) + j
        yk = yk + kbuf_ref[off:off + L, :] * cwk_ref[j:j + 1, :]
    kbuf_ref[0:halo, :] = kbuf_ref[L:L + halo, :]
    k_all = (yk * jax.nn.sigmoid(yk)).astype(BF16)

    def log_sigmoid(z):
        return jnp.minimum(z, 0.0) - jnp.log1p(jnp.exp(-jnp.abs(z)))

    src = lax.broadcasted_iota(jnp.int32, (L, L), 0)
    dst = lax.broadcasted_iota(jnp.int32, (L, L), 1)
    causal = src <= dst
    tri_up = causal.astype(BF16)
    gates_row = grow_ref[...]
    gates_col = gates_row.T
    lf_hi, lf_lo = _split_bf16(log_sigmoid(gates_row), 2)
    b_rows = _dot(lf_hi, tri_up) + _dot(lf_lo, tri_up)
    b_cols = b_rows.T
    pad_rows = ct_ref.shape[1] - d
    ones_rows = jnp.where(lax.broadcasted_iota(jnp.int32, (pad_rows, L), 0) == 0, 1.0, 0.0).astype(BF16)
    for hd in range(B_HEADS):
        rows = slice(hd * d, (hd + 1) * d)
        qt = q_all[rows, :]
        k = k_all[:, rows]
        vt_aug = jnp.concatenate([vt_ref[rows, :], ones_rows], axis=0)
        i_col = gates_col[:, hd:hd + 1]
        i_row = gates_row[hd:hd + 1, :]
        b_col = b_cols[:, hd + B_HEADS:hd + B_HEADS + 1]
        b_row = b_rows[hd + B_HEADS:hd + B_HEADS + 1, :]

        m_prev = m_ref[hd]
        log_inter = b_row + m_prev
        dmat = jnp.where(causal, b_row + (i_col - b_col), -jnp.inf)
        m_t = jnp.maximum(log_inter, jnp.max(dmat, axis=0, keepdims=True))
        w_inter = jnp.exp(log_inter - m_t)
        sc = _dot(k, qt) * jnp.exp(dmat - m_t)
        ct_prev = ct_ref[hd]
        inter = w_inter * _dot(ct_prev.astype(BF16), qt)
        num = inter[:d, :] + _dot(vt_ref[rows, :], sc.astype(BF16))
        den = inter[d:d + 1, :] + jnp.sum(sc, axis=0, keepdims=True)
        hidden = num / jnp.maximum(jnp.abs(den), jnp.exp(-m_t))
        out_t = jax.nn.sigmoid(obt_ref[rows, :].astype(F32)) * hidden
        o_ref[:, rows] = out_t.T.astype(o_ref.dtype)

        b_last = b_row[:, L - 1:L]
        log_old = b_last + m_prev
        log_new = b_last - b_row + i_row
        m_new = jnp.maximum(log_old, jnp.max(log_new, axis=1, keepdims=True))
        a_old = jnp.exp(log_old - m_new)
        a_new = jnp.exp(log_new - m_new)
        ct_ref[hd] = a_old * ct_prev + _dot((vt_aug.astype(F32) * a_new).astype(BF16), k)
        m_ref[hd] = m_new


def _mlstm(qbt, kb, vbt, obt, gates, conv_w, conv_b, bsz, seq):
    b_width, t = qbt.shape
    d = B_HEAD_DIM
    L = min(MLSTM_KERNEL_CHUNK, seq)
    assert seq % L == 0 and L % LANES == 0 and d == LANES and b_width == B_HEADS * d
    nc = seq // L
    cwq = jnp.broadcast_to(conv_w[:, :b_width, None].astype(F32), (CONV_WIDTH, b_width, L))
    cbq = jnp.broadcast_to(conv_b[:b_width, None].astype(F32), (b_width, L))
    cwk = conv_w[:, b_width:].astype(F32)
    cbk = conv_b[b_width:].astype(F32).reshape(1, b_width)
    tposed = lambda b, c: (0, b * nc + c)
    natural = lambda b, c: (b * nc + c, 0)
    const2 = lambda b, c: (0, 0)
    return pl.pallas_call(
        _mlstm_kernel,
        grid=(bsz, nc),
        in_specs=[
            pl.BlockSpec((b_width, L), tposed),
            pl.BlockSpec((L, b_width), natural),
            pl.BlockSpec((b_width, L), tposed),
            pl.BlockSpec((b_width, L), tposed),
            pl.BlockSpec((2 * B_HEADS, L), tposed),
            pl.BlockSpec((CONV_WIDTH, b_width, L), lambda b, c: (0, 0, 0)),
            pl.BlockSpec((b_width, L), const2),
            pl.BlockSpec((CONV_WIDTH, b_width), const2),
            pl.BlockSpec((1, b_width), const2),
        ],
        out_specs=pl.BlockSpec((L, b_width), natural),
        out_shape=jax.ShapeDtypeStruct((t, b_width), BF16),
        scratch_shapes=[
            pltpu.VMEM((b_width, LANES), F32),
            pltpu.VMEM((L + SUBLANES_F32, b_width), F32),
            pltpu.VMEM((B_HEADS, d + SUBLANES_BF16, d), F32),
            pltpu.VMEM((B_HEADS, 1, 1), F32),
        ],
        compiler_params=_compiler_params(("parallel", "arbitrary")),
        name="mlstm",
    )(qbt, kb, vbt, obt, gates, cwq, cbq, cwk, cbk)


def _pool_kernel(x_ref, halo_ref, g_ref, w_ref, scale_ref, o_ref, sums_ref, *, tiles_per_seq):
    i = pl.program_id(0)
    tm, d = x_ref.shape
    n_win = len(POOL_WINDOWS)
    grp = d // n_win
    base = POOL_PAD + POOL_HALO
    rows = POOL_HALO + tm
    x = x_ref[...]
    seq_tile = i % tiles_per_seq
    sums_ref[:, 0:POOL_PAD, :] = jnp.zeros((n_win, POOL_PAD, d), F32)
    halo_h = _rms_normalize(halo_ref[...], g_ref[...])
    sums_ref[0, POOL_PAD:base, :] = jnp.where(seq_tile == 0, 0.0, halo_h)
    h = _rms_normalize(x, g_ref[...])
    sums_ref[0, base:base + tm, :] = h
    t1 = (lax.broadcasted_iota(jnp.int32, (tm, 1), 0) + seq_tile * tm + 1).astype(F32)
    for k, win in enumerate(POOL_WINDOWS):
        half = win // 2
        c0 = k * grp
        if k + 1 < n_win:
            both = (sums_ref[k, POOL_PAD:POOL_PAD + rows, c0:]
                    + sums_ref[k, POOL_PAD - half:POOL_PAD - half + rows, c0:])
            sums_ref[k + 1, POOL_PAD:POOL_PAD + rows, c0:] = both
            total = both[POOL_HALO:, 0:grp]
        else:
            total = (sums_ref[k, base:base + tm, c0:c0 + grp]
                     + sums_ref[k, base - half:base - half + tm, c0:c0 + grp])
        pooled = total / jnp.minimum(t1, float(win)) - h[:, c0:c0 + grp]
        y = _dot(pooled.astype(BF16), w_ref[k])
        o_ref[:, c0:c0 + grp] = x[:, c0:c0 + grp] + y * scale_ref[:, c0:c0 + grp]


def _pool(x, gain, w_grp, scale, seq):
    t, d = x.shape
    tm = min(POOL_TOKEN_TILE, seq)
    assert seq % tm == 0 and tm % POOL_HALO == 0
    assert POOL_WINDOWS == tuple(2 ** (k + 1) for k in range(len(POOL_WINDOWS)))
    assert POOL_WINDOWS[-1] <= POOL_HALO and POOL_WINDOWS[-1] // 2 <= POOL_PAD
    n_grp, grp, _ = w_grp.shape
    halo_blocks = tm // POOL_HALO
    return pl.pallas_call(
        functools.partial(_pool_kernel, tiles_per_seq=seq // tm),
        grid=(t // tm,),
        in_specs=[
            pl.BlockSpec((tm, d), lambda i: (i, 0)),
            pl.BlockSpec((POOL_HALO, d), lambda i: (jnp.maximum(i * halo_blocks - 1, 0), 0)),
            pl.BlockSpec((1, d), lambda i: (0, 0)),
            pl.BlockSpec((n_grp, grp, grp), lambda i: (0, 0, 0)),
            pl.BlockSpec((1, d), lambda i: (0, 0)),
        ],
        out_specs=pl.BlockSpec((tm, d), lambda i: (i, 0)),
        out_shape=jax.ShapeDtypeStruct((t, d), F32),
        scratch_shapes=[pltpu.VMEM((len(POOL_WINDOWS), POOL_PAD + POOL_HALO + tm, d), F32)],
        compiler_params=_compiler_params(("parallel",)),
        name="pool",
    )(x, x, gain.reshape(1, d), w_grp.astype(BF16), scale.astype(F32).reshape(1, d))


def _mixer_heads(x, gain, w_in, g_q, g_k, conv_w, conv_b, b_i, b_f, bsz, seq):
    qat, ka, vat, kb, qbt, vbt, obt, gates = _inproj(x, gain, w_in, g_q, g_k, b_i, b_f)
    ya = _moba(qat, ka, vat, bsz, seq)
    yb = _mlstm(qbt, kb, vbt, obt, gates, conv_w, conv_b, bsz, seq)
    return ya, yb


def kernel(x, norm_g, ffn_w_gate, ffn_w_up, ffn_w_down, ab_w_in, ab_w_out, ab_g_q, ab_g_k,
           ab_conv_w, ab_conv_b, ab_b_i, ab_b_f, pool_w, pool_scale):
    bsz, seq, d = x.shape
    depth = norm_g.shape[0]
    y = x.reshape(bsz * seq, d)
    w_gate, w_up, w_down = (w.astype(BF16) for w in (ffn_w_gate, ffn_w_up, ffn_w_down))
    for layer in range(depth):
        y = _ffn(y, norm_g[layer, 0], w_gate, w_up, w_down, (layer, 0))
        mixer_out = None
        if layer % 2 == 0:
            e = layer // 2
            ya, yb = _mixer_heads(y, norm_g[layer, 1], ab_w_in[e], ab_g_q[e], ab_g_k[e],
                                  ab_conv_w[e], ab_conv_b[e], ab_b_i[e], ab_b_f[e], bsz, seq)
            mixer_out = (ya, yb, ab_w_out[e])
        else:
            o = layer // 2
            y = _pool(y, norm_g[layer, 1], pool_w[o], pool_scale[o], seq)
        y = _ffn(y, norm_g[layer, 2], w_gate, w_up, w_down, (layer, 1), mixer_out=mixer_out)
    return y.reshape(bsz, seq, d)
```
